```python
import math
import jax, jax.numpy as jnp
from jax import lax
import numpy as np

D_MODEL = 1024
BATCH = 8
SEQ = 2048
DEPTH = 2
DEC_BATCH = 128
DEC_SEQ = 8
PAST_LEN = 16384
PAGE_SIZE = 128

BRANCH_W = D_MODEL // 2
N_BRANCH = 4
HG_HEADS = 4
HG_DK = BRANCH_W // HG_HEADS
HG_DV = BRANCH_W // HG_HEADS
HG_CHUNK = 32
RW_HEAD = 64
RW_HEADS = BRANCH_W // RW_HEAD
RW_W_RANK = 64
RW_A_RANK = 64
RW_G_RANK = 128
RW_DECAY_SCALE = 0.606531
RW_GN_EPS = 64e-5
CF_WIDTH = 31
LRU_BLOCKS = 8
LRU_BW = BRANCH_W // LRU_BLOCKS
LRU_CONV = 4
LRU_C = 8.0
MLP_HIDDEN = 4 * D_MODEL
EPS = 1e-6

HG_COLS = 4 * BRANCH_W
RW_COLS = 3 * BRANCH_W + RW_W_RANK + RW_A_RANK + RW_G_RANK
CF_COLS = 2 * BRANCH_W
LRU_COLS = 2 * BRANCH_W
IN_COLS = HG_COLS + RW_COLS + CF_COLS + LRU_COLS

kernel_name = "hgrn2_rwkv7_conformer_rglru_parallel_decoder_step"


def rmsnorm(x, g):
    xf = x.astype(jnp.float32)
    y = xf * lax.rsqrt(jnp.mean(xf * xf, axis=-1, keepdims=True) + EPS)
    return (y * g.astype(jnp.float32)).astype(x.dtype)


def layernorm(x, g, b):
    xf = x.astype(jnp.float32)
    mu = jnp.mean(xf, axis=-1, keepdims=True)
    var = jnp.mean(jnp.square(xf - mu), axis=-1, keepdims=True)
    y = (xf - mu) * lax.rsqrt(var + 1e-5) * g.astype(jnp.float32) + b.astype(jnp.float32)
    return y.astype(x.dtype)


def causal_dwconv(buf, u, w, b):
    k = w.shape[0]
    full = jnp.concatenate([buf.astype(u.dtype), u], axis=1)
    y = lax.conv_general_dilated(full, w[:, None, :].astype(u.dtype), window_strides=(1,),
                                 padding="VALID", dimension_numbers=("NWC", "WIO", "NWC"),
                                 feature_group_count=u.shape[-1])
    return y + b.astype(u.dtype), full[:, -(k - 1):]


def hgrn2_chunked(q, k, logf, v, s0):
    bsz, t, h, _ = q.shape
    dv = v.shape[-1]
    c = math.gcd(t, HG_CHUNK)
    n = t // c

    def to_chunks(a):
        return a.reshape(bsz, n, c, h, a.shape[-1]).transpose(1, 0, 3, 2, 4)

    causal = jnp.tril(jnp.ones((c, c), dtype=bool))[:, :, None]

    def step(s, inp):
        qi, ki, fi, vi = inp
        b = jnp.cumsum(fi, axis=2)
        o_inter = jnp.einsum("bhtd,bhdv->bhtv", qi * jnp.exp(b), s)
        diff = b[:, :, :, None, :] - b[:, :, None, :, :]
        decay = jnp.exp(jnp.where(causal, diff, -jnp.inf))
        scores = jnp.sum(qi[:, :, :, None, :] * ki[:, :, None, :, :] * decay, axis=-1)
        o = o_inter + jnp.einsum("bhts,bhsv->bhtv", scores, vi)
        b_last = b[:, :, -1:, :]
        s_new = jnp.exp(b_last[:, :, 0, :])[..., None] * s + \
            jnp.einsum("bhsd,bhsv->bhdv", ki * jnp.exp(b_last - b), vi)
        return s_new, o

    s_fin, o = lax.scan(step, s0, tuple(map(to_chunks, (q, k, logf, v))))
    return o.transpose(1, 0, 3, 2, 4).reshape(bsz, t, h, dv), s_fin


def rwkv7_scan(r, w, k, v, kk, a, s0):
    def step(s, inp):
        rt, wt, kt, vt, kkt, at = inp
        sa = jnp.einsum("bhvk,bhk->bhv", s, -kkt)
        s = s * wt[:, :, None, :] + sa[..., None] * (kkt * at)[:, :, None, :] + vt[..., None] * kt[:, :, None, :]
        return s, jnp.einsum("bhvk,bhk->bhv", s, rt)

    xs = tuple(jnp.moveaxis(z, 1, 0) for z in (r, w, k, v, kk, a))
    s_fin, y = lax.scan(step, s0, xs)
    return jnp.moveaxis(y, 0, 1), s_fin


def _lin_combine(e1, e2):
    a1, b1 = e1
    a2, b2 = e2
    return a1 * a2, a2 * b1 + b2


def token_mixers(h, p, l, lb_l, s_hg, s_rw, s_shift, s_cf, s_lh, s_lc):
    bsz, t, _ = h.shape
    dt = h.dtype
    f32 = jnp.float32
    W = BRANCH_W
    proj = h @ p["w_in"][l]
    hg, rw, cf, lr = jnp.split(proj, [HG_COLS, HG_COLS + RW_COLS, HG_COLS + RW_COLS + CF_COLS], axis=-1)

    q, fz, iv, og = jnp.split(hg.astype(f32), 4, axis=-1)
    lb_l = lb_l.astype(f32)
    logf = jnp.log(lb_l + (1.0 - lb_l) * jax.nn.sigmoid(fz))
    kf = (1.0 - lb_l) * jax.nn.sigmoid(-fz)
    o, s_hg_new = hgrn2_chunked(jax.nn.silu(q).reshape(bsz, t, HG_HEADS, HG_DK),
                                kf.reshape(bsz, t, HG_HEADS, HG_DK),
                                logf.reshape(bsz, t, HG_HEADS, HG_DK),
                                iv.reshape(bsz, t, HG_HEADS, HG_DV), s_hg.astype(f32))
    o = o * lax.rsqrt(jnp.mean(o * o, axis=-1, keepdims=True) + EPS)
    y_hg = o.reshape(bsz, t, W) * p["hg_norm_g"][l] * jax.nn.silu(og)

    prev = jnp.concatenate([s_shift[:, None, :].astype(dt), rw[:, :-1]], axis=1)
    rwm = (rw + (prev - rw) * p["rw_mu"][l]).astype(f32)
    r, k, v, wd, ad, gd = jnp.split(rwm, [W, 2 * W, 3 * W, 3 * W + RW_W_RANK, 3 * W + RW_W_RANK + RW_A_RANK], axis=-1)
    log_w = -RW_DECAY_SCALE * jax.nn.sigmoid(p["rw_w0"][l] + jnp.tanh(wd) @ p["rw_w_up"][l])
    a = jax.nn.sigmoid(p["rw_a0"][l] + ad @ p["rw_a_up"][l])
    gate = jax.nn.sigmoid(gd) @ p["rw_g_up"][l]
    hr = lambda z: z.reshape(bsz, t, RW_HEADS, RW_HEAD)
    kk = hr(k * p["rw_k_k"][l])
    kk = kk / jnp.maximum(jnp.sqrt(jnp.sum(kk * kk, axis=-1, keepdims=True)), 1e-12)
    k = k * (1.0 + (a - 1.0) * p["rw_k_a"][l])
    y, s_rw_new = rwkv7_scan(hr(r), hr(jnp.exp(log_w)), hr(k), hr(v), kk, hr(a), s_rw.astype(f32))
    mu = jnp.mean(y, axis=-1, keepdims=True)
    var = jnp.mean(jnp.square(y - mu), axis=-1, keepdims=True)
    y = ((y - mu) * lax.rsqrt(var + RW_GN_EPS)).reshape(bsz, t, W) * p["rw_ln_g"][l] + p["rw_ln_b"][l]
    bonus = jnp.sum(hr(r * k * p["rw_r_k"][l]), axis=-1, keepdims=True) * hr(v)
    y_rw = (y + bonus.reshape(bsz, t, W)) * gate

    u = cf[..., :W] * jax.nn.sigmoid(cf[..., W:])
    yc, s_cf_new = causal_dwconv(s_cf, u, p["cf_dw"][l], p["cf_dw_b"][l])
    y_cf = jax.nn.silu(layernorm(yc, p["cf_ln_g"][l], p["cf_ln_b"][l]))

    xl, gl = lr[..., :W], lr[..., W:]
    xc, s_lc_new = causal_dwconv(s_lc, xl, p["lru_conv_w"][l], p["lru_conv_b"][l])
    xcf = xc.astype(f32)
    xb = xcf.reshape(bsz, t, LRU_BLOCKS, LRU_BW)
    rg = jax.nn.sigmoid(jnp.einsum("btnc,ncd->btnd", xb, p["lru_wa"][l]).reshape(bsz, t, W) + p["lru_ba"][l])
    ig = jax.nn.sigmoid(jnp.einsum("btnc,ncd->btnd", xb, p["lru_wx"][l]).reshape(bsz, t, W) + p["lru_bx"][l])
    log_a = -LRU_C * rg * jax.nn.softplus(-p["lru_lambda"][l].astype(f32))
    a_t = jnp.exp(log_a)
    b_t = jnp.sqrt(-jnp.expm1(2.0 * log_a)) * (ig * xcf)
    b_t = b_t.at[:, 0].add(a_t[:, 0] * s_lh.astype(f32))
    _, hs = lax.associative_scan(_lin_combine, (a_t, b_t), axis=1)
    y_lru = hs * jax.nn.gelu(gl.astype(f32))

    branches = jnp.stack([y_hg, y_rw, y_cf, y_lru], axis=2).astype(dt)
    bo = jnp.einsum("btkc,kcd->btkd", branches, p["w_branch"][l])
    gates = jax.nn.sigmoid((h @ p["w_gate"][l] + p["b_gate"][l]).reshape(bsz, t, N_BRANCH, D_MODEL))
    out = jnp.sum(gates * bo, axis=2) @ p["w_out"][l]
    new_states = (s_hg_new.astype(s_hg.dtype), s_rw_new.astype(s_rw.dtype), rw[:, -1].astype(s_shift.dtype),
                  s_cf_new.astype(s_cf.dtype), hs[:, -1].astype(s_lh.dtype), s_lc_new.astype(s_lc.dtype))
    return out, new_states


def trunk(x, c, states, p):
    sm = jax.nn.softmax(p["hg_lower"].astype(jnp.float32), axis=0)
    lb = jnp.cumsum(sm, axis=0) - sm[0]
    mod_c = jax.nn.silu(c)
    collected = [[] for _ in range(6)]
    for l in range(DEPTH):
        mod = mod_c @ p["ada_w"][l] + p["ada_b"][l]
        sh1, sc1, g1, sh2, sc2, g2 = [m[:, None, :] for m in jnp.split(mod, 6, axis=-1)]
        h = rmsnorm(x, p["norm_mix_g"][l]) * (1.0 + sc1) + sh1
        out, st_new = token_mixers(h, p, l, lb[l], *[s[l] for s in states])
        x = x + g1 * out
        h2 = rmsnorm(x, p["norm_mlp_g"][l]) * (1.0 + sc2) + sh2
        x = x + g2 * (jnp.square(jax.nn.relu(h2 @ p["w_mlp1"][l])) @ p["w_mlp2"][l])
        for lst, s in zip(collected, st_new):
            lst.append(s)
    y = rmsnorm(x, p["norm_final_g"])
    return y, [jnp.stack(lst, axis=0) for lst in collected]


def setup_inputs(seed: int = 0) -> dict:
    key = jax.random.key(seed)
    keys = jax.random.split(key, 64)
    ctr = [0]

    def nrm(shape, s):
        kk = keys[ctr[0]]
        ctr[0] += 1
        return s * jax.random.normal(kk, shape, jnp.float32)

    def unif(shape, lo, hi):
        kk = keys[ctr[0]]
        ctr[0] += 1
        return jax.random.uniform(kk, shape, jnp.float32, lo, hi)

    D, W, L = D_MODEL, BRANCH_W, DEPTH
    inp = {}
    inp["x_prompt"] = nrm((BATCH, SEQ, D), 1.0)
    inp["x_sample"] = nrm((DEC_BATCH, DEC_SEQ, D), 1.0)
    inp["state_hgrn"] = nrm((L, DEC_BATCH, HG_HEADS, HG_DK, HG_DV), 0.5)
    inp["state_rwkv"] = nrm((L, DEC_BATCH, RW_HEADS, RW_HEAD, RW_HEAD), 0.3)
    inp["state_rwkv_shift"] = nrm((L, DEC_BATCH, RW_COLS), 1.0)
    inp["state_conv"] = nrm((L, DEC_BATCH, CF_WIDTH - 1, W), 0.5)
    inp["state_lru_h"] = nrm((L, DEC_BATCH, W), 0.5)
    inp["state_lru_conv"] = nrm((L, DEC_BATCH, LRU_CONV - 1, W), 1.0)
    inp["c_prompt"] = nrm((BATCH, D), 1.0)
    inp["c_sample"] = nrm((DEC_BATCH, D), 1.0)
    inp["ada_w"] = nrm((L, D, 6 * D), 0.5 * D ** -0.5)
    inp["ada_b"] = nrm((L, 6 * D), 0.02)
    inp["norm_mix_g"] = 1.0 + nrm((L, D), 0.05)
    inp["norm_mlp_g"] = 1.0 + nrm((L, D), 0.05)
    inp["norm_final_g"] = 1.0 + nrm((D,), 0.05)
    inp["w_in"] = nrm((L, D, IN_COLS), D ** -0.5)
    inp["hg_lower"] = nrm((L, W), 0.1)
    inp["hg_norm_g"] = 1.0 + nrm((L, W), 0.05)
    inp["rw_mu"] = unif((L, RW_COLS), 0.0, 1.0)
    inp["rw_w0"] = nrm((L, W), 0.5)
    inp["rw_w_up"] = nrm((L, RW_W_RANK, W), 0.5 * RW_W_RANK ** -0.5)
    inp["rw_a0"] = nrm((L, W), 0.1)
    inp["rw_a_up"] = nrm((L, RW_A_RANK, W), RW_A_RANK ** -0.5)
    inp["rw_g_up"] = nrm((L, RW_G_RANK, W), RW_G_RANK ** -0.5)
    inp["rw_k_k"] = 0.85 + nrm((L, W), 0.05)
    inp["rw_k_a"] = 1.0 + nrm((L, W), 0.05)
    inp["rw_r_k"] = nrm((L, W), 0.1)
    inp["rw_ln_g"] = 1.0 + nrm((L, W), 0.05)
    inp["rw_ln_b"] = nrm((L, W), 0.02)
    inp["cf_dw"] = nrm((L, CF_WIDTH, W), CF_WIDTH ** -0.5)
    inp["cf_dw_b"] = nrm((L, W), 0.02)
    inp["cf_ln_g"] = 1.0 + nrm((L, W), 0.05)
    inp["cf_ln_b"] = nrm((L, W), 0.02)
    inp["lru_conv_w"] = nrm((L, LRU_CONV, W), LRU_CONV ** -0.5)
    inp["lru_conv_b"] = nrm((L, W), 0.02)
    inp["lru_wa"] = nrm((L, LRU_BLOCKS, LRU_BW, LRU_BW), LRU_BW ** -0.5)
    inp["lru_ba"] = nrm((L, W), 0.02)
    inp["lru_wx"] = nrm((L, LRU_BLOCKS, LRU_BW, LRU_BW), LRU_BW ** -0.5)
    inp["lru_bx"] = nrm((L, W), 0.02)
    a0 = unif((L, W), 0.9, 0.999) ** (1.0 / LRU_C)
    inp["lru_lambda"] = jnp.log(a0) - jnp.log1p(-a0)
    inp["w_branch"] = nrm((L, N_BRANCH, W, D), W ** -0.5)
    inp["w_gate"] = nrm((L, D, N_BRANCH * D), D ** -0.5)
    inp["b_gate"] = nrm((L, N_BRANCH * D), 0.02)
    inp["w_out"] = nrm((L, D, D), D ** -0.5)
    inp["w_mlp1"] = nrm((L, D, MLP_HIDDEN), D ** -0.5)
    inp["w_mlp2"] = nrm((L, MLP_HIDDEN, D), MLP_HIDDEN ** -0.5)
    return inp


def reference(x_prompt, x_sample, state_hgrn, state_rwkv, state_rwkv_shift, state_conv, state_lru_h,
              state_lru_conv, c_prompt, c_sample, ada_w, ada_b, norm_mix_g, norm_mlp_g, norm_final_g,
              w_in, hg_lower, hg_norm_g, rw_mu, rw_w0, rw_w_up, rw_a0, rw_a_up, rw_g_up, rw_k_k, rw_k_a,
              rw_r_k, rw_ln_g, rw_ln_b, cf_dw, cf_dw_b, cf_ln_g, cf_ln_b, lru_conv_w, lru_conv_b, lru_wa,
              lru_ba, lru_wx, lru_bx, lru_lambda, w_branch, w_gate, b_gate, w_out, w_mlp1, w_mlp2):
    p = dict(ada_w=ada_w, ada_b=ada_b, norm_mix_g=norm_mix_g, norm_mlp_g=norm_mlp_g,
             norm_final_g=norm_final_g, w_in=w_in, hg_lower=hg_lower, hg_norm_g=hg_norm_g, rw_mu=rw_mu,
             rw_w0=rw_w0, rw_w_up=rw_w_up, rw_a0=rw_a0, rw_a_up=rw_a_up, rw_g_up=rw_g_up, rw_k_k=rw_k_k,
             rw_k_a=rw_k_a, rw_r_k=rw_r_k, rw_ln_g=rw_ln_g, rw_ln_b=rw_ln_b, cf_dw=cf_dw, cf_dw_b=cf_dw_b,
             cf_ln_g=cf_ln_g, cf_ln_b=cf_ln_b, lru_conv_w=lru_conv_w, lru_conv_b=lru_conv_b, lru_wa=lru_wa,
             lru_ba=lru_ba, lru_wx=lru_wx, lru_bx=lru_bx, lru_lambda=lru_lambda, w_branch=w_branch,
             w_gate=w_gate, b_gate=b_gate, w_out=w_out, w_mlp1=w_mlp1, w_mlp2=w_mlp2)
    dt = x_prompt.dtype
    bp = x_prompt.shape[0]
    zero_states = (
        jnp.zeros((DEPTH, bp, HG_HEADS, HG_DK, HG_DV), dt),
        jnp.zeros((DEPTH, bp, RW_HEADS, RW_HEAD, RW_HEAD), dt),
        jnp.zeros((DEPTH, bp, RW_COLS), dt),
        jnp.zeros((DEPTH, bp, CF_WIDTH - 1, BRANCH_W), dt),
        jnp.zeros((DEPTH, bp, BRANCH_W), dt),
        jnp.zeros((DEPTH, bp, LRU_CONV - 1, BRANCH_W), dt),
    )
    y_prompt, st_p = trunk(x_prompt, c_prompt, zero_states, p)
    sample_states = (state_hgrn, state_rwkv, state_rwkv_shift, state_conv, state_lru_h, state_lru_conv)
    y_sample, st_s = trunk(x_sample, c_sample, sample_states, p)
    hgrn_p, rwkv_p, shift_p, conv_p, lru_h_p, lru_conv_p = st_p
    hgrn_s, rwkv_s, shift_s, conv_s, lru_h_s, lru_conv_s = st_s
    return (y_prompt, y_sample, hgrn_p, rwkv_p, shift_p, conv_p, lru_h_p, lru_conv_p,
            hgrn_s, rwkv_s, shift_s, conv_s, lru_h_s, lru_conv_s)
```

```python
import functools
import math

import jax
import jax.numpy as jnp
from jax import lax
from jax.experimental import pallas as pl
from jax.experimental.pallas import tpu as pltpu

D = 1024
W = 512
HG_H = 4
HG_D = 128
RW_H = 8
RW_D = 64
RW_COLS = 1792
RW_PAD = 2048
CF_K = 31
LRU_K = 4
HID = 4096
EPS = 1e-6
RW_GN_EPS = 64e-5
RW_DECAY = 0.606531
LRU_C = 8.0

P_COLS = 10240
VMEM_LIMIT = 56 * 1024 * 1024

F32 = jnp.float32
BF16 = jnp.bfloat16


def _bf(x):
    return x.astype(BF16)


def _mm(a, b):
    return jnp.dot(_bf(a), _bf(b), preferred_element_type=F32)


def _mm_nt(a, b):
    return lax.dot_general(_bf(a), _bf(b), (((1,), (1,)), ((), ())), preferred_element_type=F32)


def _mm_tn(a, b):
    return lax.dot_general(_bf(a), _bf(b), (((0,), (0,)), ((), ())), preferred_element_type=F32)


def _sigmoid(x):
    return 1.0 / (1.0 + jnp.exp(-x))


def _silu(x):
    return x * _sigmoid(x)


def _shift_rows(x, d, fill):
    row = lax.broadcasted_iota(jnp.int32, x.shape, 0)
    return jnp.where(row >= d, pltpu.roll(x, d, axis=0), fill)


def _cumsum_rows(x):
    n = x.shape[0]
    d = 1
    while d < n:
        x = x + _shift_rows(x, d, 0.0)
        d *= 2
    return x


def _headsum(x, ones_bd):
    hi = _bf(x)
    lo = _bf(x - hi.astype(F32))
    return (jnp.dot(hi, ones_bd, preferred_element_type=F32)
            + jnp.dot(lo, ones_bd, preferred_element_type=F32))


def _cparams(sem):
    return pltpu.CompilerParams(dimension_semantics=sem, vmem_limit_bytes=VMEM_LIMIT)


def _mod_kernel(c_ref, w_ref, b_ref, o_ref):
    c = c_ref[...]
    o_ref[0, 0] = _mm(_silu(c), w_ref[0]) + b_ref[0, 0]


def _mod_call(c, ada_w, ada_b):
    nl = ada_w.shape[0]
    bsz = c.shape[0]
    return pl.pallas_call(
        _mod_kernel,
        out_shape=jax.ShapeDtypeStruct((nl, 6, bsz, D), F32),
        grid=(nl, 6),
        in_specs=[
            pl.BlockSpec((bsz, D), lambda l, j: (0, 0)),
            pl.BlockSpec((1, D, D), lambda l, j: (l, 0, j)),
            pl.BlockSpec((1, 1, 1, D), lambda l, j: (l, j, 0, 0)),
        ],
        out_specs=pl.BlockSpec((1, 1, bsz, D), lambda l, j: (l, j, 0, 0)),
        compiler_params=_cparams(("arbitrary", "arbitrary")),
        name="adaln_mod",
    )(c, ada_w, ada_b.reshape(nl, 6, 1, D))


def _adaln(x, g, sc, sh):
    ms = jnp.mean(x * x, axis=-1, keepdims=True)
    y = x * lax.rsqrt(ms + EPS) * g
    return y * (1.0 + sc) + sh


def _inproj_kernel(x_ref, sc_ref, sh_ref, g_ref, w_ref, b_ref, o_ref, h_scr):
    @pl.when(pl.program_id(2) == 0)
    def _():
        h = _adaln(x_ref[...], g_ref[...], sc_ref[...], sh_ref[...])
        h_scr[...] = _bf(h.reshape(h_scr.shape))

    acc = jnp.dot(h_scr[...], w_ref[...], preferred_element_type=F32) + b_ref[...]
    o_ref[...] = acc.reshape(o_ref.shape)


def _tile(bsz, t, rows):
    if t >= rows:
        return 1, rows
    return min(bsz, rows // t), t


def _inproj_call(x, sc, sh, g, wcat, bcat, rows=1024, tn=1024):
    bsz, t, _ = x.shape
    sb, tt = _tile(bsz, t, rows)
    grid = (bsz // sb, t // tt, P_COLS // tn)
    return pl.pallas_call(
        _inproj_kernel,
        out_shape=jax.ShapeDtypeStruct((bsz, t, P_COLS), F32),
        grid=grid,
        in_specs=[
            pl.BlockSpec((sb, tt, D), lambda i, j, n: (i, j, 0)),
            pl.BlockSpec((sb, 1, D), lambda i, j, n: (i, 0, 0)),
            pl.BlockSpec((sb, 1, D), lambda i, j, n: (i, 0, 0)),
            pl.BlockSpec((1, D), lambda i, j, n: (0, 0)),
            pl.BlockSpec((D, tn), lambda i, j, n: (0, n)),
            pl.BlockSpec((1, tn), lambda i, j, n: (0, n)),
        ],
        out_specs=pl.BlockSpec((sb, tt, tn), lambda i, j, n: (i, j, n)),
        scratch_shapes=[pltpu.VMEM((sb * tt, D), BF16)],
        compiler_params=_cparams(("arbitrary", "arbitrary", "arbitrary")),
        name="inproj",
    )(x, sc, sh, g, wcat, bcat)


HG_SUB = 16


def _hgrn_chunk(qr, fz, iv, og, lb, gn, s_refs_get, s_refs_set):
    c = qr.shape[0]
    sub = min(HG_SUB, c)
    q = _silu(qr)
    f = lb + (1.0 - lb) * _sigmoid(fz)
    logf = jnp.log(f)
    kf = (1.0 - lb) * _sigmoid(-fz)
    b = _cumsum_rows(logf)
    outs = []
    for h in range(HG_H):
        hs = slice(h * HG_D, (h + 1) * HG_D)
        bh, qh, kh, vh = b[:, hs], q[:, hs], kf[:, hs], iv[:, hs]
        st = s_refs_get(h)
        o_inter = _mm_nt(qh * jnp.exp(bh), st)
        b_last = bh[c - 1:c, :]
        pieces = []
        for i in range(c // sub):
            r0 = i * sub
            m = bh[r0 - 1:r0, :] if i > 0 else jnp.zeros((1, HG_D), F32)
            qs = qh[r0:r0 + sub] * jnp.exp(bh[r0:r0 + sub] - m)
            kd = kh[r0:r0 + sub] * jnp.exp(jnp.minimum(m - bh[r0:r0 + sub], 80.0))
            if i > 0:
                kp = kh[:r0] * jnp.exp(m - bh[:r0])
                kall = jnp.concatenate([kp, kd], axis=0)
            else:
                kall = kd
            sc = _mm_nt(qs, kall)
            row = lax.broadcasted_iota(jnp.int32, sc.shape, 0) + r0
            col = lax.broadcasted_iota(jnp.int32, sc.shape, 1)
            sc = jnp.where(col <= row, sc, 0.0)
            pieces.append(_mm(sc, vh[:r0 + sub]))
        o = o_inter + (jnp.concatenate(pieces, axis=0) if len(pieces) > 1 else pieces[0])
        s_refs_set(h, st * jnp.exp(b_last) + _mm_tn(vh, kh * jnp.exp(b_last - bh)))
        o = o * lax.rsqrt(jnp.mean(o * o, axis=-1, keepdims=True) + EPS)
        outs.append(o * gn[:, hs] * _silu(og[:, hs]))
    return jnp.concatenate(outs, axis=1)


def _hgrn_kernel(l, chunk, p_ref, s0_ref, lower_ref, gn_ref, y_ref, sout_ref, s_scr):
    sb, tt, _ = p_ref.shape
    nch = tt // chunk
    tstep = pl.program_id(1)

    @pl.when(tstep == 0)
    def _():
        def init(i, carry):
            for h in range(HG_H):
                s_scr[i, h] = s0_ref[i, h].T
            return carry
        lax.fori_loop(0, sb, init, 0)

    low = lower_ref[...]
    e = jnp.exp(low - jnp.max(low, axis=0, keepdims=True))
    sm = e / jnp.sum(e, axis=0, keepdims=True)
    lb = jnp.sum(sm[:l + 1], axis=0, keepdims=True) - sm[0:1]
    gn = gn_ref[...]

    def body(i, carry):
        s = i // nch
        r0 = pl.multiple_of((i % nch) * chunk, chunk)
        rows = pl.ds(r0, chunk)
        blk = p_ref[s, rows, :]
        y = _hgrn_chunk(blk[:, 0:W], blk[:, W:2 * W], blk[:, 2 * W:3 * W], blk[:, 3 * W:4 * W], lb, gn,
                        lambda h: s_scr[s, h],
                        lambda h, v: s_scr.__setitem__((s, h), v))
        y_ref[s, rows, :] = y
        return carry
    lax.fori_loop(0, sb * nch, body, 0)

    @pl.when(tstep == pl.num_programs(1) - 1)
    def _():
        def fin(i, carry):
            for h in range(HG_H):
                sout_ref[i, h] = s_scr[i, h].T
            return carry
        lax.fori_loop(0, sb, fin, 0)


def _hgrn_call(p, s0, hg_lower, gn, l, rows, chunk):
    bsz, t, _ = p.shape
    sb, tt = _tile(bsz, t, rows)
    return pl.pallas_call(
        functools.partial(_hgrn_kernel, l, chunk),
        out_shape=(jax.ShapeDtypeStruct((bsz, t, W), F32),
                   jax.ShapeDtypeStruct((bsz, HG_H, HG_D, HG_D), F32)),
        grid=(bsz // sb, t // tt),
        in_specs=[
            pl.BlockSpec((sb, tt, 4 * W), lambda i, j: (i, j, 2)),
            pl.BlockSpec((sb, HG_H, HG_D, HG_D), lambda i, j: (i, 0, 0, 0)),
            pl.BlockSpec(hg_lower.shape, lambda i, j: (0, 0)),
            pl.BlockSpec((1, W), lambda i, j: (0, 0)),
        ],
        out_specs=(pl.BlockSpec((sb, tt, W), lambda i, j: (i, j, 0)),
                   pl.BlockSpec((sb, HG_H, HG_D, HG_D), lambda i, j: (i, 0, 0, 0))),
        scratch_shapes=[pltpu.VMEM((sb, HG_H, HG_D, HG_D), F32)],
        compiler_params=_cparams(("arbitrary", "arbitrary")),
        name="hgrn2",
    )(p, s0, hg_lower, gn)


def _rwkv_chunk(rw, prev_row, prm, wup, ones_bd, s_get, s_set):
    c = rw.shape[0]
    mu, w0, a0, k_k, k_a, r_k, ln_g, ln_b = prm
    row = lax.broadcasted_iota(jnp.int32, rw.shape, 0)
    prev = jnp.where(row == 0, prev_row, pltpu.roll(rw, 1, axis=0))
    rwm = rw + (prev - rw) * mu
    r, k, v = rwm[:, 0:W], rwm[:, W:2 * W], rwm[:, 2 * W:3 * W]
    lr = rwm[:, 3 * W:3 * W + 256]
    lane = lax.broadcasted_iota(jnp.int32, lr.shape, 1)
    act = jnp.where(lane < 64, jnp.tanh(lr), jnp.where(lane < 128, lr, _sigmoid(lr)))
    up = jnp.dot(_bf(act), wup, preferred_element_type=F32)
    log_w = -RW_DECAY * _sigmoid(w0 + up[:, 0:W])
    a = _sigmoid(a0 + up[:, W:2 * W])
    gate = up[:, 2 * W:3 * W]
    kk = k * k_k
    kk = kk / jnp.maximum(jnp.sqrt(_headsum(kk * kk, ones_bd)), 1e-12)
    k2 = k * (1.0 + (a - 1.0) * k_a)
    kka = kk * a

    logp = _cumsum_rows(log_w)
    logp_last = logp[c - 1:c, :]
    a_t = -kk * jnp.exp(logp - log_w)
    r_t = r * jnp.exp(logp)
    einv = jnp.exp(-logp)
    b_i, k_i = kka * einv, k2 * einv
    elast = jnp.exp(logp_last - logp)
    b_d, k_d = kka * elast, k2 * elast
    p_c = jnp.exp(logp_last)

    rr = lax.broadcasted_iota(jnp.int32, (c, c), 0)
    cc = lax.broadcasted_iota(jnp.int32, (c, c), 1)
    strict = rr > cc
    incl = rr >= cc
    nlev = int(math.log2(c))
    ys = []
    for h in range(RW_H):
        hs = slice(h * RW_D, (h + 1) * RW_D)
        s0 = s_get(h)
        ar = jnp.concatenate([a_t[:, hs], r_t[:, hs]], axis=0)
        bk = jnp.concatenate([b_i[:, hs], k_i[:, hs]], axis=0)
        g = _mm_nt(ar, bk)
        a_s = _mm_nt(ar, s0)
        vh = v[:, hs]
        lab = jnp.where(strict, g[:c, :c], 0.0)
        lak = jnp.where(strict, g[:c, c:], 0.0)
        mrb = jnp.where(incl, g[c:, :c], 0.0)
        mrk = jnp.where(incl, g[c:, c:], 0.0)
        x = a_s[:c] + _mm(lak, vh)
        lp = lab
        for j in range(nlev):
            x = x + _mm(lp, x)
            if j < nlev - 1:
                lp = _mm(lp, lp)
        uv = jnp.concatenate([x, vh], axis=0)
        y = a_s[c:] + _mm(jnp.concatenate([mrb, mrk], axis=1), uv)
        bkd = jnp.concatenate([b_d[:, hs], k_d[:, hs]], axis=0)
        s_set(h, s0 * p_c[:, hs] + _mm_tn(uv, bkd))
        ys.append(y)
    y = jnp.concatenate(ys, axis=1)
    mean = _headsum(y, ones_bd) * (1.0 / RW_D)
    yc = y - mean
    var = _headsum(yc * yc, ones_bd) * (1.0 / RW_D)
    yn = yc * lax.rsqrt(var + RW_GN_EPS) * ln_g + ln_b
    bonus = _headsum(r * k2 * r_k, ones_bd) * v
    return (yn + bonus) * gate


def _rwkv_kernel(chunk, p_ref, s0_ref, sh0_ref, mu_ref, vec_ref, wup_ref, ones_ref,
                 y_ref, sout_ref, shout_ref, s_scr, prev_scr):
    sb, tt, _ = p_ref.shape
    nch = tt // chunk
    tstep = pl.program_id(1)

    @pl.when(tstep == 0)
    def _():
        s_scr[...] = s0_ref[...]
        prev_scr[...] = sh0_ref[...]

    mu = mu_ref[...]
    vec = vec_ref[...]
    prm = (mu,) + tuple(vec[i:i + 1, :] for i in range(7))
    wup = wup_ref[...]
    ones_bd = ones_ref[...]

    def body(i, carry):
        s = i // nch
        r0 = pl.multiple_of((i % nch) * chunk, chunk)
        rows = pl.ds(r0, chunk)
        rw = p_ref[s, rows, :]
        y = _rwkv_chunk(rw, prev_scr[s], prm, wup, ones_bd,
                        lambda h: s_scr[s, h],
                        lambda h, val: s_scr.__setitem__((s, h), val))
        prev_scr[s] = rw[chunk - 1:chunk, :]
        y_ref[s, rows, :] = y
        return carry
    lax.fori_loop(0, sb * nch, body, 0)

    @pl.when(tstep == pl.num_programs(1) - 1)
    def _():
        sout_ref[...] = s_scr[...]
        shout_ref[...] = prev_scr[...]


def _rwkv_call(p, s0, sh0, mu, vec, wup, ones_bd, rows, chunk):
    bsz, t, _ = p.shape
    sb, tt = _tile(bsz, t, rows)
    y, s_new, sh_new = pl.pallas_call(
        functools.partial(_rwkv_kernel, chunk),
        out_shape=(jax.ShapeDtypeStruct((bsz, t, W), F32),
                   jax.ShapeDtypeStruct((bsz, RW_H, RW_D, RW_D), F32),
                   jax.ShapeDtypeStruct((bsz, 1, RW_PAD), F32)),
        grid=(bsz // sb, t // tt),
        in_specs=[
            pl.BlockSpec((sb, tt, RW_PAD), lambda i, j: (i, j, 3)),
            pl.BlockSpec((sb, RW_H, RW_D, RW_D), lambda i, j: (i, 0, 0, 0)),
            pl.BlockSpec((sb, 1, RW_PAD), lambda i, j: (i, 0, 0)),
            pl.BlockSpec((1, RW_PAD), lambda i, j: (0, 0)),
            pl.BlockSpec((8, W), lambda i, j: (0, 0)),
            pl.BlockSpec((256, 3 * W), lambda i, j: (0, 0)),
            pl.BlockSpec((W, W), lambda i, j: (0, 0)),
        ],
        out_specs=(pl.BlockSpec((sb, tt, W), lambda i, j: (i, j, 0)),
                   pl.BlockSpec((sb, RW_H, RW_D, RW_D), lambda i, j: (i, 0, 0, 0)),
                   pl.BlockSpec((sb, 1, RW_PAD), lambda i, j: (i, 0, 0))),
        scratch_shapes=[pltpu.VMEM((sb, RW_H, RW_D, RW_D), F32),
                        pltpu.VMEM((sb, 1, RW_PAD), F32)],
        compiler_params=_cparams(("arbitrary", "arbitrary")),
        name="rwkv7",
    )(p, s0, sh0, mu, vec, wup, ones_bd)
    return y, s_new, sh_new


CF_PAD = 32
CF_RB = 64


def _conf_kernel(p_ref, s0_ref, dw_ref, vec_ref, y_ref, sout_ref, ext):
    sb, tt, _ = p_ref.shape
    tstep = pl.program_id(1)
    off = CF_PAD - (CF_K - 1)

    @pl.when(tstep == 0)
    def _():
        ext[:, off:CF_PAD, :] = s0_ref[...]

    vec = vec_ref[...]
    bias, ln_g, ln_b = vec[0:1], vec[1:2], vec[2:3]
    dw = dw_ref[...]
    rb = min(CF_RB, tt)

    def body(s, carry):
        blk = p_ref[s]
        ext[s, CF_PAD:CF_PAD + tt, :] = blk[:, 0:W] * _sigmoid(blk[:, W:2 * W])
        for r in range(tt // rb):
            acc = jnp.zeros((rb, W), F32) + bias
            for j in range(CF_K):
                acc = acc + dw[j:j + 1, :] * ext[s, pl.ds(r * rb + off + j, rb), :]
            mean = jnp.mean(acc, axis=-1, keepdims=True)
            xc = acc - mean
            var = jnp.mean(xc * xc, axis=-1, keepdims=True)
            yn = xc * lax.rsqrt(var + 1e-5) * ln_g + ln_b
            y_ref[s, r * rb:(r + 1) * rb, :] = _silu(yn)
        tail = ext[s, tt:tt + CF_PAD, :]
        ext[s, 0:CF_PAD, :] = tail
        return carry
    lax.fori_loop(0, sb, body, 0)

    @pl.when(tstep == pl.num_programs(1) - 1)
    def _():
        sout_ref[...] = ext[:, off:CF_PAD, :]


def _conf_call(p, s0, dw, vec, rows):
    bsz, t, _ = p.shape
    sb, tt = _tile(bsz, t, rows)
    return pl.pallas_call(
        _conf_kernel,
        out_shape=(jax.ShapeDtypeStruct((bsz, t, W), F32),
                   jax.ShapeDtypeStruct((bsz, CF_K - 1, W), F32)),
        grid=(bsz // sb, t // tt),
        in_specs=[
            pl.BlockSpec((sb, tt, 2 * W), lambda i, j: (i, j, 8)),
            pl.BlockSpec((sb, CF_K - 1, W), lambda i, j: (i, 0, 0)),
            pl.BlockSpec((32, W), lambda i, j: (0, 0)),
            pl.BlockSpec((8, W), lambda i, j: (0, 0)),
        ],
        out_specs=(pl.BlockSpec((sb, tt, W), lambda i, j: (i, j, 0)),
                   pl.BlockSpec((sb, CF_K - 1, W), lambda i, j: (i, 0, 0))),
        scratch_shapes=[pltpu.VMEM((sb, CF_PAD + tt, W), F32)],
        compiler_params=_cparams(("arbitrary", "arbitrary")),
        name="conformer",
    )(p, s0, dw, vec)


LRU_PAD = 8


def _gelu_tanh(x):
    return 0.5 * x * (1.0 + jnp.tanh(0.7978845608028654 * (x + 0.044715 * (x * x * x))))


def _lru_kernel(p_ref, h0_ref, c0_ref, cw_ref, vec_ref, wax_ref, y_ref, hout_ref, cout_ref, ext, hcar):
    sb, tt, _ = p_ref.shape
    tstep = pl.program_id(1)
    off = LRU_PAD - (LRU_K - 1)

    @pl.when(tstep == 0)
    def _():
        ext[:, off:LRU_PAD, :] = c0_ref[...]
        hcar[...] = h0_ref[...]

    vec = vec_ref[...]
    cb, ba, bx, lam = vec[0:1], vec[1:2], vec[2:3], vec[3:4]
    sp = jnp.maximum(-lam, 0.0) + jnp.log1p(jnp.exp(-jnp.abs(lam)))
    cw = cw_ref[...]
    wax = wax_ref[...]

    def body(s, carry):
        blk = p_ref[s]
        ext[s, LRU_PAD:LRU_PAD + tt, :] = blk[:, 0:W]
        xc = jnp.zeros((tt, W), F32) + cb
        for j in range(LRU_K):
            xc = xc + cw[j:j + 1, :] * ext[s, pl.ds(off + j, tt), :]
        pre = jnp.dot(_bf(xc), wax, preferred_element_type=F32)
        rg = _sigmoid(pre[:, 0:W] + ba)
        ig = _sigmoid(pre[:, W:2 * W] + bx)
        log_a = -LRU_C * rg * sp
        a = jnp.exp(log_a)
        bv = jnp.sqrt(1.0 - a * a) * (ig * xc)
        d = 1
        while d < tt:
            a_s = _shift_rows(a, d, 1.0)
            b_s = _shift_rows(bv, d, 0.0)
            bv = a * b_s + bv
            a = a * a_s
            d *= 2
        hs = a * hcar[s] + bv
        hcar[s] = hs[tt - 1:tt, :]
        y_ref[s] = hs * _gelu_tanh(blk[:, W:2 * W])
        tail = ext[s, tt:tt + LRU_PAD, :]
        ext[s, 0:LRU_PAD, :] = tail
        return carry
    lax.fori_loop(0, sb, body, 0)

    @pl.when(tstep == pl.num_programs(1) - 1)
    def _():
        hout_ref[...] = hcar[...]
        cout_ref[...] = ext[:, off:LRU_PAD, :]


def _lru_call(p, h0, c0, cw, vec, wax, rows):
    bsz, t, _ = p.shape
    sb, tt = _tile(bsz, t, rows)
    return pl.pallas_call(
        _lru_kernel,
        out_shape=(jax.ShapeDtypeStruct((bsz, t, W), F32),
                   jax.ShapeDtypeStruct((bsz, 1, W), F32),
                   jax.ShapeDtypeStruct((bsz, LRU_K - 1, W), F32)),
        grid=(bsz // sb, t // tt),
        in_specs=[
            pl.BlockSpec((sb, tt, 2 * W), lambda i, j: (i, j, 9)),
            pl.BlockSpec((sb, 1, W), lambda i, j: (i, 0, 0)),
            pl.BlockSpec((sb, LRU_K - 1, W), lambda i, j: (i, 0, 0)),
            pl.BlockSpec((8, W), lambda i, j: (0, 0)),
            pl.BlockSpec((8, W), lambda i, j: (0, 0)),
            pl.BlockSpec((W, 2 * W), lambda i, j: (0, 0)),
        ],
        out_specs=(pl.BlockSpec((sb, tt, W), lambda i, j: (i, j, 0)),
                   pl.BlockSpec((sb, 1, W), lambda i, j: (i, 0, 0)),
                   pl.BlockSpec((sb, LRU_K - 1, W), lambda i, j: (i, 0, 0))),
        scratch_shapes=[pltpu.VMEM((sb, LRU_PAD + tt, W), F32),
                        pltpu.VMEM((sb, 1, W), F32)],
        compiler_params=_cparams(("arbitrary", "arbitrary")),
        name="rglru",
    )(p, h0, c0, cw, vec, wax)


def _merge_kernel(gl_ref, yhg_ref, yrw_ref, ycf_ref, ylr_ref, x_ref, g1_ref, sc2_ref, sh2_ref, ng_ref,
                  wb_ref, wo_ref, x1_ref, h2_ref):
    sb, tt, _ = x_ref.shape
    tm = sb * tt
    z = jnp.zeros((tm, D), F32)
    for b, y_ref in enumerate((yhg_ref, yrw_ref, ycf_ref, ylr_ref)):
        bo = jnp.dot(_bf(y_ref[...].reshape(tm, W)), wb_ref[b], preferred_element_type=F32)
        gate = _sigmoid(gl_ref[:, :, b * D:(b + 1) * D].reshape(tm, D))
        z = z + gate * bo
    out = jnp.dot(_bf(z), wo_ref[...], preferred_element_type=F32).reshape(sb, tt, D)
    x1 = x_ref[...] + g1_ref[...] * out
    x1_ref[...] = x1
    h2_ref[...] = _bf(_adaln(x1, ng_ref[...], sc2_ref[...], sh2_ref[...]))


def _merge_call(p, ys, x, g1, sc2, sh2, ng, wb, wo, rows=512):
    bsz, t, _ = x.shape
    sb, tt = _tile(bsz, t, rows)
    tok = lambda width: pl.BlockSpec((sb, tt, width), lambda i, j: (i, j, 0))
    seq = pl.BlockSpec((sb, 1, D), lambda i, j: (i, 0, 0))
    return pl.pallas_call(
        _merge_kernel,
        out_shape=(jax.ShapeDtypeStruct((bsz, t, D), F32),
                   jax.ShapeDtypeStruct((bsz, t, D), BF16)),
        grid=(bsz // sb, t // tt),
        in_specs=[tok(4 * D), tok(W), tok(W), tok(W), tok(W), tok(D), seq, seq, seq,
                  pl.BlockSpec((1, D), lambda i, j: (0, 0)),
                  pl.BlockSpec((4, W, D), lambda i, j: (0, 0, 0)),
                  pl.BlockSpec((D, D), lambda i, j: (0, 0))],
        out_specs=(tok(D), tok(D)),
        compiler_params=_cparams(("arbitrary", "arbitrary")),
        name="merge",
    )(p, *ys, x, g1, sc2, sh2, ng, wb, wo)


def _mlp_kernel(final, h2_ref, x1_ref, g2_ref, fg_ref, w1_ref, w2_ref, o_ref, acc):
    sb, tt, _ = x1_ref.shape
    k = pl.program_id(2)

    @pl.when(k == 0)
    def _():
        acc[...] = jnp.zeros_like(acc)

    hid = jnp.dot(h2_ref[...].reshape(sb * tt, D), w1_ref[...], preferred_element_type=F32)
    act = jnp.square(jnp.maximum(hid, 0.0))
    acc[...] += jnp.dot(_bf(act), w2_ref[...], preferred_element_type=F32)

    @pl.when(k == pl.num_programs(2) - 1)
    def _():
        x2 = x1_ref[...] + g2_ref[...] * acc[...].reshape(sb, tt, D)
        if final:
            ms = jnp.mean(x2 * x2, axis=-1, keepdims=True)
            x2 = x2 * lax.rsqrt(ms + EPS) * fg_ref[...]
        o_ref[...] = x2


def _mlp_call(h2, x1, g2, fg, w1, w2, final, rows=1024, th=1024):
    bsz, t, _ = x1.shape
    sb, tt = _tile(bsz, t, rows)
    tok = pl.BlockSpec((sb, tt, D), lambda i, j, k: (i, j, 0))
    return pl.pallas_call(
        functools.partial(_mlp_kernel, final),
        out_shape=jax.ShapeDtypeStruct((bsz, t, D), F32),
        grid=(bsz // sb, t // tt, HID // th),
        in_specs=[tok, tok,
                  pl.BlockSpec((sb, 1, D), lambda i, j, k: (i, 0, 0)),
                  pl.BlockSpec((1, D), lambda i, j, k: (0, 0)),
                  pl.BlockSpec((D, th), lambda i, j, k: (0, k)),
                  pl.BlockSpec((th, D), lambda i, j, k: (k, 0))],
        out_specs=tok,
        scratch_shapes=[pltpu.VMEM((sb * tt, D), F32)],
        compiler_params=_cparams(("arbitrary", "arbitrary", "arbitrary")),
        name="mlp",
    )(h2, x1, g2, fg, w1, w2)


def _block_diag(w):
    n, c, d = w.shape
    eye = jnp.eye(n, dtype=w.dtype)
    return (eye[:, None, :, None] * w[:, :, None, :]).reshape(n * c, n * d)


def _prep_layer(wt, l):
    w_in = wt["w_in"][l]
    zpad = jnp.zeros((D, RW_PAD - RW_COLS), F32)
    wcat = jnp.concatenate([wt["w_gate"][l], w_in[:, 0:2048], w_in[:, 2048:2048 + RW_COLS], zpad,
                            w_in[:, 2048 + RW_COLS:]], axis=1)
    bcat = jnp.concatenate([wt["b_gate"][l], jnp.zeros((P_COLS - 4 * D,), F32)])[None, :]
    wup = jnp.zeros((256, 3 * W), F32)
    wup = wup.at[0:64, 0:W].set(wt["rw_w_up"][l])
    wup = wup.at[64:128, W:2 * W].set(wt["rw_a_up"][l])
    wup = wup.at[128:256, 2 * W:3 * W].set(wt["rw_g_up"][l])
    zrow = jnp.zeros((W,), F32)
    rw_vec = jnp.stack([wt["rw_w0"][l], wt["rw_a0"][l], wt["rw_k_k"][l], wt["rw_k_a"][l], wt["rw_r_k"][l],
                        wt["rw_ln_g"][l], wt["rw_ln_b"][l], zrow])
    mu = jnp.concatenate([wt["rw_mu"][l], jnp.zeros((RW_PAD - RW_COLS,), F32)])[None, :]
    cf_vec = jnp.stack([wt["cf_dw_b"][l], wt["cf_ln_g"][l], wt["cf_ln_b"][l]] + [zrow] * 5)
    cf_dw = jnp.concatenate([wt["cf_dw"][l], jnp.zeros((1, W), F32)], axis=0)
    lru_vec = jnp.stack([wt["lru_conv_b"][l], wt["lru_ba"][l], wt["lru_bx"][l], wt["lru_lambda"][l]] + [zrow] * 4)
    lru_cw = jnp.concatenate([wt["lru_conv_w"][l], jnp.zeros((8 - LRU_K, W), F32)], axis=0)
    wax = jnp.concatenate([_block_diag(wt["lru_wa"][l]), _block_diag(wt["lru_wx"][l])], axis=1)
    return dict(
        wcat=_bf(wcat), bcat=bcat, wup=_bf(wup), rw_vec=rw_vec, mu=mu, cf_vec=cf_vec, cf_dw=cf_dw,
        lru_vec=lru_vec, lru_cw=lru_cw, wax=_bf(wax),
        wb=_bf(wt["w_branch"][l]), wo=_bf(wt["w_out"][l]), w1=_bf(wt["w_mlp1"][l]), w2=_bf(wt["w_mlp2"][l]),
        hg_gn=wt["hg_norm_g"][l][None, :], nmix=wt["norm_mix_g"][l][None, :], nmlp=wt["norm_mlp_g"][l][None, :],
    )


def _trunk(x, c, states, wt, layers, ones_bd, mix_rows, chunk):
    s_hg, s_rw, s_shift, s_cf, s_lh, s_lc = states
    bsz = x.shape[0]
    nl = len(layers)
    mod = _mod_call(c, wt["ada_w"], wt["ada_b"])
    new = [[] for _ in range(6)]
    fg = wt["norm_final_g"][None, :]
    for l, lw in enumerate(layers):
        sh1, sc1, g1, sh2, sc2, g2 = [mod[l, i][:, None, :] for i in range(6)]
        p = _inproj_call(x, sc1, sh1, lw["nmix"], lw["wcat"], lw["bcat"])
        y_hg, n_hg = _hgrn_call(p, s_hg[l], wt["hg_lower"], lw["hg_gn"], l, mix_rows, chunk)
        sh0 = jnp.pad(s_shift[l], ((0, 0), (0, RW_PAD - RW_COLS)))[:, None, :]
        y_rw, n_rw, n_sh = _rwkv_call(p, s_rw[l], sh0, lw["mu"], lw["rw_vec"], lw["wup"], ones_bd,
                                      mix_rows, chunk)
        y_cf, n_cf = _conf_call(p, s_cf[l], lw["cf_dw"], lw["cf_vec"], mix_rows)
        y_lr, n_lh, n_lc = _lru_call(p, s_lh[l][:, None, :], s_lc[l], lw["lru_cw"], lw["lru_vec"], lw["wax"],
                                     mix_rows)
        x1, h2 = _merge_call(p, (y_hg, y_rw, y_cf, y_lr), x, g1, sc2, sh2, lw["nmlp"], lw["wb"], lw["wo"])
        x = _mlp_call(h2, x1, g2, fg, lw["w1"], lw["w2"], final=(l == nl - 1))
        for lst, s in zip(new, (n_hg, n_rw, n_sh[:, 0, :RW_COLS], n_cf, n_lh[:, 0, :], n_lc)):
            lst.append(s)
    return x, [jnp.stack(lst, axis=0) for lst in new]


def _run(x_prompt, x_sample, sample_states, c_prompt, c_sample, wt):
    nl = wt["w_in"].shape[0]
    layers = [_prep_layer(wt, l) for l in range(nl)]
    head = jnp.arange(W, dtype=jnp.int32) // RW_D
    ones_bd = _bf((head[:, None] == head[None, :]).astype(F32))
    bp = x_prompt.shape[0]
    zero_states = (
        jnp.zeros((nl, bp, HG_H, HG_D, HG_D), F32),
        jnp.zeros((nl, bp, RW_H, RW_D, RW_D), F32),
        jnp.zeros((nl, bp, RW_COLS), F32),
        jnp.zeros((nl, bp, CF_K - 1, W), F32),
        jnp.zeros((nl, bp, W), F32),
        jnp.zeros((nl, bp, LRU_K - 1, W), F32),
    )
    tp = x_prompt.shape[1]
    ts = x_sample.shape[1]
    y_p, st_p = _trunk(x_prompt, c_prompt, zero_states, wt, layers, ones_bd, min(256, tp), min(64, tp))
    y_s, st_s = _trunk(x_sample, c_sample, sample_states, wt, layers, ones_bd, 8 * ts, ts)
    return (y_p, y_s, *st_p, *st_s)


def kernel(x_prompt, x_sample, state_hgrn, state_rwkv, state_rwkv_shift, state_conv, state_lru_h, state_lru_conv, c_prompt, c_sample, ada_w, ada_b, norm_mix_g, norm_mlp_g, norm_final_g, w_in, hg_lower, hg_norm_g, rw_mu, rw_w0, rw_w_up, rw_a0, rw_a_up, rw_g_up, rw_k_k, rw_k_a, rw_r_k, rw_ln_g, rw_ln_b, cf_dw, cf_dw_b, cf_ln_g, cf_ln_b, lru_conv_w, lru_conv_b, lru_wa, lru_ba, lru_wx, lru_bx, lru_lambda, w_branch, w_gate, b_gate, w_out, w_mlp1, w_mlp2):
    wt = dict(ada_w=ada_w, ada_b=ada_b, norm_mix_g=norm_mix_g, norm_mlp_g=norm_mlp_g,
              norm_final_g=norm_final_g, w_in=w_in, hg_lower=hg_lower, hg_norm_g=hg_norm_g, rw_mu=rw_mu,
              rw_w0=rw_w0, rw_w_up=rw_w_up, rw_a0=rw_a0, rw_a_up=rw_a_up, rw_g_up=rw_g_up, rw_k_k=rw_k_k,
              rw_k_a=rw_k_a, rw_r_k=rw_r_k, rw_ln_g=rw_ln_g, rw_ln_b=rw_ln_b, cf_dw=cf_dw, cf_dw_b=cf_dw_b,
              cf_ln_g=cf_ln_g, cf_ln_b=cf_ln_b, lru_conv_w=lru_conv_w, lru_conv_b=lru_conv_b, lru_wa=lru_wa,
              lru_ba=lru_ba, lru_wx=lru_wx, lru_bx=lru_bx, lru_lambda=lru_lambda, w_branch=w_branch,
              w_gate=w_gate, b_gate=b_gate, w_out=w_out, w_mlp1=w_mlp1, w_mlp2=w_mlp2)
    sample_states = (state_hgrn, state_rwkv, state_rwkv_shift, state_conv, state_lru_h, state_lru_conv)
    return _run(x_prompt, x_sample, sample_states, c_prompt, c_sample, wt)
```

```python
import functools
import math

import jax
import jax.numpy as jnp
from jax import lax
from jax.experimental import pallas as pl
from jax.experimental.pallas import tpu as pltpu

D = 1024
W = 512
HG_H = 4
HG_D = 128
RW_H = 8
RW_D = 64
RW_COLS = 1792
RW_PAD = 2048
CF_K = 31
LRU_K = 4
HID = 4096
EPS = 1e-6
RW_GN_EPS = 64e-5
RW_DECAY = 0.606531
LRU_C = 8.0

P_COLS = 10240
VMEM_LIMIT = 56 * 1024 * 1024

F32 = jnp.float32
BF16 = jnp.bfloat16


def _bf(x):
    return x.astype(BF16)


def _mm(a, b):
    return jnp.dot(_bf(a), _bf(b), preferred_element_type=F32)


def _mm_nt(a, b):
    return lax.dot_general(_bf(a), _bf(b), (((1,), (1,)), ((), ())), preferred_element_type=F32)


def _mm_tn(a, b):
    return lax.dot_general(_bf(a), _bf(b), (((0,), (0,)), ((), ())), preferred_element_type=F32)


def _sigmoid(x):
    return 1.0 / (1.0 + jnp.exp(-x))


def _silu(x):
    return x * _sigmoid(x)


def _shift_rows(x, d, fill):
    row = lax.broadcasted_iota(jnp.int32, x.shape, 0)
    return jnp.where(row >= d, pltpu.roll(x, d, axis=0), fill)


def _cumsum_rows(x):
    n = x.shape[0]
    d = 1
    while d < n:
        x = x + _shift_rows(x, d, 0.0)
        d *= 2
    return x


def _headsum(x, ones_bd):
    hi = _bf(x)
    lo = _bf(x - hi.astype(F32))
    return (jnp.dot(hi, ones_bd, preferred_element_type=F32)
            + jnp.dot(lo, ones_bd, preferred_element_type=F32))


def _cparams(sem):
    return pltpu.CompilerParams(dimension_semantics=sem, vmem_limit_bytes=VMEM_LIMIT)


def _mod_kernel(c_ref, w_ref, b_ref, o_ref):
    c = c_ref[...]
    o_ref[0, 0] = _mm(_silu(c), w_ref[0]) + b_ref[0, 0]


def _mod_call(c, ada_w, ada_b):
    nl = ada_w.shape[0]
    bsz = c.shape[0]
    return pl.pallas_call(
        _mod_kernel,
        out_shape=jax.ShapeDtypeStruct((nl, 6, bsz, D), F32),
        grid=(nl, 6),
        in_specs=[
            pl.BlockSpec((bsz, D), lambda l, j: (0, 0)),
            pl.BlockSpec((1, D, D), lambda l, j: (l, 0, j)),
            pl.BlockSpec((1, 1, 1, D), lambda l, j: (l, j, 0, 0)),
        ],
        out_specs=pl.BlockSpec((1, 1, bsz, D), lambda l, j: (l, j, 0, 0)),
        compiler_params=_cparams(("arbitrary", "arbitrary")),
        name="adaln_mod",
    )(c, ada_w, ada_b.reshape(nl, 6, 1, D))


def _adaln(x, g, sc, sh):
    ms = jnp.mean(x * x, axis=-1, keepdims=True)
    y = x * lax.rsqrt(ms + EPS) * g
    return y * (1.0 + sc) + sh


def _inproj_kernel(x_ref, sc_ref, sh_ref, g_ref, w_ref, b_ref, o_ref, h_scr):
    @pl.when(pl.program_id(2) == 0)
    def _():
        h = _adaln(x_ref[...], g_ref[...], sc_ref[...], sh_ref[...])
        h_scr[...] = _bf(h.reshape(h_scr.shape))

    acc = jnp.dot(h_scr[...], w_ref[...], preferred_element_type=F32) + b_ref[...]
    o_ref[...] = acc.reshape(o_ref.shape)


def _tile(bsz, t, rows):
    if t >= rows:
        return 1, rows
    return min(bsz, rows // t), t


def _inproj_call(x, sc, sh, g, wcat, bcat, rows=1024, tn=1024):
    bsz, t, _ = x.shape
    sb, tt = _tile(bsz, t, rows)
    grid = (bsz // sb, t // tt, P_COLS // tn)
    return pl.pallas_call(
        _inproj_kernel,
        out_shape=jax.ShapeDtypeStruct((bsz, t, P_COLS), F32),
        grid=grid,
        in_specs=[
            pl.BlockSpec((sb, tt, D), lambda i, j, n: (i, j, 0)),
            pl.BlockSpec((sb, 1, D), lambda i, j, n: (i, 0, 0)),
            pl.BlockSpec((sb, 1, D), lambda i, j, n: (i, 0, 0)),
            pl.BlockSpec((1, D), lambda i, j, n: (0, 0)),
            pl.BlockSpec((D, tn), lambda i, j, n: (0, n)),
            pl.BlockSpec((1, tn), lambda i, j, n: (0, n)),
        ],
        out_specs=pl.BlockSpec((sb, tt, tn), lambda i, j, n: (i, j, n)),
        scratch_shapes=[pltpu.VMEM((sb * tt, D), BF16)],
        compiler_params=_cparams(("arbitrary", "arbitrary", "arbitrary")),
        name="inproj",
    )(x, sc, sh, g, wcat, bcat)


HG_SUB = 16


def _hgrn_chunk(qr, fz, iv, og, lb, gn, states):
    c = qr.shape[0]
    sub = min(HG_SUB, c)
    nsub = c // sub
    heads = range(HG_H)
    q = _silu(qr)
    f = lb + (1.0 - lb) * _sigmoid(fz)
    logf = jnp.log(f)
    kf = (1.0 - lb) * _sigmoid(-fz)
    b = _cumsum_rows(logf)
    qe = q * jnp.exp(b)
    b_last = b[c - 1:c, :]
    kdec = kf * jnp.exp(b_last - b)
    e_last = jnp.exp(b_last)
    sl = [slice(h * HG_D, (h + 1) * HG_D) for h in heads]
    o_inter = [_mm_nt(qe[:, sl[h]], states[h]) for h in heads]
    new_states = [states[h] * e_last[:, sl[h]] + _mm_tn(iv[:, sl[h]], kdec[:, sl[h]]) for h in heads]
    pieces = [[] for _ in heads]
    for i in range(nsub):
        r0 = i * sub
        m = b[r0 - 1:r0, :] if i > 0 else jnp.zeros((1, W), F32)
        qs = q[r0:r0 + sub] * jnp.exp(b[r0:r0 + sub] - m)
        kd = kf[r0:r0 + sub] * jnp.exp(jnp.minimum(m - b[r0:r0 + sub], 80.0))
        if i > 0:
            kall = jnp.concatenate([kf[:r0] * jnp.exp(m - b[:r0]), kd], axis=0)
        else:
            kall = kd
        row = lax.broadcasted_iota(jnp.int32, (sub, r0 + sub), 0) + r0
        col = lax.broadcasted_iota(jnp.int32, (sub, r0 + sub), 1)
        causal = col <= row
        scs = [jnp.where(causal, _mm_nt(qs[:, sl[h]], kall[:, sl[h]]), 0.0) for h in heads]
        for h in heads:
            pieces[h].append(_mm(scs[h], iv[:r0 + sub, sl[h]]))
    outs = []
    for h in heads:
        o = o_inter[h] + (jnp.concatenate(pieces[h], axis=0) if nsub > 1 else pieces[h][0])
        outs.append(o * lax.rsqrt(jnp.mean(o * o, axis=-1, keepdims=True) + EPS))
    y = jnp.concatenate(outs, axis=1) * gn * _silu(og)
    return y, new_states


def _hgrn_kernel(l, chunk, p_ref, s0_ref, lower_ref, gn_ref, y_ref, sout_ref, s_scr):
    sb, tt, _ = p_ref.shape
    nch = tt // chunk
    tstep = pl.program_id(1)

    @pl.when(tstep == 0)
    def _():
        def init(i, carry):
            for h in range(HG_H):
                s_scr[i, h] = s0_ref[i, h].T
            return carry
        lax.fori_loop(0, sb, init, 0)

    low = lower_ref[...]
    e = jnp.exp(low - jnp.max(low, axis=0, keepdims=True))
    sm = e / jnp.sum(e, axis=0, keepdims=True)
    lb = jnp.sum(sm[:l + 1], axis=0, keepdims=True) - sm[0:1]
    gn = gn_ref[...]

    def body(i, carry):
        s = i // nch
        r0 = pl.multiple_of((i % nch) * chunk, chunk)
        rows = pl.ds(r0, chunk)
        blk = p_ref[s, rows, :]
        states = [s_scr[s, h] for h in range(HG_H)]
        y, states = _hgrn_chunk(blk[:, 0:W], blk[:, W:2 * W], blk[:, 2 * W:3 * W], blk[:, 3 * W:4 * W], lb, gn,
                                states)
        for h in range(HG_H):
            s_scr[s, h] = states[h]
        y_ref[s, rows, :] = y
        return carry
    lax.fori_loop(0, sb * nch, body, 0)

    @pl.when(tstep == pl.num_programs(1) - 1)
    def _():
        def fin(i, carry):
            for h in range(HG_H):
                sout_ref[i, h] = s_scr[i, h].T
            return carry
        lax.fori_loop(0, sb, fin, 0)


def _hgrn_call(p, s0, hg_lower, gn, l, rows, chunk):
    bsz, t, _ = p.shape
    sb, tt = _tile(bsz, t, rows)
    return pl.pallas_call(
        functools.partial(_hgrn_kernel, l, chunk),
        out_shape=(jax.ShapeDtypeStruct((bsz, t, W), F32),
                   jax.ShapeDtypeStruct((bsz, HG_H, HG_D, HG_D), F32)),
        grid=(bsz // sb, t // tt),
        in_specs=[
            pl.BlockSpec((sb, tt, 4 * W), lambda i, j: (i, j, 2)),
            pl.BlockSpec((sb, HG_H, HG_D, HG_D), lambda i, j: (i, 0, 0, 0)),
            pl.BlockSpec(hg_lower.shape, lambda i, j: (0, 0)),
            pl.BlockSpec((1, W), lambda i, j: (0, 0)),
        ],
        out_specs=(pl.BlockSpec((sb, tt, W), lambda i, j: (i, j, 0)),
                   pl.BlockSpec((sb, HG_H, HG_D, HG_D), lambda i, j: (i, 0, 0, 0))),
        scratch_shapes=[pltpu.VMEM((sb, HG_H, HG_D, HG_D), F32)],
        compiler_params=_cparams(("arbitrary", "arbitrary")),
        name="hgrn2",
    )(p, s0, hg_lower, gn)


def _rwkv_chunk(rw, prev_row, prm, wup, ones_bd, states):
    c = rw.shape[0]
    mu, w0, a0, k_k, k_a, r_k, ln_g, ln_b = prm
    row = lax.broadcasted_iota(jnp.int32, rw.shape, 0)
    prev = jnp.where(row == 0, prev_row, pltpu.roll(rw, 1, axis=0))
    rwm = rw + (prev - rw) * mu
    r, k, v = rwm[:, 0:W], rwm[:, W:2 * W], rwm[:, 2 * W:3 * W]
    lr = rwm[:, 3 * W:3 * W + 256]
    lane = lax.broadcasted_iota(jnp.int32, lr.shape, 1)
    act = jnp.where(lane < 64, jnp.tanh(lr), jnp.where(lane < 128, lr, _sigmoid(lr)))
    up = jnp.dot(_bf(act), wup, preferred_element_type=F32)
    log_w = -RW_DECAY * _sigmoid(w0 + up[:, 0:W])
    a = _sigmoid(a0 + up[:, W:2 * W])
    gate = up[:, 2 * W:3 * W]
    kk = k * k_k
    kk = kk / jnp.maximum(jnp.sqrt(_headsum(kk * kk, ones_bd)), 1e-12)
    k2 = k * (1.0 + (a - 1.0) * k_a)
    kka = kk * a

    logp = _cumsum_rows(log_w)
    logp_last = logp[c - 1:c, :]
    a_t = -kk * jnp.exp(logp - log_w)
    r_t = r * jnp.exp(logp)
    einv = jnp.exp(-logp)
    b_i, k_i = kka * einv, k2 * einv
    elast = jnp.exp(logp_last - logp)
    b_d, k_d = kka * elast, k2 * elast
    p_c = jnp.exp(logp_last)

    rr = lax.broadcasted_iota(jnp.int32, (c, c), 0)
    cc = lax.broadcasted_iota(jnp.int32, (c, c), 1)
    strict = rr > cc
    incl = rr >= cc
    nlev = int(math.log2(c))
    heads = range(RW_H)
    sl = [slice(h * RW_D, (h + 1) * RW_D) for h in heads]
    ar = [jnp.concatenate([a_t[:, sl[h]], r_t[:, sl[h]]], axis=0) for h in heads]
    bk = [jnp.concatenate([b_i[:, sl[h]], k_i[:, sl[h]]], axis=0) for h in heads]
    g = [_mm_nt(ar[h], bk[h]) for h in heads]
    a_s = [_mm_nt(ar[h], states[h]) for h in heads]
    vh = [v[:, sl[h]] for h in heads]
    lp = [jnp.where(strict, g[h][:c, :c], 0.0) for h in heads]
    x = [a_s[h][:c] + _mm(jnp.where(strict, g[h][:c, c:], 0.0), vh[h]) for h in heads]
    for j in range(nlev):
        x = [x[h] + _mm(lp[h], x[h]) for h in heads]
        if j < nlev - 1:
            lp = [_mm(lp[h], lp[h]) for h in heads]
    uv = [jnp.concatenate([x[h], vh[h]], axis=0) for h in heads]
    mrbk = [jnp.concatenate([jnp.where(incl, g[h][c:, :c], 0.0), jnp.where(incl, g[h][c:, c:], 0.0)], axis=1)
            for h in heads]
    ys = [a_s[h][c:] + _mm(mrbk[h], uv[h]) for h in heads]
    new_states = [states[h] * p_c[:, sl[h]]
                  + _mm_tn(uv[h], jnp.concatenate([b_d[:, sl[h]], k_d[:, sl[h]]], axis=0)) for h in heads]
    y = jnp.concatenate(ys, axis=1)
    mean = _headsum(y, ones_bd) * (1.0 / RW_D)
    yc = y - mean
    var = _headsum(yc * yc, ones_bd) * (1.0 / RW_D)
    yn = yc * lax.rsqrt(var + RW_GN_EPS) * ln_g + ln_b
    bonus = _headsum(r * k2 * r_k, ones_bd) * v
    return (yn + bonus) * gate, new_states


def _rwkv_kernel(chunk, p_ref, s0_ref, sh0_ref, mu_ref, vec_ref, wup_ref, ones_ref,
                 y_ref, sout_ref, shout_ref, s_scr, prev_scr):
    sb, tt, _ = p_ref.shape
    nch = tt // chunk
    tstep = pl.program_id(1)

    @pl.when(tstep == 0)
    def _():
        s_scr[...] = s0_ref[...]
        prev_scr[...] = sh0_ref[...]

    mu = mu_ref[...]
    vec = vec_ref[...]
    prm = (mu,) + tuple(vec[i:i + 1, :] for i in range(7))
    wup = wup_ref[...]
    ones_bd = ones_ref[...]

    def body(i, carry):
        s = i // nch
        r0 = pl.multiple_of((i % nch) * chunk, chunk)
        rows = pl.ds(r0, chunk)
        rw = p_ref[s, rows, :]
        y, states = _rwkv_chunk(rw, prev_scr[s], prm, wup, ones_bd, [s_scr[s, h] for h in range(RW_H)])
        for h in range(RW_H):
            s_scr[s, h] = states[h]
        prev_scr[s] = rw[chunk - 1:chunk, :]
        y_ref[s, rows, :] = y
        return carry
    lax.fori_loop(0, sb * nch, body, 0)

    @pl.when(tstep == pl.num_programs(1) - 1)
    def _():
        sout_ref[...] = s_scr[...]
        shout_ref[...] = prev_scr[...]


def _rwkv_call(p, s0, sh0, mu, vec, wup, ones_bd, rows, chunk):
    bsz, t, _ = p.shape
    sb, tt = _tile(bsz, t, rows)
    y, s_new, sh_new = pl.pallas_call(
        functools.partial(_rwkv_kernel, chunk),
        out_shape=(jax.ShapeDtypeStruct((bsz, t, W), F32),
                   jax.ShapeDtypeStruct((bsz, RW_H, RW_D, RW_D), F32),
                   jax.ShapeDtypeStruct((bsz, 1, RW_PAD), F32)),
        grid=(bsz // sb, t // tt),
        in_specs=[
            pl.BlockSpec((sb, tt, RW_PAD), lambda i, j: (i, j, 3)),
            pl.BlockSpec((sb, RW_H, RW_D, RW_D), lambda i, j: (i, 0, 0, 0)),
            pl.BlockSpec((sb, 1, RW_PAD), lambda i, j: (i, 0, 0)),
            pl.BlockSpec((1, RW_PAD), lambda i, j: (0, 0)),
            pl.BlockSpec((8, W), lambda i, j: (0, 0)),
            pl.BlockSpec((256, 3 * W), lambda i, j: (0, 0)),
            pl.BlockSpec((W, W), lambda i, j: (0, 0)),
        ],
        out_specs=(pl.BlockSpec((sb, tt, W), lambda i, j: (i, j, 0)),
                   pl.BlockSpec((sb, RW_H, RW_D, RW_D), lambda i, j: (i, 0, 0, 0)),
                   pl.BlockSpec((sb, 1, RW_PAD), lambda i, j: (i, 0, 0))),
        scratch_shapes=[pltpu.VMEM((sb, RW_H, RW_D, RW_D), F32),
                        pltpu.VMEM((sb, 1, RW_PAD), F32)],
        compiler_params=_cparams(("arbitrary", "arbitrary")),
        name="rwkv7",
    )(p, s0, sh0, mu, vec, wup, ones_bd)
    return y, s_new, sh_new


CF_PAD = 32
CF_RB = 64


def _conf_kernel(p_ref, s0_ref, dw_ref, vec_ref, y_ref, sout_ref, ext):
    sb, tt, _ = p_ref.shape
    tstep = pl.program_id(1)
    off = CF_PAD - (CF_K - 1)

    @pl.when(tstep == 0)
    def _():
        ext[:, off:CF_PAD, :] = s0_ref[...]

    vec = vec_ref[...]
    bias, ln_g, ln_b = vec[0:1], vec[1:2], vec[2:3]
    dw = dw_ref[...]
    rb = min(CF_RB, tt)

    def body(s, carry):
        blk = p_ref[s]
        ext[s, CF_PAD:CF_PAD + tt, :] = blk[:, 0:W] * _sigmoid(blk[:, W:2 * W])
        for r in range(tt // rb):
            acc = jnp.zeros((rb, W), F32) + bias
            for j in range(CF_K):
                acc = acc + dw[j:j + 1, :] * ext[s, pl.ds(r * rb + off + j, rb), :]
            mean = jnp.mean(acc, axis=-1, keepdims=True)
            xc = acc - mean
            var = jnp.mean(xc * xc, axis=-1, keepdims=True)
            yn = xc * lax.rsqrt(var + 1e-5) * ln_g + ln_b
            y_ref[s, r * rb:(r + 1) * rb, :] = _silu(yn)
        tail = ext[s, tt:tt + CF_PAD, :]
        ext[s, 0:CF_PAD, :] = tail
        return carry
    lax.fori_loop(0, sb, body, 0)

    @pl.when(tstep == pl.num_programs(1) - 1)
    def _():
        sout_ref[...] = ext[:, off:CF_PAD, :]


def _conf_call(p, s0, dw, vec, rows):
    bsz, t, _ = p.shape
    sb, tt = _tile(bsz, t, rows)
    return pl.pallas_call(
        _conf_kernel,
        out_shape=(jax.ShapeDtypeStruct((bsz, t, W), F32),
                   jax.ShapeDtypeStruct((bsz, CF_K - 1, W), F32)),
        grid=(bsz // sb, t // tt),
        in_specs=[
            pl.BlockSpec((sb, tt, 2 * W), lambda i, j: (i, j, 8)),
            pl.BlockSpec((sb, CF_K - 1, W), lambda i, j: (i, 0, 0)),
            pl.BlockSpec((32, W), lambda i, j: (0, 0)),
            pl.BlockSpec((8, W), lambda i, j: (0, 0)),
        ],
        out_specs=(pl.BlockSpec((sb, tt, W), lambda i, j: (i, j, 0)),
                   pl.BlockSpec((sb, CF_K - 1, W), lambda i, j: (i, 0, 0))),
        scratch_shapes=[pltpu.VMEM((sb, CF_PAD + tt, W), F32)],
        compiler_params=_cparams(("arbitrary", "arbitrary")),
        name="conformer",
    )(p, s0, dw, vec)


LRU_PAD = 8


def _gelu_tanh(x):
    return 0.5 * x * (1.0 + jnp.tanh(0.7978845608028654 * (x + 0.044715 * (x * x * x))))


def _lru_kernel(p_ref, h0_ref, c0_ref, cw_ref, vec_ref, wax_ref, y_ref, hout_ref, cout_ref, ext, hcar):
    sb, tt, _ = p_ref.shape
    tstep = pl.program_id(1)
    off = LRU_PAD - (LRU_K - 1)

    @pl.when(tstep == 0)
    def _():
        ext[:, off:LRU_PAD, :] = c0_ref[...]
        hcar[...] = h0_ref[...]

    vec = vec_ref[...]
    cb, ba, bx, lam = vec[0:1], vec[1:2], vec[2:3], vec[3:4]
    sp = jnp.maximum(-lam, 0.0) + jnp.log1p(jnp.exp(-jnp.abs(lam)))
    cw = cw_ref[...]
    wax = wax_ref[...]

    def body(s, carry):
        blk = p_ref[s]
        ext[s, LRU_PAD:LRU_PAD + tt, :] = blk[:, 0:W]
        xc = jnp.zeros((tt, W), F32) + cb
        for j in range(LRU_K):
            xc = xc + cw[j:j + 1, :] * ext[s, pl.ds(off + j, tt), :]
        pre = jnp.dot(_bf(xc), wax, preferred_element_type=F32)
        rg = _sigmoid(pre[:, 0:W] + ba)
        ig = _sigmoid(pre[:, W:2 * W] + bx)
        log_a = -LRU_C * rg * sp
        a = jnp.exp(log_a)
        bv = jnp.sqrt(1.0 - a * a) * (ig * xc)
        d = 1
        while d < tt:
            a_s = _shift_rows(a, d, 1.0)
            b_s = _shift_rows(bv, d, 0.0)
            bv = a * b_s + bv
            a = a * a_s
            d *= 2
        hs = a * hcar[s] + bv
        hcar[s] = hs[tt - 1:tt, :]
        y_ref[s] = hs * _gelu_tanh(blk[:, W:2 * W])
        tail = ext[s, tt:tt + LRU_PAD, :]
        ext[s, 0:LRU_PAD, :] = tail
        return carry
    lax.fori_loop(0, sb, body, 0)

    @pl.when(tstep == pl.num_programs(1) - 1)
    def _():
        hout_ref[...] = hcar[...]
        cout_ref[...] = ext[:, off:LRU_PAD, :]


def _lru_call(p, h0, c0, cw, vec, wax, rows):
    bsz, t, _ = p.shape
    sb, tt = _tile(bsz, t, rows)
    return pl.pallas_call(
        _lru_kernel,
        out_shape=(jax.ShapeDtypeStruct((bsz, t, W), F32),
                   jax.ShapeDtypeStruct((bsz, 1, W), F32),
                   jax.ShapeDtypeStruct((bsz, LRU_K - 1, W), F32)),
        grid=(bsz // sb, t // tt),
        in_specs=[
            pl.BlockSpec((sb, tt, 2 * W), lambda i, j: (i, j, 9)),
            pl.BlockSpec((sb, 1, W), lambda i, j: (i, 0, 0)),
            pl.BlockSpec((sb, LRU_K - 1, W), lambda i, j: (i, 0, 0)),
            pl.BlockSpec((8, W), lambda i, j: (0, 0)),
            pl.BlockSpec((8, W), lambda i, j: (0, 0)),
            pl.BlockSpec((W, 2 * W), lambda i, j: (0, 0)),
        ],
        out_specs=(pl.BlockSpec((sb, tt, W), lambda i, j: (i, j, 0)),
                   pl.BlockSpec((sb, 1, W), lambda i, j: (i, 0, 0)),
                   pl.BlockSpec((sb, LRU_K - 1, W), lambda i, j: (i, 0, 0))),
        scratch_shapes=[pltpu.VMEM((sb, LRU_PAD + tt, W), F32),
                        pltpu.VMEM((sb, 1, W), F32)],
        compiler_params=_cparams(("arbitrary", "arbitrary")),
        name="rglru",
    )(p, h0, c0, cw, vec, wax)


def _merge_kernel(gl_ref, yhg_ref, yrw_ref, ycf_ref, ylr_ref, x_ref, g1_ref, sc2_ref, sh2_ref, ng_ref,
                  wb_ref, wo_ref, x1_ref, h2_ref):
    sb, tt, _ = x_ref.shape
    tm = sb * tt
    z = jnp.zeros((tm, D), F32)
    for b, y_ref in enumerate((yhg_ref, yrw_ref, ycf_ref, ylr_ref)):
        bo = jnp.dot(_bf(y_ref[...].reshape(tm, W)), wb_ref[b], preferred_element_type=F32)
        gate = _sigmoid(gl_ref[:, :, b * D:(b + 1) * D].reshape(tm, D))
        z = z + gate * bo
    out = jnp.dot(_bf(z), wo_ref[...], preferred_element_type=F32).reshape(sb, tt, D)
    x1 = x_ref[...] + g1_ref[...] * out
    x1_ref[...] = x1
    h2_ref[...] = _bf(_adaln(x1, ng_ref[...], sc2_ref[...], sh2_ref[...]))


def _merge_call(p, ys, x, g1, sc2, sh2, ng, wb, wo, rows=512):
    bsz, t, _ = x.shape
    sb, tt = _tile(bsz, t, rows)
    tok = lambda width: pl.BlockSpec((sb, tt, width), lambda i, j: (i, j, 0))
    seq = pl.BlockSpec((sb, 1, D), lambda i, j: (i, 0, 0))
    return pl.pallas_call(
        _merge_kernel,
        out_shape=(jax.ShapeDtypeStruct((bsz, t, D), F32),
                   jax.ShapeDtypeStruct((bsz, t, D), BF16)),
        grid=(bsz // sb, t // tt),
        in_specs=[tok(4 * D), tok(W), tok(W), tok(W), tok(W), tok(D), seq, seq, seq,
                  pl.BlockSpec((1, D), lambda i, j: (0, 0)),
                  pl.BlockSpec((4, W, D), lambda i, j: (0, 0, 0)),
                  pl.BlockSpec((D, D), lambda i, j: (0, 0))],
        out_specs=(tok(D), tok(D)),
        compiler_params=_cparams(("arbitrary", "arbitrary")),
        name="merge",
    )(p, *ys, x, g1, sc2, sh2, ng, wb, wo)


def _mlp_kernel(final, h2_ref, x1_ref, g2_ref, fg_ref, w1_ref, w2_ref, o_ref, acc):
    sb, tt, _ = x1_ref.shape
    k = pl.program_id(2)

    @pl.when(k == 0)
    def _():
        acc[...] = jnp.zeros_like(acc)

    hid = jnp.dot(h2_ref[...].reshape(sb * tt, D), w1_ref[...], preferred_element_type=F32)
    act = jnp.square(jnp.maximum(hid, 0.0))
    acc[...] += jnp.dot(_bf(act), w2_ref[...], preferred_element_type=F32)

    @pl.when(k == pl.num_programs(2) - 1)
    def _():
        x2 = x1_ref[...] + g2_ref[...] * acc[...].reshape(sb, tt, D)
        if final:
            ms = jnp.mean(x2 * x2, axis=-1, keepdims=True)
            x2 = x2 * lax.rsqrt(ms + EPS) * fg_ref[...]
        o_ref[...] = x2


def _mlp_call(h2, x1, g2, fg, w1, w2, final, rows=1024, th=1024):
    bsz, t, _ = x1.shape
    sb, tt = _tile(bsz, t, rows)
    tok = pl.BlockSpec((sb, tt, D), lambda i, j, k: (i, j, 0))
    return pl.pallas_call(
        functools.partial(_mlp_kernel, final),
        out_shape=jax.ShapeDtypeStruct((bsz, t, D), F32),
        grid=(bsz // sb, t // tt, HID // th),
        in_specs=[tok, tok,
                  pl.BlockSpec((sb, 1, D), lambda i, j, k: (i, 0, 0)),
                  pl.BlockSpec((1, D), lambda i, j, k: (0, 0)),
                  pl.BlockSpec((D, th), lambda i, j, k: (0, k)),
                  pl.BlockSpec((th, D), lambda i, j, k: (k, 0))],
        out_specs=tok,
        scratch_shapes=[pltpu.VMEM((sb * tt, D), F32)],
        compiler_params=_cparams(("arbitrary", "arbitrary", "arbitrary")),
        name="mlp",
    )(h2, x1, g2, fg, w1, w2)


def _block_diag(w):
    n, c, d = w.shape
    eye = jnp.eye(n, dtype=w.dtype)
    return (eye[:, None, :, None] * w[:, :, None, :]).reshape(n * c, n * d)


def _prep_layer(wt, l):
    w_in = wt["w_in"][l]
    zpad = jnp.zeros((D, RW_PAD - RW_COLS), F32)
    wcat = jnp.concatenate([wt["w_gate"][l], w_in[:, 0:2048], w_in[:, 2048:2048 + RW_COLS], zpad,
                            w_in[:, 2048 + RW_COLS:]], axis=1)
    bcat = jnp.concatenate([wt["b_gate"][l], jnp.zeros((P_COLS - 4 * D,), F32)])[None, :]
    wup = jnp.zeros((256, 3 * W), F32)
    wup = wup.at[0:64, 0:W].set(wt["rw_w_up"][l])
    wup = wup.at[64:128, W:2 * W].set(wt["rw_a_up"][l])
    wup = wup.at[128:256, 2 * W:3 * W].set(wt["rw_g_up"][l])
    zrow = jnp.zeros((W,), F32)
    rw_vec = jnp.stack([wt["rw_w0"][l], wt["rw_a0"][l], wt["rw_k_k"][l], wt["rw_k_a"][l], wt["rw_r_k"][l],
                        wt["rw_ln_g"][l], wt["rw_ln_b"][l], zrow])
    mu = jnp.concatenate([wt["rw_mu"][l], jnp.zeros((RW_PAD - RW_COLS,), F32)])[None, :]
    cf_vec = jnp.stack([wt["cf_dw_b"][l], wt["cf_ln_g"][l], wt["cf_ln_b"][l]] + [zrow] * 5)
    cf_dw = jnp.concatenate([wt["cf_dw"][l], jnp.zeros((1, W), F32)], axis=0)
    lru_vec = jnp.stack([wt["lru_conv_b"][l], wt["lru_ba"][l], wt["lru_bx"][l], wt["lru_lambda"][l]] + [zrow] * 4)
    lru_cw = jnp.concatenate([wt["lru_conv_w"][l], jnp.zeros((8 - LRU_K, W), F32)], axis=0)
    wax = jnp.concatenate([_block_diag(wt["lru_wa"][l]), _block_diag(wt["lru_wx"][l])], axis=1)
    return dict(
        wcat=_bf(wcat), bcat=bcat, wup=_bf(wup), rw_vec=rw_vec, mu=mu, cf_vec=cf_vec, cf_dw=cf_dw,
        lru_vec=lru_vec, lru_cw=lru_cw, wax=_bf(wax),
        wb=_bf(wt["w_branch"][l]), wo=_bf(wt["w_out"][l]), w1=_bf(wt["w_mlp1"][l]), w2=_bf(wt["w_mlp2"][l]),
        hg_gn=wt["hg_norm_g"][l][None, :], nmix=wt["norm_mix_g"][l][None, :], nmlp=wt["norm_mlp_g"][l][None, :],
    )


def _trunk(x, c, states, wt, layers, ones_bd, mix_rows, chunk):
    s_hg, s_rw, s_shift, s_cf, s_lh, s_lc = states
    bsz = x.shape[0]
    nl = len(layers)
    mod = _mod_call(c, wt["ada_w"], wt["ada_b"])
    new = [[] for _ in range(6)]
    fg = wt["norm_final_g"][None, :]
    for l, lw in enumerate(layers):
        sh1, sc1, g1, sh2, sc2, g2 = [mod[l, i][:, None, :] for i in range(6)]
        p = _inproj_call(x, sc1, sh1, lw["nmix"], lw["wcat"], lw["bcat"])
        y_hg, n_hg = _hgrn_call(p, s_hg[l], wt["hg_lower"], lw["hg_gn"], l, mix_rows, chunk)
        sh0 = jnp.pad(s_shift[l], ((0, 0), (0, RW_PAD - RW_COLS)))[:, None, :]
        y_rw, n_rw, n_sh = _rwkv_call(p, s_rw[l], sh0, lw["mu"], lw["rw_vec"], lw["wup"], ones_bd,
                                      mix_rows, chunk)
        y_cf, n_cf = _conf_call(p, s_cf[l], lw["cf_dw"], lw["cf_vec"], mix_rows)
        y_lr, n_lh, n_lc = _lru_call(p, s_lh[l][:, None, :], s_lc[l], lw["lru_cw"], lw["lru_vec"], lw["wax"],
                                     mix_rows)
        x1, h2 = _merge_call(p, (y_hg, y_rw, y_cf, y_lr), x, g1, sc2, sh2, lw["nmlp"], lw["wb"], lw["wo"])
        x = _mlp_call(h2, x1, g2, fg, lw["w1"], lw["w2"], final=(l == nl - 1))
        for lst, s in zip(new, (n_hg, n_rw, n_sh[:, 0, :RW_COLS], n_cf, n_lh[:, 0, :], n_lc)):
            lst.append(s)
    return x, [jnp.stack(lst, axis=0) for lst in new]


def _run(x_prompt, x_sample, sample_states, c_prompt, c_sample, wt):
    nl = wt["w_in"].shape[0]
    layers = [_prep_layer(wt, l) for l in range(nl)]
    head = jnp.arange(W, dtype=jnp.int32) // RW_D
    ones_bd = _bf((head[:, None] == head[None, :]).astype(F32))
    bp = x_prompt.shape[0]
    zero_states = (
        jnp.zeros((nl, bp, HG_H, HG_D, HG_D), F32),
        jnp.zeros((nl, bp, RW_H, RW_D, RW_D), F32),
        jnp.zeros((nl, bp, RW_COLS), F32),
        jnp.zeros((nl, bp, CF_K - 1, W), F32),
        jnp.zeros((nl, bp, W), F32),
        jnp.zeros((nl, bp, LRU_K - 1, W), F32),
    )
    tp = x_prompt.shape[1]
    ts = x_sample.shape[1]
    y_p, st_p = _trunk(x_prompt, c_prompt, zero_states, wt, layers, ones_bd, min(256, tp), min(64, tp))
    y_s, st_s = _trunk(x_sample, c_sample, sample_states, wt, layers, ones_bd, 8 * ts, ts)
    return (y_p, y_s, *st_p, *st_s)


def kernel(x_prompt, x_sample, state_hgrn, state_rwkv, state_rwkv_shift, state_conv, state_lru_h, state_lru_conv, c_prompt, c_sample, ada_w, ada_b, norm_mix_g, norm_mlp_g, norm_final_g, w_in, hg_lower, hg_norm_g, rw_mu, rw_w0, rw_w_up, rw_a0, rw_a_up, rw_g_up, rw_k_k, rw_k_a, rw_r_k, rw_ln_g, rw_ln_b, cf_dw, cf_dw_b, cf_ln_g, cf_ln_b, lru_conv_w, lru_conv_b, lru_wa, lru_ba, lru_wx, lru_bx, lru_lambda, w_branch, w_gate, b_gate, w_out, w_mlp1, w_mlp2):
    wt = dict(ada_w=ada_w, ada_b=ada_b, norm_mix_g=norm_mix_g, norm_mlp_g=norm_mlp_g,
              norm_final_g=norm_final_g, w_in=w_in, hg_lower=hg_lower, hg_norm_g=hg_norm_g, rw_mu=rw_mu,
              rw_w0=rw_w0, rw_w_up=rw_w_up, rw_a0=rw_a0, rw_a_up=rw_a_up, rw_g_up=rw_g_up, rw_k_k=rw_k_k,
              rw_k_a=rw_k_a, rw_r_k=rw_r_k, rw_ln_g=rw_ln_g, rw_ln_b=rw_ln_b, cf_dw=cf_dw, cf_dw_b=cf_dw_b,
              cf_ln_g=cf_ln_g, cf_ln_b=cf_ln_b, lru_conv_w=lru_conv_w, lru_conv_b=lru_conv_b, lru_wa=lru_wa,
              lru_ba=lru_ba, lru_wx=lru_wx, lru_bx=lru_bx, lru_lambda=lru_lambda, w_branch=w_branch,
              w_gate=w_gate, b_gate=b_gate, w_out=w_out, w_mlp1=w_mlp1, w_mlp2=w_mlp2)
    sample_states = (state_hgrn, state_rwkv, state_rwkv_shift, state_conv, state_lru_h, state_lru_conv)
    return _run(x_prompt, x_sample, sample_states, c_prompt, c_sample, wt)
```

```python
import functools
import math

import jax
import jax.numpy as jnp
from jax import lax
from jax.experimental import pallas as pl
from jax.experimental.pallas import tpu as pltpu

D = 1024
W = 512
HG_H = 4
HG_D = 128
RW_H = 8
RW_D = 64
RW_COLS = 1792
RW_PAD = 2048
CF_K = 31
LRU_K = 4
HID = 4096
EPS = 1e-6
RW_GN_EPS = 64e-5
RW_DECAY = 0.606531
LRU_C = 8.0
LANES = 128

P_COLS = 10240
VMEM_LIMIT = 56 * 1024 * 1024

F32 = jnp.float32
BF16 = jnp.bfloat16


def _bf(x):
    return x.astype(BF16)


def _mm(a, b):
    return jnp.dot(_bf(a), _bf(b), preferred_element_type=F32)


def _mm_nt(a, b):
    return lax.dot_general(_bf(a), _bf(b), (((1,), (1,)), ((), ())), preferred_element_type=F32)


def _mm_tn(a, b):
    return lax.dot_general(_bf(a), _bf(b), (((0,), (0,)), ((), ())), preferred_element_type=F32)


def _sigmoid(x):
    return 1.0 / (1.0 + jnp.exp(-x))


def _silu(x):
    return x * _sigmoid(x)


def _local_row(shape, seg):
    return jnp.bitwise_and(lax.broadcasted_iota(jnp.int32, shape, 0), seg - 1)


def _shift_rows(x, d, fill, seg):
    return jnp.where(_local_row(x.shape, seg) >= d, pltpu.roll(x, d, axis=0), fill)


def _cumsum_rows(x, seg):
    d = 1
    while d < seg:
        x = x + _shift_rows(x, d, 0.0, seg)
        d *= 2
    return x


def _seg_last(x, seg):
    n = x.shape[0]
    local = _local_row(x.shape, seg)
    d = 1
    while d < seg:
        x = jnp.where(local + d < seg, pltpu.roll(x, n - d, axis=0), x)
        d *= 2
    return x


def _seg_id(shape, axis, log2_seg):
    return lax.shift_right_logical(lax.broadcasted_iota(jnp.int32, shape, axis), log2_seg)


def _headsum(x, ones_bd):
    hi = _bf(x)
    lo = _bf(x - hi.astype(F32))
    return (jnp.dot(hi, ones_bd, preferred_element_type=F32)
            + jnp.dot(lo, ones_bd, preferred_element_type=F32))


def _cparams(sem):
    return pltpu.CompilerParams(dimension_semantics=sem, vmem_limit_bytes=VMEM_LIMIT)


def _tile(bsz, t, rows):
    if t >= rows:
        return 1, rows
    return min(bsz, rows // t), t


def _state_specs(l, sb, tail):
    zeros = (0,) * len(tail)
    in_spec = pl.BlockSpec((None, sb) + tail, lambda i, j: (l, i) + zeros)
    prev_spec = pl.BlockSpec((l, sb) + tail, lambda i, j: (0, i) + zeros)
    out_spec = pl.BlockSpec((l + 1, sb) + tail, lambda i, j: (0, i) + zeros)
    return in_spec, prev_spec, out_spec


def _emit_state(l, out_ref, prev_ref, new):
    for q in range(l):
        out_ref[q] = prev_ref[q]
    out_ref[l] = new


def _mod_kernel(c_ref, w_ref, b_ref, o_ref):
    c = c_ref[...]
    o_ref[0, 0] = _mm(_silu(c), w_ref[0]) + b_ref[0, 0]


def _mod_call(c, ada_w, ada_b):
    nl = ada_w.shape[0]
    bsz = c.shape[0]
    return pl.pallas_call(
        _mod_kernel,
        out_shape=jax.ShapeDtypeStruct((nl, 6, bsz, D), F32),
        grid=(nl, 6),
        in_specs=[
            pl.BlockSpec((bsz, D), lambda l, j: (0, 0)),
            pl.BlockSpec((1, D, D), lambda l, j: (l, 0, j)),
            pl.BlockSpec((1, 1, 1, D), lambda l, j: (l, j, 0, 0)),
        ],
        out_specs=pl.BlockSpec((1, 1, bsz, D), lambda l, j: (l, j, 0, 0)),
        compiler_params=_cparams(("arbitrary", "arbitrary")),
        name="adaln_mod",
    )(c, ada_w, ada_b.reshape(nl, 6, 1, D))


def _adaln(x, g, sc, sh):
    ms = jnp.mean(x * x, axis=-1, keepdims=True)
    y = x * lax.rsqrt(ms + EPS) * g
    return y * (1.0 + sc) + sh


def _inproj_kernel(x_ref, sc_ref, sh_ref, g_ref, w_ref, b_ref, o_ref, h_scr):
    @pl.when(pl.program_id(2) == 0)
    def _():
        h = _adaln(x_ref[...], g_ref[...], sc_ref[...], sh_ref[...])
        h_scr[...] = _bf(h.reshape(h_scr.shape))

    acc = jnp.dot(h_scr[...], w_ref[...], preferred_element_type=F32) + b_ref[...]
    o_ref[...] = acc.reshape(o_ref.shape)


def _inproj_call(x, sc, sh, g, wcat, bcat, rows=1024, tn=1024):
    bsz, t, _ = x.shape
    sb, tt = _tile(bsz, t, rows)
    grid = (bsz // sb, t // tt, P_COLS // tn)
    return pl.pallas_call(
        _inproj_kernel,
        out_shape=jax.ShapeDtypeStruct((bsz, t, P_COLS), F32),
        grid=grid,
        in_specs=[
            pl.BlockSpec((sb, tt, D), lambda i, j, n: (i, j, 0)),
            pl.BlockSpec((sb, 1, D), lambda i, j, n: (i, 0, 0)),
            pl.BlockSpec((sb, 1, D), lambda i, j, n: (i, 0, 0)),
            pl.BlockSpec((1, D), lambda i, j, n: (0, 0)),
            pl.BlockSpec((D, tn), lambda i, j, n: (0, n)),
            pl.BlockSpec((1, tn), lambda i, j, n: (0, n)),
        ],
        out_specs=pl.BlockSpec((sb, tt, tn), lambda i, j, n: (i, j, n)),
        scratch_shapes=[pltpu.VMEM((sb * tt, D), BF16)],
        compiler_params=_cparams(("arbitrary", "arbitrary", "arbitrary")),
        name="inproj",
    )(x, sc, sh, g, wcat, bcat)


HG_SUB = 16


def _hgrn_chunk(qr, fz, iv, og, lb, gn, states, seg):
    c = qr.shape[0]
    nseg = c // seg
    heads = range(HG_H)
    q = _silu(qr)
    f = lb + (1.0 - lb) * _sigmoid(fz)
    logf = jnp.log(f)
    kf = (1.0 - lb) * _sigmoid(-fz)
    b = _cumsum_rows(logf, seg)
    qe = q * jnp.exp(b)
    b_last = b[c - 1:c, :] if nseg == 1 else _seg_last(b, seg)
    kdec = kf * jnp.exp(b_last - b)
    e_last = jnp.exp(b_last)
    sl = [slice(h * HG_D, (h + 1) * HG_D) for h in heads]

    if nseg == 1:
        sub = min(HG_SUB, c)
        nsub = c // sub
        o_inter = [_mm_nt(qe[:, sl[h]], states[h]) for h in heads]
        new_states = [states[h] * e_last[:, sl[h]] + _mm_tn(iv[:, sl[h]], kdec[:, sl[h]]) for h in heads]
        pieces = [[] for _ in heads]
        for i in range(nsub):
            r0 = i * sub
            m = b[r0 - 1:r0, :] if i > 0 else jnp.zeros((1, W), F32)
            qs = q[r0:r0 + sub] * jnp.exp(b[r0:r0 + sub] - m)
            kd = kf[r0:r0 + sub] * jnp.exp(jnp.minimum(m - b[r0:r0 + sub], 80.0))
            if i > 0:
                kall = jnp.concatenate([kf[:r0] * jnp.exp(m - b[:r0]), kd], axis=0)
            else:
                kall = kd
            row = lax.broadcasted_iota(jnp.int32, (sub, r0 + sub), 0) + r0
            col = lax.broadcasted_iota(jnp.int32, (sub, r0 + sub), 1)
            causal = col <= row
            scs = [jnp.where(causal, _mm_nt(qs[:, sl[h]], kall[:, sl[h]]), 0.0) for h in heads]
            for h in heads:
                pieces[h].append(_mm(scs[h], iv[:r0 + sub, sl[h]]))
        o_intra = [jnp.concatenate(pieces[h], axis=0) if nsub > 1 else pieces[h][0] for h in heads]
    else:
        lg = int(math.log2(seg))
        kd = kf * jnp.exp(jnp.minimum(-b, 80.0))
        rr = lax.broadcasted_iota(jnp.int32, (c, c), 0)
        cc = lax.broadcasted_iota(jnp.int32, (c, c), 1)
        causal = (lax.shift_right_logical(rr, lg) == lax.shift_right_logical(cc, lg)) & (cc <= rr)
        rowseg = _seg_id((c, HG_D), 0, lg)
        blockmask = _seg_id((c, nseg * HG_D), 0, lg) == _seg_id((c, nseg * HG_D), 1, 7)
        e3 = e_last.reshape(nseg, seg, W)[:, 0:1, :]
        full = [_mm_nt(qe[:, sl[h]], states[h].reshape(nseg * HG_D, HG_D)) for h in heads]
        o_inter = []
        for h in heads:
            o = full[h][:, 0:HG_D]
            for s in range(1, nseg):
                o = jnp.where(rowseg == s, full[h][:, s * HG_D:(s + 1) * HG_D], o)
            o_inter.append(o)
        scs = [jnp.where(causal, _mm_nt(qe[:, sl[h]], kd[:, sl[h]]), 0.0) for h in heads]
        o_intra = [_mm(scs[h], iv[:, sl[h]]) for h in heads]
        new_states = []
        for h in heads:
            ivexp = jnp.where(blockmask, jnp.concatenate([iv[:, sl[h]]] * nseg, axis=1), 0.0)
            upd = _mm_tn(ivexp, kdec[:, sl[h]]).reshape(nseg, HG_D, HG_D)
            new_states.append(states[h] * e3[:, :, sl[h]] + upd)
    outs = []
    for h in heads:
        o = o_inter[h] + o_intra[h]
        outs.append(o * lax.rsqrt(jnp.mean(o * o, axis=-1, keepdims=True) + EPS))
    y = jnp.concatenate(outs, axis=1) * gn * _silu(og)
    return y, new_states


def _hgrn_kernel(cfg, *refs):
    l, chunk, seg, zero_init = cfg
    refs = list(refs)
    p_ref = refs.pop(0)
    s0_ref = None if zero_init else refs.pop(0)
    lower_ref, gn_ref = refs.pop(0), refs.pop(0)
    prev_ref = refs.pop(0) if l > 0 else None
    y_ref, sout_ref, s_scr = refs
    sb, tt, _ = p_ref.shape
    nseg = chunk // seg
    nchunks = sb * tt // chunk
    tstep = pl.program_id(1)

    @pl.when(tstep == 0)
    def _():
        if zero_init:
            s_scr[...] = jnp.zeros_like(s_scr)
        else:
            def init(i, carry):
                for h in range(HG_H):
                    s_scr[i, h] = s0_ref[i, h].T
                return carry
            lax.fori_loop(0, sb, init, 0)

    low = lower_ref[...]
    e = jnp.exp(low - jnp.max(low, axis=0, keepdims=True))
    sm = e / jnp.sum(e, axis=0, keepdims=True)
    lb = jnp.sum(sm[:l + 1], axis=0, keepdims=True) - sm[0:1]
    gn = gn_ref[...]

    def body(i, carry):
        if nseg == 1:
            r0 = pl.multiple_of(i * chunk, chunk)
            blk = p_ref[0, pl.ds(r0, chunk), :]
            states = [s_scr[0, h] for h in range(HG_H)]
        else:
            s0 = pl.multiple_of(i * nseg, nseg)
            blk = p_ref[pl.ds(s0, nseg)].reshape(chunk, 4 * W)
            states = [s_scr[pl.ds(s0, nseg), h] for h in range(HG_H)]
        y, states = _hgrn_chunk(blk[:, 0:W], blk[:, W:2 * W], blk[:, 2 * W:3 * W], blk[:, 3 * W:4 * W], lb, gn,
                                states, seg)
        for h in range(HG_H):
            if nseg == 1:
                s_scr[0, h] = states[h]
            else:
                s_scr[pl.ds(s0, nseg), h] = states[h]
        if nseg == 1:
            y_ref[0, pl.ds(r0, chunk), :] = y
        else:
            y_ref[pl.ds(s0, nseg)] = y.reshape(nseg, seg, W)
        return carry
    lax.fori_loop(0, nchunks, body, 0)

    @pl.when(tstep == pl.num_programs(1) - 1)
    def _():
        for q in range(l):
            sout_ref[q] = prev_ref[q]

        def fin(i, carry):
            for h in range(HG_H):
                sout_ref[l, i, h] = s_scr[i, h].T
            return carry
        lax.fori_loop(0, sb, fin, 0)


def _hgrn_call(p, s_in, prev, hg_lower, gn, l, rows, chunk, seg):
    bsz, t, _ = p.shape
    sb, tt = _tile(bsz, t, rows)
    assert seg < chunk or sb == 1
    zero_init = s_in is None
    in_spec, prev_spec, out_spec = _state_specs(l, sb, (HG_H, HG_D, HG_D))
    args = [p] + ([] if zero_init else [s_in]) + [hg_lower, gn] + ([prev] if l > 0 else [])
    specs = ([pl.BlockSpec((sb, tt, 4 * W), lambda i, j: (i, j, 2))] + ([] if zero_init else [in_spec])
             + [pl.BlockSpec(hg_lower.shape, lambda i, j: (0, 0)), pl.BlockSpec((1, W), lambda i, j: (0, 0))]
             + ([prev_spec] if l > 0 else []))
    return pl.pallas_call(
        functools.partial(_hgrn_kernel, (l, chunk, seg, zero_init)),
        out_shape=(jax.ShapeDtypeStruct((bsz, t, W), F32),
                   jax.ShapeDtypeStruct((l + 1, bsz, HG_H, HG_D, HG_D), F32)),
        grid=(bsz // sb, t // tt),
        in_specs=specs,
        out_specs=(pl.BlockSpec((sb, tt, W), lambda i, j: (i, j, 0)), out_spec),
        scratch_shapes=[pltpu.VMEM((sb, HG_H, HG_D, HG_D), F32)],
        compiler_params=_cparams(("arbitrary", "arbitrary")),
        name="hgrn2",
    )(*args)


def _pick64(full, rowseg):
    nb = full.shape[1] // LANES
    half = lax.shift_right_logical(rowseg, 1)
    sel = full[:, 0:LANES]
    for j in range(1, nb):
        sel = jnp.where(half == j, full[:, j * LANES:(j + 1) * LANES], sel)
    sel = jnp.where(jnp.bitwise_and(rowseg, 1) == 1, pltpu.roll(sel, RW_D, axis=1), sel)
    return sel[:, 0:RW_D]


def _rwkv_recur(at, rt, bi, ki, bd, kd, v, states, pc, seg):
    c = at.shape[0]
    nseg = c // seg
    lg = int(math.log2(seg))
    rr = lax.broadcasted_iota(jnp.int32, (c, c), 0)
    cc = lax.broadcasted_iota(jnp.int32, (c, c), 1)
    if nseg == 1:
        strict = rr > cc
        incl = rr >= cc
    else:
        same = lax.shift_right_logical(rr, lg) == lax.shift_right_logical(cc, lg)
        strict = same & (rr > cc)
        incl = same & (rr >= cc)
        r2 = jnp.bitwise_and(lax.broadcasted_iota(jnp.int32, (2 * c, LANES), 0), c - 1)
        rowseg = lax.shift_right_logical(r2, lg)
        rb = jnp.bitwise_and(lax.broadcasted_iota(jnp.int32, (2 * c, nseg * RW_D), 0), c - 1)
        blockmask = lax.shift_right_logical(rb, lg) == _seg_id((2 * c, nseg * RW_D), 1, 6)
    heads = range(RW_H)
    sl = [slice(h * RW_D, (h + 1) * RW_D) for h in heads]
    ar = [jnp.concatenate([at[:, sl[h]], rt[:, sl[h]]], axis=0) for h in heads]
    bk = [jnp.concatenate([bi[:, sl[h]], ki[:, sl[h]]], axis=0) for h in heads]
    g = [_mm_nt(ar[h], bk[h]) for h in heads]
    if nseg == 1:
        a_s = [_mm_nt(ar[h], states[h]) for h in heads]
    else:
        a_s = [_pick64(_mm_nt(ar[h], states[h].reshape(nseg * RW_D, RW_D)), rowseg) for h in heads]
    vh = [v[:, sl[h]] for h in heads]
    lp = [jnp.where(strict, g[h][:c, :c], 0.0) for h in heads]
    x = [a_s[h][:c] + _mm(jnp.where(strict, g[h][:c, c:], 0.0), vh[h]) for h in heads]
    for j in range(lg):
        x = [x[h] + _mm(lp[h], x[h]) for h in heads]
        if j < lg - 1:
            lp = [_mm(lp[h], lp[h]) for h in heads]
    uv = [jnp.concatenate([x[h], vh[h]], axis=0) for h in heads]
    mrbk = [jnp.concatenate([jnp.where(incl, g[h][c:, :c], 0.0), jnp.where(incl, g[h][c:, c:], 0.0)], axis=1)
            for h in heads]
    ys = [a_s[h][c:] + _mm(mrbk[h], uv[h]) for h in heads]
    bkd = [jnp.concatenate([bd[:, sl[h]], kd[:, sl[h]]], axis=0) for h in heads]
    if nseg == 1:
        new_states = [states[h] * pc[:, sl[h]] + _mm_tn(uv[h], bkd[h]) for h in heads]
    else:
        new_states = []
        for h in heads:
            u2 = jnp.concatenate([uv[h], uv[h]], axis=1)
            uvexp = jnp.where(blockmask, jnp.concatenate([u2] * (nseg // 2), axis=1), 0.0)
            upd = _mm_tn(uvexp, bkd[h]).reshape(nseg, RW_D, RW_D)
            new_states.append(states[h] * pc[:, :, sl[h]] + upd)
    return ys, new_states


def _rwkv_kernel(cfg, *refs):
    l, chunk, seg, zero_init = cfg
    refs = list(refs)
    p_ref = refs.pop(0)
    s0_ref, sh0_ref = (None, None) if zero_init else (refs.pop(0), refs.pop(0))
    mu_ref, vec_ref, wup_ref, ones_ref = refs.pop(0), refs.pop(0), refs.pop(0), refs.pop(0)
    sprev_ref, shprev_ref = (refs.pop(0), refs.pop(0)) if l > 0 else (None, None)
    y_ref, sout_ref, shout_ref = refs.pop(0), refs.pop(0), refs.pop(0)
    s_scr, prev_scr, at_s, rt_s, bi_s, ki_s, bd_s, kd_s, v_s, lpl_s, yr_s, bon_s, gate_s = refs
    sb, tt, _ = p_ref.shape
    rows = sb * tt
    nseg = chunk // seg
    tstep = pl.program_id(1)

    @pl.when(tstep == 0)
    def _():
        if zero_init:
            s_scr[...] = jnp.zeros_like(s_scr)
            prev_scr[...] = jnp.zeros_like(prev_scr)
        else:
            s_scr[...] = s0_ref[...]
            prev_scr[:, :, 0:RW_COLS] = sh0_ref[...]
            prev_scr[:, :, RW_COLS:RW_PAD] = jnp.zeros((sb, 1, RW_PAD - RW_COLS), F32)

    vec = vec_ref[...]
    w0, a0, k_k, k_a, r_k, ln_g, ln_b = [vec[i:i + 1, :] for i in range(7)]
    ones_bd = ones_ref[...]

    rw3 = p_ref[...]
    rw = rw3.reshape(rows, RW_PAD)
    prev_rows = jnp.broadcast_to(prev_scr[...], (sb, tt, RW_PAD)).reshape(rows, RW_PAD)
    prev = jnp.where(_local_row(rw.shape, tt) == 0, prev_rows, pltpu.roll(rw, 1, axis=0))
    prev_scr[...] = rw3[:, tt - 1:tt, :]
    rwm = rw + (prev - rw) * mu_ref[...]
    r, k, v = rwm[:, 0:W], rwm[:, W:2 * W], rwm[:, 2 * W:3 * W]
    lr = rwm[:, 3 * W:3 * W + 256]
    lane = lax.broadcasted_iota(jnp.int32, lr.shape, 1)
    act = jnp.where(lane < 64, jnp.tanh(lr), jnp.where(lane < 128, lr, _sigmoid(lr)))
    up = jnp.dot(_bf(act), wup_ref[...], preferred_element_type=F32)
    log_w = -RW_DECAY * _sigmoid(w0 + up[:, 0:W])
    a = _sigmoid(a0 + up[:, W:2 * W])
    gate_s[...] = up[:, 2 * W:3 * W]
    kk = k * k_k
    kk = kk / jnp.maximum(jnp.sqrt(_headsum(kk * kk, ones_bd)), 1e-12)
    k2 = k * (1.0 + (a - 1.0) * k_a)
    kka = kk * a
    bon_s[...] = _headsum(r * k2 * r_k, ones_bd) * v
    v_s[...] = v
    logp = _cumsum_rows(log_w, seg)
    lpl = _seg_last(logp, seg)
    lpl_s[...] = lpl
    at_s[...] = -kk * jnp.exp(logp - log_w)
    rt_s[...] = r * jnp.exp(logp)
    einv = jnp.exp(-logp)
    bi_s[...] = kka * einv
    ki_s[...] = k2 * einv
    elast = jnp.exp(lpl - logp)
    bd_s[...] = kka * elast
    kd_s[...] = k2 * elast

    def body(ch, carry):
        r0 = pl.multiple_of(ch * chunk, chunk)
        rs = pl.ds(r0, chunk)
        if nseg == 1:
            states = [s_scr[0, h] for h in range(RW_H)]
            pc = jnp.exp(lpl_s[pl.ds(r0, 1), :])
        else:
            s0 = pl.multiple_of(ch * nseg, nseg)
            states = [s_scr[pl.ds(s0, nseg), h] for h in range(RW_H)]
            pc = jnp.exp(lpl_s[rs, :].reshape(nseg, seg, W)[:, 0:1, :])
        ys, states = _rwkv_recur(at_s[rs, :], rt_s[rs, :], bi_s[rs, :], ki_s[rs, :], bd_s[rs, :], kd_s[rs, :],
                                 v_s[rs, :], states, pc, seg)
        for h in range(RW_H):
            if nseg == 1:
                s_scr[0, h] = states[h]
            else:
                s_scr[pl.ds(s0, nseg), h] = states[h]
        yr_s[rs, :] = jnp.concatenate(ys, axis=1)
        return carry
    lax.fori_loop(0, rows // chunk, body, 0)

    y = yr_s[...]
    mean = _headsum(y, ones_bd) * (1.0 / RW_D)
    yc = y - mean
    var = _headsum(yc * yc, ones_bd) * (1.0 / RW_D)
    yn = yc * lax.rsqrt(var + RW_GN_EPS) * ln_g + ln_b
    y_ref[...] = ((yn + bon_s[...]) * gate_s[...]).reshape(sb, tt, W)

    @pl.when(tstep == pl.num_programs(1) - 1)
    def _():
        _emit_state(l, sout_ref, sprev_ref, s_scr[...])
        _emit_state(l, shout_ref, shprev_ref, prev_scr[:, :, 0:RW_COLS])


def _rwkv_call(p, s_in, sh_in, prev, mu, vec, wup, ones_bd, l, rows, chunk, seg):
    bsz, t, _ = p.shape
    sb, tt = _tile(bsz, t, rows)
    assert seg < chunk or sb == 1
    zero_init = s_in is None
    s_specs = _state_specs(l, sb, (RW_H, RW_D, RW_D))
    sh_specs = _state_specs(l, sb, (1, RW_COLS))
    args = ([p] + ([] if zero_init else [s_in, sh_in]) + [mu, vec, wup, ones_bd]
            + (list(prev) if l > 0 else []))
    specs = ([pl.BlockSpec((sb, tt, RW_PAD), lambda i, j: (i, j, 3))]
             + ([] if zero_init else [s_specs[0], sh_specs[0]])
             + [pl.BlockSpec((1, RW_PAD), lambda i, j: (0, 0)),
                pl.BlockSpec((8, W), lambda i, j: (0, 0)),
                pl.BlockSpec((256, 3 * W), lambda i, j: (0, 0)),
                pl.BlockSpec((W, W), lambda i, j: (0, 0))]
             + ([s_specs[1], sh_specs[1]] if l > 0 else []))
    tile_scr = [pltpu.VMEM((sb * tt, W), F32) for _ in range(11)]
    return pl.pallas_call(
        functools.partial(_rwkv_kernel, (l, chunk, seg, zero_init)),
        out_shape=(jax.ShapeDtypeStruct((bsz, t, W), F32),
                   jax.ShapeDtypeStruct((l + 1, bsz, RW_H, RW_D, RW_D), F32),
                   jax.ShapeDtypeStruct((l + 1, bsz, 1, RW_COLS), F32)),
        grid=(bsz // sb, t // tt),
        in_specs=specs,
        out_specs=(pl.BlockSpec((sb, tt, W), lambda i, j: (i, j, 0)), s_specs[2], sh_specs[2]),
        scratch_shapes=[pltpu.VMEM((sb, RW_H, RW_D, RW_D), F32), pltpu.VMEM((sb, 1, RW_PAD), F32)] + tile_scr,
        compiler_params=_cparams(("arbitrary", "arbitrary")),
        name="rwkv7",
    )(*args)


CF_PAD = 32
CF_RB = 64
SUBLANES = 8


def _conf_kernel(cfg, *refs):
    l, zero_init = cfg
    refs = list(refs)
    p_ref = refs.pop(0)
    s0_ref = None if zero_init else refs.pop(0)
    dw_ref, vec_ref = refs.pop(0), refs.pop(0)
    prev_ref = refs.pop(0) if l > 0 else None
    y_ref, sout_ref, ext, shifted = refs
    sb, tt, _ = p_ref.shape
    tstep = pl.program_id(1)
    off = CF_PAD - (CF_K - 1)
    span = tt + CF_PAD - SUBLANES

    @pl.when(tstep == 0)
    def _():
        if zero_init:
            ext[:, 0:CF_PAD, :] = jnp.zeros((sb, CF_PAD, W), F32)
        else:
            ext[:, off:CF_PAD, :] = s0_ref[...]

    vec = vec_ref[...]
    bias, ln_g, ln_b = vec[0:1], vec[1:2], vec[2:3]
    dw = dw_ref[...]
    rb = min(CF_RB, tt)

    def body(s, carry):
        blk = p_ref[s]
        ext[s, CF_PAD:CF_PAD + tt, :] = blk[:, 0:W] * _sigmoid(blk[:, W:2 * W])
        for q in range(1, SUBLANES):
            shifted[q, 0:span, :] = ext[s, q:q + span, :]
        for r in range(tt // rb):
            acc = jnp.zeros((rb, W), F32) + bias
            for j in range(CF_K):
                a8, q = divmod(off + j, SUBLANES)
                lo = r * rb + a8 * SUBLANES
                tap = ext[s, lo:lo + rb, :] if q == 0 else shifted[q, lo:lo + rb, :]
                acc = acc + dw[j:j + 1, :] * tap
            mean = jnp.mean(acc, axis=-1, keepdims=True)
            xc = acc - mean
            var = jnp.mean(xc * xc, axis=-1, keepdims=True)
            yn = xc * lax.rsqrt(var + 1e-5) * ln_g + ln_b
            y_ref[s, r * rb:(r + 1) * rb, :] = _silu(yn)
        tail = ext[s, tt:tt + CF_PAD, :]
        ext[s, 0:CF_PAD, :] = tail
        return carry
    lax.fori_loop(0, sb, body, 0)

    @pl.when(tstep == pl.num_programs(1) - 1)
    def _():
        _emit_state(l, sout_ref, prev_ref, ext[:, off:CF_PAD, :])


def _conf_call(p, s_in, prev, dw, vec, l, rows):
    bsz, t, _ = p.shape
    sb, tt = _tile(bsz, t, rows)
    zero_init = s_in is None
    in_spec, prev_spec, out_spec = _state_specs(l, sb, (CF_K - 1, W))
    args = [p] + ([] if zero_init else [s_in]) + [dw, vec] + ([prev] if l > 0 else [])
    specs = ([pl.BlockSpec((sb, tt, 2 * W), lambda i, j: (i, j, 8))] + ([] if zero_init else [in_spec])
             + [pl.BlockSpec((32, W), lambda i, j: (0, 0)), pl.BlockSpec((8, W), lambda i, j: (0, 0))]
             + ([prev_spec] if l > 0 else []))
    return pl.pallas_call(
        functools.partial(_conf_kernel, (l, zero_init)),
        out_shape=(jax.ShapeDtypeStruct((bsz, t, W), F32),
                   jax.ShapeDtypeStruct((l + 1, bsz, CF_K - 1, W), F32)),
        grid=(bsz // sb, t // tt),
        in_specs=specs,
        out_specs=(pl.BlockSpec((sb, tt, W), lambda i, j: (i, j, 0)), out_spec),
        scratch_shapes=[pltpu.VMEM((sb, CF_PAD + tt, W), F32),
                        pltpu.VMEM((SUBLANES, CF_PAD + tt, W), F32)],
        compiler_params=_cparams(("arbitrary", "arbitrary")),
        name="conformer",
    )(*args)


LRU_PAD = 8


def _gelu_tanh(x):
    return 0.5 * x * (1.0 + jnp.tanh(0.7978845608028654 * (x + 0.044715 * (x * x * x))))


def _lru_kernel(cfg, *refs):
    l, zero_init = cfg
    refs = list(refs)
    p_ref = refs.pop(0)
    h0_ref, c0_ref = (None, None) if zero_init else (refs.pop(0), refs.pop(0))
    cw_ref, vec_ref, wax_ref = refs.pop(0), refs.pop(0), refs.pop(0)
    hprev_ref, cprev_ref = (refs.pop(0), refs.pop(0)) if l > 0 else (None, None)
    y_ref, hout_ref, cout_ref, ext, hcar = refs
    sb, tt, _ = p_ref.shape
    tstep = pl.program_id(1)
    off = LRU_PAD - (LRU_K - 1)

    @pl.when(tstep == 0)
    def _():
        if zero_init:
            ext[:, 0:LRU_PAD, :] = jnp.zeros((sb, LRU_PAD, W), F32)
            hcar[...] = jnp.zeros_like(hcar)
        else:
            ext[:, off:LRU_PAD, :] = c0_ref[...]
            hcar[...] = h0_ref[...]

    vec = vec_ref[...]
    cb, ba, bx, lam = vec[0:1], vec[1:2], vec[2:3], vec[3:4]
    sp = jnp.maximum(-lam, 0.0) + jnp.log1p(jnp.exp(-jnp.abs(lam)))
    cw = cw_ref[...]
    wax = wax_ref[...]

    def body(s, carry):
        blk = p_ref[s]
        ext[s, LRU_PAD:LRU_PAD + tt, :] = blk[:, 0:W]
        xc = jnp.zeros((tt, W), F32) + cb
        for j in range(LRU_K):
            xc = xc + cw[j:j + 1, :] * ext[s, pl.ds(off + j, tt), :]
        pre = jnp.dot(_bf(xc), wax, preferred_element_type=F32)
        rg = _sigmoid(pre[:, 0:W] + ba)
        ig = _sigmoid(pre[:, W:2 * W] + bx)
        log_a = -LRU_C * rg * sp
        a = jnp.exp(log_a)
        bv = jnp.sqrt(1.0 - a * a) * (ig * xc)
        d = 1
        while d < tt:
            a_s = _shift_rows(a, d, 1.0, tt)
            b_s = _shift_rows(bv, d, 0.0, tt)
            bv = a * b_s + bv
            a = a * a_s
            d *= 2
        hs = a * hcar[s] + bv
        hcar[s] = hs[tt - 1:tt, :]
        y_ref[s] = hs * _gelu_tanh(blk[:, W:2 * W])
        tail = ext[s, tt:tt + LRU_PAD, :]
        ext[s, 0:LRU_PAD, :] = tail
        return carry
    lax.fori_loop(0, sb, body, 0)

    @pl.when(tstep == pl.num_programs(1) - 1)
    def _():
        _emit_state(l, hout_ref, hprev_ref, hcar[...])
        _emit_state(l, cout_ref, cprev_ref, ext[:, off:LRU_PAD, :])


def _lru_call(p, h_in, c_in, prev, cw, vec, wax, l, rows):
    bsz, t, _ = p.shape
    sb, tt = _tile(bsz, t, rows)
    zero_init = h_in is None
    h_specs = _state_specs(l, sb, (1, W))
    c_specs = _state_specs(l, sb, (LRU_K - 1, W))
    args = [p] + ([] if zero_init else [h_in, c_in]) + [cw, vec, wax] + (list(prev) if l > 0 else [])
    specs = ([pl.BlockSpec((sb, tt, 2 * W), lambda i, j: (i, j, 9))]
             + ([] if zero_init else [h_specs[0], c_specs[0]])
             + [pl.BlockSpec((8, W), lambda i, j: (0, 0)), pl.BlockSpec((8, W), lambda i, j: (0, 0)),
                pl.BlockSpec((W, 2 * W), lambda i, j: (0, 0))]
             + ([h_specs[1], c_specs[1]] if l > 0 else []))
    return pl.pallas_call(
        functools.partial(_lru_kernel, (l, zero_init)),
        out_shape=(jax.ShapeDtypeStruct((bsz, t, W), F32),
                   jax.ShapeDtypeStruct((l + 1, bsz, 1, W), F32),
                   jax.ShapeDtypeStruct((l + 1, bsz, LRU_K - 1, W), F32)),
        grid=(bsz // sb, t // tt),
        in_specs=specs,
        out_specs=(pl.BlockSpec((sb, tt, W), lambda i, j: (i, j, 0)), h_specs[2], c_specs[2]),
        scratch_shapes=[pltpu.VMEM((sb, LRU_PAD + tt, W), F32), pltpu.VMEM((sb, 1, W), F32)],
        compiler_params=_cparams(("arbitrary", "arbitrary")),
        name="rglru",
    )(*args)


def _merge_kernel(gl_ref, yhg_ref, yrw_ref, ycf_ref, ylr_ref, x_ref, g1_ref, sc2_ref, sh2_ref, ng_ref,
                  wb_ref, wo_ref, x1_ref, h2_ref):
    sb, tt, _ = x_ref.shape
    tm = sb * tt
    z = jnp.zeros((tm, D), F32)
    for b, y_ref in enumerate((yhg_ref, yrw_ref, ycf_ref, ylr_ref)):
        bo = jnp.dot(_bf(y_ref[...].reshape(tm, W)), wb_ref[b], preferred_element_type=F32)
        gate = _sigmoid(gl_ref[:, :, b * D:(b + 1) * D].reshape(tm, D))
        z = z + gate * bo
    out = jnp.dot(_bf(z), wo_ref[...], preferred_element_type=F32).reshape(sb, tt, D)
    x1 = x_ref[...] + g1_ref[...] * out
    x1_ref[...] = x1
    h2_ref[...] = _bf(_adaln(x1, ng_ref[...], sc2_ref[...], sh2_ref[...]))


def _merge_call(p, ys, x, g1, sc2, sh2, ng, wb, wo, rows=512):
    bsz, t, _ = x.shape
    sb, tt = _tile(bsz, t, rows)
    tok = lambda width: pl.BlockSpec((sb, tt, width), lambda i, j: (i, j, 0))
    seq = pl.BlockSpec((sb, 1, D), lambda i, j: (i, 0, 0))
    return pl.pallas_call(
        _merge_kernel,
        out_shape=(jax.ShapeDtypeStruct((bsz, t, D), F32),
                   jax.ShapeDtypeStruct((bsz, t, D), BF16)),
        grid=(bsz // sb, t // tt),
        in_specs=[tok(4 * D), tok(W), tok(W), tok(W), tok(W), tok(D), seq, seq, seq,
                  pl.BlockSpec((1, D), lambda i, j: (0, 0)),
                  pl.BlockSpec((4, W, D), lambda i, j: (0, 0, 0)),
                  pl.BlockSpec((D, D), lambda i, j: (0, 0))],
        out_specs=(tok(D), tok(D)),
        compiler_params=_cparams(("arbitrary", "arbitrary")),
        name="merge",
    )(p, *ys, x, g1, sc2, sh2, ng, wb, wo)


def _mlp_kernel(final, h2_ref, x1_ref, g2_ref, fg_ref, w1_ref, w2_ref, o_ref, acc):
    sb, tt, _ = x1_ref.shape
    k = pl.program_id(2)

    @pl.when(k == 0)
    def _():
        acc[...] = jnp.zeros_like(acc)

    hid = jnp.dot(h2_ref[...].reshape(sb * tt, D), w1_ref[...], preferred_element_type=F32)
    act = jnp.square(jnp.maximum(hid, 0.0))
    acc[...] += jnp.dot(_bf(act), w2_ref[...], preferred_element_type=F32)

    @pl.when(k == pl.num_programs(2) - 1)
    def _():
        x2 = x1_ref[...] + g2_ref[...] * acc[...].reshape(sb, tt, D)
        if final:
            ms = jnp.mean(x2 * x2, axis=-1, keepdims=True)
            x2 = x2 * lax.rsqrt(ms + EPS) * fg_ref[...]
        o_ref[...] = x2


def _mlp_call(h2, x1, g2, fg, w1, w2, final, rows=1024, th=1024):
    bsz, t, _ = x1.shape
    sb, tt = _tile(bsz, t, rows)
    tok = pl.BlockSpec((sb, tt, D), lambda i, j, k: (i, j, 0))
    return pl.pallas_call(
        functools.partial(_mlp_kernel, final),
        out_shape=jax.ShapeDtypeStruct((bsz, t, D), F32),
        grid=(bsz // sb, t // tt, HID // th),
        in_specs=[tok, tok,
                  pl.BlockSpec((sb, 1, D), lambda i, j, k: (i, 0, 0)),
                  pl.BlockSpec((1, D), lambda i, j, k: (0, 0)),
                  pl.BlockSpec((D, th), lambda i, j, k: (0, k)),
                  pl.BlockSpec((th, D), lambda i, j, k: (k, 0))],
        out_specs=tok,
        scratch_shapes=[pltpu.VMEM((sb * tt, D), F32)],
        compiler_params=_cparams(("arbitrary", "arbitrary", "arbitrary")),
        name="mlp",
    )(h2, x1, g2, fg, w1, w2)


def _block_diag(w):
    n, c, d = w.shape
    eye = jnp.eye(n, dtype=w.dtype)
    return (eye[:, None, :, None] * w[:, :, None, :]).reshape(n * c, n * d)


def _prep_layer(wt, l):
    w_in = wt["w_in"][l]
    zpad = jnp.zeros((D, RW_PAD - RW_COLS), F32)
    wcat = jnp.concatenate([wt["w_gate"][l], w_in[:, 0:2048], w_in[:, 2048:2048 + RW_COLS], zpad,
                            w_in[:, 2048 + RW_COLS:]], axis=1)
    bcat = jnp.concatenate([wt["b_gate"][l], jnp.zeros((P_COLS - 4 * D,), F32)])[None, :]
    wup = jnp.zeros((256, 3 * W), F32)
    wup = wup.at[0:64, 0:W].set(wt["rw_w_up"][l])
    wup = wup.at[64:128, W:2 * W].set(wt["rw_a_up"][l])
    wup = wup.at[128:256, 2 * W:3 * W].set(wt["rw_g_up"][l])
    zrow = jnp.zeros((W,), F32)
    rw_vec = jnp.stack([wt["rw_w0"][l], wt["rw_a0"][l], wt["rw_k_k"][l], wt["rw_k_a"][l], wt["rw_r_k"][l],
                        wt["rw_ln_g"][l], wt["rw_ln_b"][l], zrow])
    mu = jnp.concatenate([wt["rw_mu"][l], jnp.zeros((RW_PAD - RW_COLS,), F32)])[None, :]
    cf_vec = jnp.stack([wt["cf_dw_b"][l], wt["cf_ln_g"][l], wt["cf_ln_b"][l]] + [zrow] * 5)
    cf_dw = jnp.concatenate([wt["cf_dw"][l], jnp.zeros((1, W), F32)], axis=0)
    lru_vec = jnp.stack([wt["lru_conv_b"][l], wt["lru_ba"][l], wt["lru_bx"][l], wt["lru_lambda"][l]] + [zrow] * 4)
    lru_cw = jnp.concatenate([wt["lru_conv_w"][l], jnp.zeros((8 - LRU_K, W), F32)], axis=0)
    wax = jnp.concatenate([_block_diag(wt["lru_wa"][l]), _block_diag(wt["lru_wx"][l])], axis=1)
    return dict(
        wcat=_bf(wcat), bcat=bcat, wup=_bf(wup), rw_vec=rw_vec, mu=mu, cf_vec=cf_vec, cf_dw=cf_dw,
        lru_vec=lru_vec, lru_cw=lru_cw, wax=_bf(wax),
        wb=_bf(wt["w_branch"][l]), wo=_bf(wt["w_out"][l]), w1=_bf(wt["w_mlp1"][l]), w2=_bf(wt["w_mlp2"][l]),
        hg_gn=wt["hg_norm_g"][l][None, :], nmix=wt["norm_mix_g"][l][None, :], nmlp=wt["norm_mlp_g"][l][None, :],
    )


def _trunk(x, c, states, wt, layers, ones_bd, mix_rows, rec_rows, chunk, seg):
    nl = len(layers)
    if states is None:
        s_hg = s_rw = s_shift = s_cf = s_lh = s_lc = None
    else:
        s_hg, s_rw, s_shift, s_cf, s_lh, s_lc = states
        s_shift = s_shift[:, :, None, :]
        s_lh = s_lh[:, :, None, :]
    mod = _mod_call(c, wt["ada_w"], wt["ada_b"])
    fg = wt["norm_final_g"][None, :]
    n_hg = n_rw = n_sh = n_cf = n_lh = n_lc = None
    for l, lw in enumerate(layers):
        sh1, sc1, g1, sh2, sc2, g2 = [mod[l, i][:, None, :] for i in range(6)]
        p = _inproj_call(x, sc1, sh1, lw["nmix"], lw["wcat"], lw["bcat"])
        y_hg, n_hg = _hgrn_call(p, s_hg, n_hg, wt["hg_lower"], lw["hg_gn"], l, rec_rows, chunk, seg)
        y_rw, n_rw, n_sh = _rwkv_call(p, s_rw, s_shift, (n_rw, n_sh), lw["mu"], lw["rw_vec"], lw["wup"], ones_bd,
                                      l, rec_rows, chunk, seg)
        y_cf, n_cf = _conf_call(p, s_cf, n_cf, lw["cf_dw"], lw["cf_vec"], l, mix_rows)
        y_lr, n_lh, n_lc = _lru_call(p, s_lh, s_lc, (n_lh, n_lc), lw["lru_cw"], lw["lru_vec"], lw["wax"], l,
                                     mix_rows)
        x1, h2 = _merge_call(p, (y_hg, y_rw, y_cf, y_lr), x, g1, sc2, sh2, lw["nmlp"], lw["wb"], lw["wo"])
        x = _mlp_call(h2, x1, g2, fg, lw["w1"], lw["w2"], final=(l == nl - 1))
    return x, [n_hg, n_rw, n_sh[:, :, 0, :], n_cf, n_lh[:, :, 0, :], n_lc]


def _run(x_prompt, x_sample, sample_states, c_prompt, c_sample, wt):
    nl = wt["w_in"].shape[0]
    layers = [_prep_layer(wt, l) for l in range(nl)]
    head = jnp.arange(W, dtype=jnp.int32) // RW_D
    ones_bd = _bf((head[:, None] == head[None, :]).astype(F32))
    tp = x_prompt.shape[1]
    ts = x_sample.shape[1]
    rows_p = min(256, tp)
    chunk_p = min(64, tp)
    y_p, st_p = _trunk(x_prompt, c_prompt, None, wt, layers, ones_bd, rows_p, rows_p, chunk_p, chunk_p)
    y_s, st_s = _trunk(x_sample, c_sample, sample_states, wt, layers, ones_bd, 8 * ts, 8 * ts, 8 * ts, ts)
    return (y_p, y_s, *st_p, *st_s)


def kernel(x_prompt, x_sample, state_hgrn, state_rwkv, state_rwkv_shift, state_conv, state_lru_h, state_lru_conv, c_prompt, c_sample, ada_w, ada_b, norm_mix_g, norm_mlp_g, norm_final_g, w_in, hg_lower, hg_norm_g, rw_mu, rw_w0, rw_w_up, rw_a0, rw_a_up, rw_g_up, rw_k_k, rw_k_a, rw_r_k, rw_ln_g, rw_ln_b, cf_dw, cf_dw_b, cf_ln_g, cf_ln_b, lru_conv_w, lru_conv_b, lru_wa, lru_ba, lru_wx, lru_bx, lru_lambda, w_branch, w_gate, b_gate, w_out, w_mlp1, w_mlp2):
    wt = dict(ada_w=ada_w, ada_b=ada_b, norm_mix_g=norm_mix_g, norm_mlp_g=norm_mlp_g,
              norm_final_g=norm_final_g, w_in=w_in, hg_lower=hg_lower, hg_norm_g=hg_norm_g, rw_mu=rw_mu,
              rw_w0=rw_w0, rw_w_up=rw_w_up, rw_a0=rw_a0, rw_a_up=rw_a_up, rw_g_up=rw_g_up, rw_k_k=rw_k_k,
              rw_k_a=rw_k_a, rw_r_k=rw_r_k, rw_ln_g=rw_ln_g, rw_ln_b=rw_ln_b, cf_dw=cf_dw, cf_dw_b=cf_dw_b,
              cf_ln_g=cf_ln_g, cf_ln_b=cf_ln_b, lru_conv_w=lru_conv_w, lru_conv_b=lru_conv_b, lru_wa=lru_wa,
              lru_ba=lru_ba, lru_wx=lru_wx, lru_bx=lru_bx, lru_lambda=lru_lambda, w_branch=w_branch,
              w_gate=w_gate, b_gate=b_gate, w_out=w_out, w_mlp1=w_mlp1, w_mlp2=w_mlp2)
    sample_states = (state_hgrn, state_rwkv, state_rwkv_shift, state_conv, state_lru_h, state_lru_conv)
    return _run(x_prompt, x_sample, sample_states, c_prompt, c_sample, wt)
```

```python
import functools
import math

import jax
import jax.numpy as jnp
from jax import lax
from jax.experimental import pallas as pl
from jax.experimental.pallas import tpu as pltpu

D = 1024
W = 512
HG_H = 4
HG_D = 128
RW_H = 8
RW_D = 64
RW_COLS = 1792
RW_PAD = 2048
CF_K = 31
LRU_K = 4
HID = 4096
EPS = 1e-6
RW_GN_EPS = 64e-5
RW_DECAY = 0.606531
LRU_C = 8.0
LANES = 128

P_COLS = 6144
SEQ_UNROLL = 2
VMEM_LIMIT = 56 * 1024 * 1024

F32 = jnp.float32
BF16 = jnp.bfloat16


def _bf(x):
    return x.astype(BF16)


def _mm(a, b):
    return jnp.dot(_bf(a), _bf(b), preferred_element_type=F32)


def _mm_nt(a, b):
    return lax.dot_general(_bf(a), _bf(b), (((1,), (1,)), ((), ())), preferred_element_type=F32)


def _mm_tn(a, b):
    return lax.dot_general(_bf(a), _bf(b), (((0,), (0,)), ((), ())), preferred_element_type=F32)


def _sigmoid(x):
    return 1.0 / (1.0 + jnp.exp(-x))


def _silu(x):
    return x * _sigmoid(x)


def _local_row(shape, seg):
    return jnp.bitwise_and(lax.broadcasted_iota(jnp.int32, shape, 0), seg - 1)


def _shift_rows(x, d, fill, seg):
    return jnp.where(_local_row(x.shape, seg) >= d, pltpu.roll(x, d, axis=0), fill)


def _scan_mats(chunk, seg):
    r = jnp.arange(chunk, dtype=jnp.int32)
    same = (r[:, None] // seg) == (r[None, :] // seg)
    tri = same & (r[None, :] <= r[:, None])
    return _bf(jnp.concatenate([tri, same], axis=0).astype(F32))


def _cumsum_last(x, te):
    c = x.shape[0]
    p1 = _bf(x)
    r1 = x - p1.astype(F32)
    p2 = _bf(r1)
    p3 = _bf(r1 - p2.astype(F32))
    out = (jnp.dot(te, p1, preferred_element_type=F32) + jnp.dot(te, p2, preferred_element_type=F32)
           + jnp.dot(te, p3, preferred_element_type=F32))
    return out[:c], out[c:]


def _seg_id(shape, axis, log2_seg):
    return lax.shift_right_logical(lax.broadcasted_iota(jnp.int32, shape, axis), log2_seg)


def _headsum(x, ones_bd):
    hi = _bf(x)
    lo = _bf(x - hi.astype(F32))
    return (jnp.dot(hi, ones_bd, preferred_element_type=F32)
            + jnp.dot(lo, ones_bd, preferred_element_type=F32))


def _cparams(sem):
    return pltpu.CompilerParams(dimension_semantics=sem, vmem_limit_bytes=VMEM_LIMIT)


def _tile(bsz, t, rows):
    if t >= rows:
        return 1, rows
    return min(bsz, rows // t), t


def _state_specs(l, sb, tail):
    zeros = (0,) * len(tail)
    in_spec = pl.BlockSpec((None, sb) + tail, lambda i, j: (l, i) + zeros)
    prev_spec = pl.BlockSpec((l, sb) + tail, lambda i, j: (0, i) + zeros)
    out_spec = pl.BlockSpec((l + 1, sb) + tail, lambda i, j: (0, i) + zeros)
    return in_spec, prev_spec, out_spec


def _emit_state(l, out_ref, prev_ref, new):
    for q in range(l):
        out_ref[q] = prev_ref[q]
    out_ref[l] = new


def _mod_kernel(c_ref, w_ref, b_ref, o_ref):
    c = c_ref[...]
    o_ref[0, 0] = _mm(_silu(c), w_ref[0]) + b_ref[0, 0]


def _mod_call(c, ada_w, ada_b):
    nl = ada_w.shape[0]
    bsz = c.shape[0]
    return pl.pallas_call(
        _mod_kernel,
        out_shape=jax.ShapeDtypeStruct((nl, 6, bsz, D), F32),
        grid=(nl, 6),
        in_specs=[
            pl.BlockSpec((bsz, D), lambda l, j: (0, 0)),
            pl.BlockSpec((1, D, D), lambda l, j: (l, 0, j)),
            pl.BlockSpec((1, 1, 1, D), lambda l, j: (l, j, 0, 0)),
        ],
        out_specs=pl.BlockSpec((1, 1, bsz, D), lambda l, j: (l, j, 0, 0)),
        compiler_params=_cparams(("arbitrary", "arbitrary")),
        name="adaln_mod",
    )(c, ada_w, ada_b.reshape(nl, 6, 1, D))


def _adaln(x, g, sc, sh):
    ms = jnp.mean(x * x, axis=-1, keepdims=True)
    y = x * lax.rsqrt(ms + EPS) * g
    return y * (1.0 + sc) + sh


def _inproj_kernel(x_ref, sc_ref, sh_ref, g_ref, w_ref, o_ref, h_scr):
    @pl.when(pl.program_id(2) == 0)
    def _():
        h = _adaln(x_ref[...], g_ref[...], sc_ref[...], sh_ref[...])
        h_scr[...] = _bf(h.reshape(h_scr.shape))

    acc = jnp.dot(h_scr[...], w_ref[...], preferred_element_type=F32)
    o_ref[...] = acc.reshape(o_ref.shape)


def _inproj_call(x, sc, sh, g, wcat, rows=1024, tn=1024):
    bsz, t, _ = x.shape
    sb, tt = _tile(bsz, t, rows)
    grid = (bsz // sb, t // tt, P_COLS // tn)
    return pl.pallas_call(
        _inproj_kernel,
        out_shape=jax.ShapeDtypeStruct((bsz, t, P_COLS), F32),
        grid=grid,
        in_specs=[
            pl.BlockSpec((sb, tt, D), lambda i, j, n: (i, j, 0)),
            pl.BlockSpec((sb, 1, D), lambda i, j, n: (i, 0, 0)),
            pl.BlockSpec((sb, 1, D), lambda i, j, n: (i, 0, 0)),
            pl.BlockSpec((1, D), lambda i, j, n: (0, 0)),
            pl.BlockSpec((D, tn), lambda i, j, n: (0, n)),
        ],
        out_specs=pl.BlockSpec((sb, tt, tn), lambda i, j, n: (i, j, n)),
        scratch_shapes=[pltpu.VMEM((sb * tt, D), BF16)],
        compiler_params=_cparams(("arbitrary", "arbitrary", "arbitrary")),
        name="inproj",
    )(x, sc, sh, g, wcat)


HG_SUB = 16


def _hgrn_chunk(qr, fz, iv, og, lb, gn, te, states, seg):
    c, width = qr.shape
    nseg = c // seg
    heads = range(width // HG_D)
    q = _silu(qr)
    f = lb + (1.0 - lb) * _sigmoid(fz)
    logf = jnp.log(f)
    kf = (1.0 - lb) * _sigmoid(-fz)
    b, b_last = _cumsum_last(logf, te)
    qe = q * jnp.exp(b)
    kdec = kf * jnp.exp(b_last - b)
    e_last = jnp.exp(b_last)
    sl = [slice(h * HG_D, (h + 1) * HG_D) for h in heads]

    if nseg == 1:
        sub = min(HG_SUB, c)
        nsub = c // sub
        o_inter = [_mm_nt(qe[:, sl[h]], states[h]) for h in heads]
        new_states = [states[h] * e_last[0:1, sl[h]] + _mm_tn(iv[:, sl[h]], kdec[:, sl[h]]) for h in heads]
        pieces = [[] for _ in heads]
        for i in range(nsub):
            r0 = i * sub
            m = b[r0 - 1:r0, :] if i > 0 else jnp.zeros((1, width), F32)
            qs = q[r0:r0 + sub] * jnp.exp(b[r0:r0 + sub] - m)
            kd = kf[r0:r0 + sub] * jnp.exp(jnp.minimum(m - b[r0:r0 + sub], 80.0))
            if i > 0:
                kall = jnp.concatenate([kf[:r0] * jnp.exp(m - b[:r0]), kd], axis=0)
            else:
                kall = kd
            row = lax.broadcasted_iota(jnp.int32, (sub, r0 + sub), 0) + r0
            col = lax.broadcasted_iota(jnp.int32, (sub, r0 + sub), 1)
            causal = col <= row
            scs = [jnp.where(causal, _mm_nt(qs[:, sl[h]], kall[:, sl[h]]), 0.0) for h in heads]
            for h in heads:
                pieces[h].append(_mm(scs[h], iv[:r0 + sub, sl[h]]))
        o_intra = [jnp.concatenate(pieces[h], axis=0) if nsub > 1 else pieces[h][0] for h in heads]
    else:
        lg = int(math.log2(seg))
        kd = kf * jnp.exp(jnp.minimum(-b, 80.0))
        rr = lax.broadcasted_iota(jnp.int32, (c, c), 0)
        cc = lax.broadcasted_iota(jnp.int32, (c, c), 1)
        causal = (lax.shift_right_logical(rr, lg) == lax.shift_right_logical(cc, lg)) & (cc <= rr)
        rowseg = _seg_id((c, HG_D), 0, lg)
        blockmask = _seg_id((c, nseg * HG_D), 0, lg) == _seg_id((c, nseg * HG_D), 1, 7)
        e3 = e_last.reshape(nseg, seg, width)[:, 0:1, :]
        full = [_mm_nt(qe[:, sl[h]], states[h].reshape(nseg * HG_D, HG_D)) for h in heads]
        o_inter = []
        for h in heads:
            o = full[h][:, 0:HG_D]
            for s in range(1, nseg):
                o = jnp.where(rowseg == s, full[h][:, s * HG_D:(s + 1) * HG_D], o)
            o_inter.append(o)
        scs = [jnp.where(causal, _mm_nt(qe[:, sl[h]], kd[:, sl[h]]), 0.0) for h in heads]
        o_intra = [_mm(scs[h], iv[:, sl[h]]) for h in heads]
        new_states = []
        for h in heads:
            ivexp = jnp.where(blockmask, jnp.concatenate([iv[:, sl[h]]] * nseg, axis=1), 0.0)
            upd = _mm_tn(ivexp, kdec[:, sl[h]]).reshape(nseg, HG_D, HG_D)
            new_states.append(states[h] * e3[:, :, sl[h]] + upd)
    outs = []
    for h in heads:
        o = o_inter[h] + o_intra[h]
        outs.append(o * lax.rsqrt(jnp.mean(o * o, axis=-1, keepdims=True) + EPS))
    y = jnp.concatenate(outs, axis=1) * gn * _silu(og)
    return y, new_states


def _hgrn_kernel(cfg, *refs):
    l, chunk, seg, zero_init = cfg
    refs = list(refs)
    p_ref = refs.pop(0)
    s0_ref = None if zero_init else refs.pop(0)
    lower_ref, gn_ref, te_ref = refs.pop(0), refs.pop(0), refs.pop(0)
    prev_ref = refs.pop(0) if l > 0 else None
    y_ref, sout_ref, s_scr = refs
    sb, tt, _ = p_ref.shape
    nseg = chunk // seg
    tstep = pl.program_id(1)

    @pl.when(tstep == 0)
    def _():
        if zero_init:
            s_scr[...] = jnp.zeros_like(s_scr)
        else:
            def init(i, carry):
                for h in range(HG_H):
                    s_scr[i, h] = s0_ref[i, h].T
                return carry
            lax.fori_loop(0, sb, init, 0)

    low = lower_ref[...]
    e = jnp.exp(low - jnp.max(low, axis=0, keepdims=True))
    sm = e / jnp.sum(e, axis=0, keepdims=True)
    lb = jnp.sum(sm[:l + 1], axis=0, keepdims=True) - sm[0:1]
    gn = gn_ref[...]
    te = te_ref[...]

    def run(blk, states):
        return _hgrn_chunk(blk[:, 0:W], blk[:, W:2 * W], blk[:, 2 * W:3 * W], blk[:, 3 * W:4 * W], lb, gn, te,
                           states, seg)

    def body_long(i, carry):
        rs = pl.ds(pl.multiple_of(i * chunk, chunk), chunk)
        cols = [jnp.concatenate([p_ref[q, rs, j * W:(j + 1) * W] for q in range(sb)], axis=1) for j in range(4)]
        wide = lambda v: jnp.concatenate([v] * sb, axis=1)
        y, states = _hgrn_chunk(*cols, wide(lb), wide(gn), te,
                                [s_scr[q, h] for q in range(sb) for h in range(HG_H)], seg)
        for q in range(sb):
            for h in range(HG_H):
                s_scr[q, h] = states[q * HG_H + h]
            y_ref[q, rs, :] = y[:, q * W:(q + 1) * W]
        return carry

    def body_short(i, carry):
        s0 = pl.multiple_of(i * nseg, nseg)
        y, states = run(p_ref[pl.ds(s0, nseg)].reshape(chunk, 4 * W),
                        [s_scr[pl.ds(s0, nseg), h] for h in range(HG_H)])
        for h in range(HG_H):
            s_scr[pl.ds(s0, nseg), h] = states[h]
        y_ref[pl.ds(s0, nseg)] = y.reshape(nseg, seg, W)
        return carry

    if nseg == 1:
        lax.fori_loop(0, tt // chunk, body_long, 0)
    else:
        lax.fori_loop(0, sb * tt // chunk, body_short, 0)

    @pl.when(tstep == pl.num_programs(1) - 1)
    def _():
        for q in range(l):
            sout_ref[q] = prev_ref[q]

        def fin(i, carry):
            for h in range(HG_H):
                sout_ref[l, i, h] = s_scr[i, h].T
            return carry
        lax.fori_loop(0, sb, fin, 0)


def _rec_tile(bsz, t, rows, nlong):
    sb, tt = _tile(bsz, t, rows)
    return (min(nlong, bsz), tt) if t >= rows else (sb, tt)


def _hgrn_call(p, s_in, prev, hg_lower, gn, te, l, rows, nlong, chunk, seg):
    bsz, t, _ = p.shape
    sb, tt = _rec_tile(bsz, t, rows, nlong)
    zero_init = s_in is None
    in_spec, prev_spec, out_spec = _state_specs(l, sb, (HG_H, HG_D, HG_D))
    args = [p] + ([] if zero_init else [s_in]) + [hg_lower, gn, te] + ([prev] if l > 0 else [])
    specs = ([pl.BlockSpec((sb, tt, 4 * W), lambda i, j: (i, j, 0))] + ([] if zero_init else [in_spec])
             + [pl.BlockSpec(hg_lower.shape, lambda i, j: (0, 0)), pl.BlockSpec((1, W), lambda i, j: (0, 0)),
                pl.BlockSpec(te.shape, lambda i, j: (0, 0))]
             + ([prev_spec] if l > 0 else []))
    return pl.pallas_call(
        functools.partial(_hgrn_kernel, (l, chunk, seg, zero_init)),
        out_shape=(jax.ShapeDtypeStruct((bsz, t, W), F32),
                   jax.ShapeDtypeStruct((l + 1, bsz, HG_H, HG_D, HG_D), F32)),
        grid=(bsz // sb, t // tt),
        in_specs=specs,
        out_specs=(pl.BlockSpec((sb, tt, W), lambda i, j: (i, j, 0)), out_spec),
        scratch_shapes=[pltpu.VMEM((sb, HG_H, HG_D, HG_D), F32)],
        compiler_params=_cparams(("arbitrary", "arbitrary")),
        name="hgrn2",
    )(*args)


def _pick64(full, rowseg):
    nb = full.shape[1] // LANES
    half = lax.shift_right_logical(rowseg, 1)
    sel = full[:, 0:LANES]
    for j in range(1, nb):
        sel = jnp.where(half == j, full[:, j * LANES:(j + 1) * LANES], sel)
    sel = jnp.where(jnp.bitwise_and(rowseg, 1) == 1, pltpu.roll(sel, RW_D, axis=1), sel)
    return sel[:, 0:RW_D]


def _rwkv_recur(at, rt, bi, ki, bd, kd, v, states, pc, seg):
    c = at.shape[0]
    nseg = c // seg
    lg = int(math.log2(seg))
    rr = lax.broadcasted_iota(jnp.int32, (c, c), 0)
    cc = lax.broadcasted_iota(jnp.int32, (c, c), 1)
    if nseg == 1:
        strict = rr > cc
        incl = rr >= cc
    else:
        same = lax.shift_right_logical(rr, lg) == lax.shift_right_logical(cc, lg)
        strict = same & (rr > cc)
        incl = same & (rr >= cc)
        r2 = jnp.bitwise_and(lax.broadcasted_iota(jnp.int32, (2 * c, LANES), 0), c - 1)
        rowseg = lax.shift_right_logical(r2, lg)
        rb = jnp.bitwise_and(lax.broadcasted_iota(jnp.int32, (2 * c, nseg * RW_D), 0), c - 1)
        blockmask = lax.shift_right_logical(rb, lg) == _seg_id((2 * c, nseg * RW_D), 1, 6)
    heads = range(at.shape[1] // RW_D)
    sl = [slice(h * RW_D, (h + 1) * RW_D) for h in heads]
    ar = [jnp.concatenate([at[:, sl[h]], rt[:, sl[h]]], axis=0) for h in heads]
    bk = [jnp.concatenate([bi[:, sl[h]], ki[:, sl[h]]], axis=0) for h in heads]
    g = [_mm_nt(ar[h], bk[h]) for h in heads]
    if nseg == 1:
        a_s = [_mm_nt(ar[h], states[h]) for h in heads]
    else:
        a_s = [_pick64(_mm_nt(ar[h], states[h].reshape(nseg * RW_D, RW_D)), rowseg) for h in heads]
    vh = [v[:, sl[h]] for h in heads]
    lp = [jnp.where(strict, g[h][:c, :c], 0.0) for h in heads]
    x = [a_s[h][:c] + _mm(jnp.where(strict, g[h][:c, c:], 0.0), vh[h]) for h in heads]
    for j in range(lg):
        x = [x[h] + _mm(lp[h], x[h]) for h in heads]
        if j < lg - 1:
            lp = [_mm(lp[h], lp[h]) for h in heads]
    uv = [jnp.concatenate([x[h], vh[h]], axis=0) for h in heads]
    mrbk = [jnp.concatenate([jnp.where(incl, g[h][c:, :c], 0.0), jnp.where(incl, g[h][c:, c:], 0.0)], axis=1)
            for h in heads]
    ys = [a_s[h][c:] + _mm(mrbk[h], uv[h]) for h in heads]
    bkd = [jnp.concatenate([bd[:, sl[h]], kd[:, sl[h]]], axis=0) for h in heads]
    if nseg == 1:
        new_states = [states[h] * pc[:, sl[h]] + _mm_tn(uv[h], bkd[h]) for h in heads]
    else:
        new_states = []
        for h in heads:
            u2 = jnp.concatenate([uv[h], uv[h]], axis=1)
            uvexp = jnp.where(blockmask, jnp.concatenate([u2] * (nseg // 2), axis=1), 0.0)
            upd = _mm_tn(uvexp, bkd[h]).reshape(nseg, RW_D, RW_D)
            new_states.append(states[h] * pc[:, :, sl[h]] + upd)
    return ys, new_states


def _rwkv_kernel(cfg, *refs):
    l, chunk, seg, zero_init = cfg
    refs = list(refs)
    p_ref = refs.pop(0)
    s0_ref, sh0_ref = (None, None) if zero_init else (refs.pop(0), refs.pop(0))
    mu_ref, vec_ref, wup_ref, ones_ref, te_ref = [refs.pop(0) for _ in range(5)]
    sprev_ref, shprev_ref = (refs.pop(0), refs.pop(0)) if l > 0 else (None, None)
    y_ref, sout_ref, shout_ref = refs.pop(0), refs.pop(0), refs.pop(0)
    s_scr, prev_scr, at_s, rt_s, bi_s, ki_s, bd_s, kd_s, v_s, lpl_s, yr_s, bon_s, gate_s = refs
    sb, tt, _ = p_ref.shape
    rows = sb * tt
    nseg = chunk // seg
    tstep = pl.program_id(1)

    @pl.when(tstep == 0)
    def _():
        if zero_init:
            s_scr[...] = jnp.zeros_like(s_scr)
            prev_scr[...] = jnp.zeros_like(prev_scr)
        else:
            s_scr[...] = s0_ref[...]
            prev_scr[:, :, 0:RW_COLS] = sh0_ref[...]
            prev_scr[:, :, RW_COLS:RW_PAD] = jnp.zeros((sb, 1, RW_PAD - RW_COLS), F32)

    vec = vec_ref[...]
    w0, a0, k_k, k_a, r_k, ln_g, ln_b = [vec[i:i + 1, :] for i in range(7)]
    ones_bd = ones_ref[...]

    rw3 = p_ref[...]
    rw = rw3.reshape(rows, RW_PAD)
    prev_rows = jnp.broadcast_to(prev_scr[...], (sb, tt, RW_PAD)).reshape(rows, RW_PAD)
    prev = jnp.where(_local_row(rw.shape, tt) == 0, prev_rows, pltpu.roll(rw, 1, axis=0))
    prev_scr[...] = rw3[:, tt - 1:tt, :]
    rwm = rw + (prev - rw) * mu_ref[...]
    r, k, v = rwm[:, 0:W], rwm[:, W:2 * W], rwm[:, 2 * W:3 * W]
    lr = rwm[:, 3 * W:3 * W + 256]
    lane = lax.broadcasted_iota(jnp.int32, lr.shape, 1)
    act = jnp.where(lane < 64, jnp.tanh(lr), jnp.where(lane < 128, lr, _sigmoid(lr)))
    up = jnp.dot(_bf(act), wup_ref[...], preferred_element_type=F32)
    log_w = -RW_DECAY * _sigmoid(w0 + up[:, 0:W])
    a = _sigmoid(a0 + up[:, W:2 * W])
    gate_s[...] = up[:, 2 * W:3 * W]
    kk = k * k_k
    kk = kk / jnp.maximum(jnp.sqrt(_headsum(kk * kk, ones_bd)), 1e-12)
    k2 = k * (1.0 + (a - 1.0) * k_a)
    kka = kk * a
    bon_s[...] = _headsum(r * k2 * r_k, ones_bd) * v
    v_s[...] = v
    te = te_ref[...]
    scans = [_cumsum_last(log_w[i * chunk:(i + 1) * chunk], te) for i in range(rows // chunk)]
    logp = jnp.concatenate([s[0] for s in scans], axis=0) if len(scans) > 1 else scans[0][0]
    lpl = jnp.concatenate([s[1] for s in scans], axis=0) if len(scans) > 1 else scans[0][1]
    lpl_s[...] = lpl
    at_s[...] = -kk * jnp.exp(logp - log_w)
    rt_s[...] = r * jnp.exp(logp)
    einv = jnp.exp(-logp)
    bi_s[...] = kka * einv
    ki_s[...] = k2 * einv
    elast = jnp.exp(lpl - logp)
    bd_s[...] = kka * elast
    kd_s[...] = k2 * elast

    def run(rs, states, pc):
        return _rwkv_recur(at_s[rs, :], rt_s[rs, :], bi_s[rs, :], ki_s[rs, :], bd_s[rs, :], kd_s[rs, :],
                           v_s[rs, :], states, pc, seg)

    def body_long(ch, carry):
        rss = [pl.ds(pl.multiple_of(q * tt + ch * chunk, chunk), chunk) for q in range(sb)]
        wide = lambda ref: jnp.concatenate([ref[rs, :] for rs in rss], axis=1)
        pc = jnp.exp(jnp.concatenate([lpl_s[pl.ds(rs.start, 1), :] for rs in rss], axis=1))
        ys, states = _rwkv_recur(wide(at_s), wide(rt_s), wide(bi_s), wide(ki_s), wide(bd_s), wide(kd_s),
                                 wide(v_s), [s_scr[q, h] for q in range(sb) for h in range(RW_H)], pc, seg)
        for q in range(sb):
            for h in range(RW_H):
                s_scr[q, h] = states[q * RW_H + h]
            yr_s[rss[q], :] = jnp.concatenate(ys[q * RW_H:(q + 1) * RW_H], axis=1)
        return carry

    def body_short(ch, carry):
        rs = pl.ds(pl.multiple_of(ch * chunk, chunk), chunk)
        s0 = pl.multiple_of(ch * nseg, nseg)
        pc = jnp.exp(lpl_s[rs, :].reshape(nseg, seg, W)[:, 0:1, :])
        ys, states = run(rs, [s_scr[pl.ds(s0, nseg), h] for h in range(RW_H)], pc)
        for h in range(RW_H):
            s_scr[pl.ds(s0, nseg), h] = states[h]
        yr_s[rs, :] = jnp.concatenate(ys, axis=1)
        return carry

    if nseg == 1:
        lax.fori_loop(0, tt // chunk, body_long, 0)
    else:
        lax.fori_loop(0, rows // chunk, body_short, 0)

    y = yr_s[...]
    mean = _headsum(y, ones_bd) * (1.0 / RW_D)
    yc = y - mean
    var = _headsum(yc * yc, ones_bd) * (1.0 / RW_D)
    yn = yc * lax.rsqrt(var + RW_GN_EPS) * ln_g + ln_b
    y_ref[...] = ((yn + bon_s[...]) * gate_s[...]).reshape(sb, tt, W)

    @pl.when(tstep == pl.num_programs(1) - 1)
    def _():
        _emit_state(l, sout_ref, sprev_ref, s_scr[...])
        _emit_state(l, shout_ref, shprev_ref, prev_scr[:, :, 0:RW_COLS])


def _rwkv_call(p, s_in, sh_in, prev, mu, vec, wup, ones_bd, te, l, rows, nlong, chunk, seg):
    bsz, t, _ = p.shape
    sb, tt = _rec_tile(bsz, t, rows, nlong)
    zero_init = s_in is None
    s_specs = _state_specs(l, sb, (RW_H, RW_D, RW_D))
    sh_specs = _state_specs(l, sb, (1, RW_COLS))
    args = ([p] + ([] if zero_init else [s_in, sh_in]) + [mu, vec, wup, ones_bd, te]
            + (list(prev) if l > 0 else []))
    specs = ([pl.BlockSpec((sb, tt, RW_PAD), lambda i, j: (i, j, 1))]
             + ([] if zero_init else [s_specs[0], sh_specs[0]])
             + [pl.BlockSpec((1, RW_PAD), lambda i, j: (0, 0)),
                pl.BlockSpec((8, W), lambda i, j: (0, 0)),
                pl.BlockSpec((256, 3 * W), lambda i, j: (0, 0)),
                pl.BlockSpec((W, W), lambda i, j: (0, 0)),
                pl.BlockSpec(te.shape, lambda i, j: (0, 0))]
             + ([s_specs[1], sh_specs[1]] if l > 0 else []))
    tile_scr = [pltpu.VMEM((sb * tt, W), F32) for _ in range(11)]
    return pl.pallas_call(
        functools.partial(_rwkv_kernel, (l, chunk, seg, zero_init)),
        out_shape=(jax.ShapeDtypeStruct((bsz, t, W), F32),
                   jax.ShapeDtypeStruct((l + 1, bsz, RW_H, RW_D, RW_D), F32),
                   jax.ShapeDtypeStruct((l + 1, bsz, 1, RW_COLS), F32)),
        grid=(bsz // sb, t // tt),
        in_specs=specs,
        out_specs=(pl.BlockSpec((sb, tt, W), lambda i, j: (i, j, 0)), s_specs[2], sh_specs[2]),
        scratch_shapes=[pltpu.VMEM((sb, RW_H, RW_D, RW_D), F32), pltpu.VMEM((sb, 1, RW_PAD), F32)] + tile_scr,
        compiler_params=_cparams(("arbitrary", "arbitrary")),
        name="rwkv7",
    )(*args)


CF_PAD = 32
CF_RB = 64
SUBLANES = 8


def _conf_kernel(cfg, *refs):
    l, zero_init = cfg
    refs = list(refs)
    p_ref = refs.pop(0)
    s0_ref = None if zero_init else refs.pop(0)
    dw_ref, vec_ref = refs.pop(0), refs.pop(0)
    prev_ref = refs.pop(0) if l > 0 else None
    y_ref, sout_ref, ext, shifted = refs
    sb, tt, _ = p_ref.shape
    tstep = pl.program_id(1)
    off = CF_PAD - (CF_K - 1)
    span = tt + CF_PAD - SUBLANES

    @pl.when(tstep == 0)
    def _():
        if zero_init:
            ext[:, 0:CF_PAD, :] = jnp.zeros((sb, CF_PAD, W), F32)
        else:
            ext[:, off:CF_PAD, :] = s0_ref[...]

    vec = vec_ref[...]
    bias, ln_g, ln_b = vec[0:1], vec[1:2], vec[2:3]
    dw = dw_ref[...]
    rb = min(CF_RB, tt)

    nslot = shifted.shape[0]

    def one_seq(s, slot):
        blk = p_ref[s]
        ext[s, CF_PAD:CF_PAD + tt, :] = blk[:, 0:W] * _sigmoid(blk[:, W:2 * W])
        for q in range(1, SUBLANES):
            shifted[slot, q, 0:span, :] = ext[s, q:q + span, :]
        for r in range(tt // rb):
            acc = jnp.zeros((rb, W), F32) + bias
            for j in range(CF_K):
                a8, q = divmod(off + j, SUBLANES)
                lo = r * rb + a8 * SUBLANES
                tap = ext[s, lo:lo + rb, :] if q == 0 else shifted[slot, q, lo:lo + rb, :]
                acc = acc + dw[j:j + 1, :] * tap
            mean = jnp.mean(acc, axis=-1, keepdims=True)
            xc = acc - mean
            var = jnp.mean(xc * xc, axis=-1, keepdims=True)
            yn = xc * lax.rsqrt(var + 1e-5) * ln_g + ln_b
            y_ref[s, r * rb:(r + 1) * rb, :] = _silu(yn)
        tail = ext[s, tt:tt + CF_PAD, :]
        ext[s, 0:CF_PAD, :] = tail

    def body(i, carry):
        for slot in range(nslot):
            one_seq(i * nslot + slot, slot)
        return carry
    lax.fori_loop(0, sb // nslot, body, 0)

    @pl.when(tstep == pl.num_programs(1) - 1)
    def _():
        _emit_state(l, sout_ref, prev_ref, ext[:, off:CF_PAD, :])


def _conf_call(p, s_in, prev, dw, vec, l, rows):
    bsz, t, _ = p.shape
    sb, tt = _tile(bsz, t, rows)
    zero_init = s_in is None
    in_spec, prev_spec, out_spec = _state_specs(l, sb, (CF_K - 1, W))
    args = [p] + ([] if zero_init else [s_in]) + [dw, vec] + ([prev] if l > 0 else [])
    specs = ([pl.BlockSpec((sb, tt, 2 * W), lambda i, j: (i, j, 4))] + ([] if zero_init else [in_spec])
             + [pl.BlockSpec((32, W), lambda i, j: (0, 0)), pl.BlockSpec((8, W), lambda i, j: (0, 0))]
             + ([prev_spec] if l > 0 else []))
    return pl.pallas_call(
        functools.partial(_conf_kernel, (l, zero_init)),
        out_shape=(jax.ShapeDtypeStruct((bsz, t, W), F32),
                   jax.ShapeDtypeStruct((l + 1, bsz, CF_K - 1, W), F32)),
        grid=(bsz // sb, t // tt),
        in_specs=specs,
        out_specs=(pl.BlockSpec((sb, tt, W), lambda i, j: (i, j, 0)), out_spec),
        scratch_shapes=[pltpu.VMEM((sb, CF_PAD + tt, W), F32),
                        pltpu.VMEM((min(sb, SEQ_UNROLL), SUBLANES, CF_PAD + tt, W), F32)],
        compiler_params=_cparams(("arbitrary", "arbitrary")),
        name="conformer",
    )(*args)


LRU_PAD = 8


def _gelu_tanh(x):
    return 0.5 * x * (1.0 + jnp.tanh(0.7978845608028654 * (x + 0.044715 * (x * x * x))))


def _lru_kernel(cfg, *refs):
    l, zero_init = cfg
    refs = list(refs)
    p_ref = refs.pop(0)
    h0_ref, c0_ref = (None, None) if zero_init else (refs.pop(0), refs.pop(0))
    cw_ref, vec_ref, wax_ref = refs.pop(0), refs.pop(0), refs.pop(0)
    hprev_ref, cprev_ref = (refs.pop(0), refs.pop(0)) if l > 0 else (None, None)
    y_ref, hout_ref, cout_ref, ext, hcar = refs
    sb, tt, _ = p_ref.shape
    tstep = pl.program_id(1)
    off = LRU_PAD - (LRU_K - 1)

    @pl.when(tstep == 0)
    def _():
        if zero_init:
            ext[:, 0:LRU_PAD, :] = jnp.zeros((sb, LRU_PAD, W), F32)
            hcar[...] = jnp.zeros_like(hcar)
        else:
            ext[:, off:LRU_PAD, :] = c0_ref[...]
            hcar[...] = h0_ref[...]

    vec = vec_ref[...]
    cb, ba, bx, lam = vec[0:1], vec[1:2], vec[2:3], vec[3:4]
    sp = jnp.maximum(-lam, 0.0) + jnp.log1p(jnp.exp(-jnp.abs(lam)))
    cw = cw_ref[...]
    wax = wax_ref[...]

    def body(s, carry):
        blk = p_ref[s]
        ext[s, LRU_PAD:LRU_PAD + tt, :] = blk[:, 0:W]
        xc = jnp.zeros((tt, W), F32) + cb
        for j in range(LRU_K):
            xc = xc + cw[j:j + 1, :] * ext[s, pl.ds(off + j, tt), :]
        pre = jnp.dot(_bf(xc), wax, preferred_element_type=F32)
        rg = _sigmoid(pre[:, 0:W] + ba)
        ig = _sigmoid(pre[:, W:2 * W] + bx)
        log_a = -LRU_C * rg * sp
        a = jnp.exp(log_a)
        bv = jnp.sqrt(1.0 - a * a) * (ig * xc)
        d = 1
        while d < tt:
            a_s = _shift_rows(a, d, 1.0, tt)
            b_s = _shift_rows(bv, d, 0.0, tt)
            bv = a * b_s + bv
            a = a * a_s
            d *= 2
        hs = a * hcar[s] + bv
        hcar[s] = hs[tt - 1:tt, :]
        y_ref[s] = hs * _gelu_tanh(blk[:, W:2 * W])
        tail = ext[s, tt:tt + LRU_PAD, :]
        ext[s, 0:LRU_PAD, :] = tail
        return carry
    lax.fori_loop(0, sb, body, 0, unroll=min(sb, SEQ_UNROLL))

    @pl.when(tstep == pl.num_programs(1) - 1)
    def _():
        _emit_state(l, hout_ref, hprev_ref, hcar[...])
        _emit_state(l, cout_ref, cprev_ref, ext[:, off:LRU_PAD, :])


def _lru_call(p, h_in, c_in, prev, cw, vec, wax, l, rows):
    bsz, t, _ = p.shape
    sb, tt = _tile(bsz, t, rows)
    zero_init = h_in is None
    h_specs = _state_specs(l, sb, (1, W))
    c_specs = _state_specs(l, sb, (LRU_K - 1, W))
    args = [p] + ([] if zero_init else [h_in, c_in]) + [cw, vec, wax] + (list(prev) if l > 0 else [])
    specs = ([pl.BlockSpec((sb, tt, 2 * W), lambda i, j: (i, j, 5))]
             + ([] if zero_init else [h_specs[0], c_specs[0]])
             + [pl.BlockSpec((8, W), lambda i, j: (0, 0)), pl.BlockSpec((8, W), lambda i, j: (0, 0)),
                pl.BlockSpec((W, 2 * W), lambda i, j: (0, 0))]
             + ([h_specs[1], c_specs[1]] if l > 0 else []))
    return pl.pallas_call(
        functools.partial(_lru_kernel, (l, zero_init)),
        out_shape=(jax.ShapeDtypeStruct((bsz, t, W), F32),
                   jax.ShapeDtypeStruct((l + 1, bsz, 1, W), F32),
                   jax.ShapeDtypeStruct((l + 1, bsz, LRU_K - 1, W), F32)),
        grid=(bsz // sb, t // tt),
        in_specs=specs,
        out_specs=(pl.BlockSpec((sb, tt, W), lambda i, j: (i, j, 0)), h_specs[2], c_specs[2]),
        scratch_shapes=[pltpu.VMEM((sb, LRU_PAD + tt, W), F32), pltpu.VMEM((sb, 1, W), F32)],
        compiler_params=_cparams(("arbitrary", "arbitrary")),
        name="rglru",
    )(*args)


def _merge_kernel(yhg_ref, yrw_ref, ycf_ref, ylr_ref, x_ref, sc1_ref, sh1_ref, g1_ref, sc2_ref, sh2_ref,
                  nmix_ref, nmlp_ref, wg_ref, bg_ref, wb_ref, wo_ref, x1_ref, h2_ref):
    sb, tt, _ = x_ref.shape
    tm = sb * tt
    x = x_ref[...]
    h = _bf(_adaln(x, nmix_ref[...], sc1_ref[...], sh1_ref[...]).reshape(tm, D))
    z = jnp.zeros((tm, D), F32)
    for b, y_ref in enumerate((yhg_ref, yrw_ref, ycf_ref, ylr_ref)):
        bo = jnp.dot(_bf(y_ref[...].reshape(tm, W)), wb_ref[b], preferred_element_type=F32)
        logit = jnp.dot(h, wg_ref[:, b * D:(b + 1) * D], preferred_element_type=F32) + bg_ref[:, b * D:(b + 1) * D]
        z = z + _sigmoid(logit) * bo
    out = jnp.dot(_bf(z), wo_ref[...], preferred_element_type=F32).reshape(sb, tt, D)
    x1 = x + g1_ref[...] * out
    x1_ref[...] = x1
    h2_ref[...] = _bf(_adaln(x1, nmlp_ref[...], sc2_ref[...], sh2_ref[...]))


def _merge_call(ys, x, mods, nmix, nmlp, wg, bg, wb, wo, rows=512):
    bsz, t, _ = x.shape
    sb, tt = _tile(bsz, t, rows)
    tok = lambda width: pl.BlockSpec((sb, tt, width), lambda i, j: (i, j, 0))
    seq = pl.BlockSpec((sb, 1, D), lambda i, j: (i, 0, 0))
    const = lambda shape: pl.BlockSpec(shape, lambda i, j: (0,) * len(shape), pipeline_mode=pl.Buffered(1))
    return pl.pallas_call(
        _merge_kernel,
        out_shape=(jax.ShapeDtypeStruct((bsz, t, D), F32),
                   jax.ShapeDtypeStruct((bsz, t, D), BF16)),
        grid=(bsz // sb, t // tt),
        in_specs=[tok(W), tok(W), tok(W), tok(W), tok(D), seq, seq, seq, seq, seq,
                  const((1, D)), const((1, D)), const((D, 4 * D)), const((1, 4 * D)),
                  const((4, W, D)), const((D, D))],
        out_specs=(tok(D), tok(D)),
        compiler_params=_cparams(("arbitrary", "arbitrary")),
        name="merge",
    )(*ys, x, *mods, nmix, nmlp, wg, bg, wb, wo)


def _mlp_kernel(final, h2_ref, x1_ref, g2_ref, fg_ref, w1_ref, w2_ref, o_ref, acc):
    sb, tt, _ = x1_ref.shape
    k = pl.program_id(2)

    @pl.when(k == 0)
    def _():
        acc[...] = jnp.zeros_like(acc)

    hid = jnp.dot(h2_ref[...].reshape(sb * tt, D), w1_ref[...], preferred_element_type=F32)
    act = jnp.square(jnp.maximum(hid, 0.0))
    acc[...] += jnp.dot(_bf(act), w2_ref[...], preferred_element_type=F32)

    @pl.when(k == pl.num_programs(2) - 1)
    def _():
        x2 = x1_ref[...] + g2_ref[...] * acc[...].reshape(sb, tt, D)
        if final:
            ms = jnp.mean(x2 * x2, axis=-1, keepdims=True)
            x2 = x2 * lax.rsqrt(ms + EPS) * fg_ref[...]
        o_ref[...] = x2


def _mlp_call(h2, x1, g2, fg, w1, w2, final, rows=1024, th=1024):
    bsz, t, _ = x1.shape
    sb, tt = _tile(bsz, t, rows)
    tok = pl.BlockSpec((sb, tt, D), lambda i, j, k: (i, j, 0))
    return pl.pallas_call(
        functools.partial(_mlp_kernel, final),
        out_shape=jax.ShapeDtypeStruct((bsz, t, D), F32),
        grid=(bsz // sb, t // tt, HID // th),
        in_specs=[tok, tok,
                  pl.BlockSpec((sb, 1, D), lambda i, j, k: (i, 0, 0)),
                  pl.BlockSpec((1, D), lambda i, j, k: (0, 0)),
                  pl.BlockSpec((D, th), lambda i, j, k: (0, k)),
                  pl.BlockSpec((th, D), lambda i, j, k: (k, 0))],
        out_specs=tok,
        scratch_shapes=[pltpu.VMEM((sb * tt, D), F32)],
        compiler_params=_cparams(("arbitrary", "arbitrary", "arbitrary")),
        name="mlp",
    )(h2, x1, g2, fg, w1, w2)


def _block_diag(w):
    n, c, d = w.shape
    eye = jnp.eye(n, dtype=w.dtype)
    return (eye[:, None, :, None] * w[:, :, None, :]).reshape(n * c, n * d)


def _prep_layer(wt, l):
    w_in = wt["w_in"][l]
    zpad = jnp.zeros((D, RW_PAD - RW_COLS), F32)
    wcat = jnp.concatenate([w_in[:, 0:2048], w_in[:, 2048:2048 + RW_COLS], zpad, w_in[:, 2048 + RW_COLS:]], axis=1)
    wup = jnp.zeros((256, 3 * W), F32)
    wup = wup.at[0:64, 0:W].set(wt["rw_w_up"][l])
    wup = wup.at[64:128, W:2 * W].set(wt["rw_a_up"][l])
    wup = wup.at[128:256, 2 * W:3 * W].set(wt["rw_g_up"][l])
    zrow = jnp.zeros((W,), F32)
    rw_vec = jnp.stack([wt["rw_w0"][l], wt["rw_a0"][l], wt["rw_k_k"][l], wt["rw_k_a"][l], wt["rw_r_k"][l],
                        wt["rw_ln_g"][l], wt["rw_ln_b"][l], zrow])
    mu = jnp.concatenate([wt["rw_mu"][l], jnp.zeros((RW_PAD - RW_COLS,), F32)])[None, :]
    cf_vec = jnp.stack([wt["cf_dw_b"][l], wt["cf_ln_g"][l], wt["cf_ln_b"][l]] + [zrow] * 5)
    cf_dw = jnp.concatenate([wt["cf_dw"][l], jnp.zeros((1, W), F32)], axis=0)
    lru_vec = jnp.stack([wt["lru_conv_b"][l], wt["lru_ba"][l], wt["lru_bx"][l], wt["lru_lambda"][l]] + [zrow] * 4)
    lru_cw = jnp.concatenate([wt["lru_conv_w"][l], jnp.zeros((8 - LRU_K, W), F32)], axis=0)
    wax = jnp.concatenate([_block_diag(wt["lru_wa"][l]), _block_diag(wt["lru_wx"][l])], axis=1)
    return dict(
        wcat=_bf(wcat), wup=_bf(wup), rw_vec=rw_vec, mu=mu, cf_vec=cf_vec, cf_dw=cf_dw,
        lru_vec=lru_vec, lru_cw=lru_cw, wax=_bf(wax), wg=_bf(wt["w_gate"][l]), bg=wt["b_gate"][l][None, :],
        wb=_bf(wt["w_branch"][l]), wo=_bf(wt["w_out"][l]), w1=_bf(wt["w_mlp1"][l]), w2=_bf(wt["w_mlp2"][l]),
        hg_gn=wt["hg_norm_g"][l][None, :], nmix=wt["norm_mix_g"][l][None, :], nmlp=wt["norm_mlp_g"][l][None, :],
    )


def _trunk(x, c, states, wt, layers, ones_bd, mix_rows, rec_rows, nlong, chunk, seg):
    nl = len(layers)
    te = _scan_mats(chunk, seg)
    if states is None:
        s_hg = s_rw = s_shift = s_cf = s_lh = s_lc = None
    else:
        s_hg, s_rw, s_shift, s_cf, s_lh, s_lc = states
        s_shift = s_shift[:, :, None, :]
        s_lh = s_lh[:, :, None, :]
    mod = _mod_call(c, wt["ada_w"], wt["ada_b"])
    fg = wt["norm_final_g"][None, :]
    n_hg = n_rw = n_sh = n_cf = n_lh = n_lc = None
    for l, lw in enumerate(layers):
        sh1, sc1, g1, sh2, sc2, g2 = [mod[l, i][:, None, :] for i in range(6)]
        p = _inproj_call(x, sc1, sh1, lw["nmix"], lw["wcat"])
        y_hg, n_hg = _hgrn_call(p, s_hg, n_hg, wt["hg_lower"], lw["hg_gn"], te, l, rec_rows, nlong, chunk, seg)
        y_rw, n_rw, n_sh = _rwkv_call(p, s_rw, s_shift, (n_rw, n_sh), lw["mu"], lw["rw_vec"], lw["wup"], ones_bd,
                                      te, l, rec_rows, nlong, chunk, seg)
        y_cf, n_cf = _conf_call(p, s_cf, n_cf, lw["cf_dw"], lw["cf_vec"], l, mix_rows)
        y_lr, n_lh, n_lc = _lru_call(p, s_lh, s_lc, (n_lh, n_lc), lw["lru_cw"], lw["lru_vec"], lw["wax"], l,
                                     mix_rows)
        x1, h2 = _merge_call((y_hg, y_rw, y_cf, y_lr), x, (sc1, sh1, g1, sc2, sh2), lw["nmix"], lw["nmlp"],
                             lw["wg"], lw["bg"], lw["wb"], lw["wo"])
        x = _mlp_call(h2, x1, g2, fg, lw["w1"], lw["w2"], final=(l == nl - 1))
    return x, [n_hg, n_rw, n_sh[:, :, 0, :], n_cf, n_lh[:, :, 0, :], n_lc]


def _run(x_prompt, x_sample, sample_states, c_prompt, c_sample, wt):
    nl = wt["w_in"].shape[0]
    layers = [_prep_layer(wt, l) for l in range(nl)]
    head = jnp.arange(W, dtype=jnp.int32) // RW_D
    ones_bd = _bf((head[:, None] == head[None, :]).astype(F32))
    tp = x_prompt.shape[1]
    ts = x_sample.shape[1]
    rows_p = min(256, tp)
    chunk_p = min(64, tp)
    y_p, st_p = _trunk(x_prompt, c_prompt, None, wt, layers, ones_bd, rows_p, rows_p // 2, 4, chunk_p, chunk_p)
    y_s, st_s = _trunk(x_sample, c_sample, sample_states, wt, layers, ones_bd, 8 * ts, 8 * ts, 1, 8 * ts, ts)
    return (y_p, y_s, *st_p, *st_s)


def kernel(x_prompt, x_sample, state_hgrn, state_rwkv, state_rwkv_shift, state_conv, state_lru_h, state_lru_conv, c_prompt, c_sample, ada_w, ada_b, norm_mix_g, norm_mlp_g, norm_final_g, w_in, hg_lower, hg_norm_g, rw_mu, rw_w0, rw_w_up, rw_a0, rw_a_up, rw_g_up, rw_k_k, rw_k_a, rw_r_k, rw_ln_g, rw_ln_b, cf_dw, cf_dw_b, cf_ln_g, cf_ln_b, lru_conv_w, lru_conv_b, lru_wa, lru_ba, lru_wx, lru_bx, lru_lambda, w_branch, w_gate, b_gate, w_out, w_mlp1, w_mlp2):
    wt = dict(ada_w=ada_w, ada_b=ada_b, norm_mix_g=norm_mix_g, norm_mlp_g=norm_mlp_g,
              norm_final_g=norm_final_g, w_in=w_in, hg_lower=hg_lower, hg_norm_g=hg_norm_g, rw_mu=rw_mu,
              rw_w0=rw_w0, rw_w_up=rw_w_up, rw_a0=rw_a0, rw_a_up=rw_a_up, rw_g_up=rw_g_up, rw_k_k=rw_k_k,
              rw_k_a=rw_k_a, rw_r_k=rw_r_k, rw_ln_g=rw_ln_g, rw_ln_b=rw_ln_b, cf_dw=cf_dw, cf_dw_b=cf_dw_b,
              cf_ln_g=cf_ln_g, cf_ln_b=cf_ln_b, lru_conv_w=lru_conv_w, lru_conv_b=lru_conv_b, lru_wa=lru_wa,
              lru_ba=lru_ba, lru_wx=lru_wx, lru_bx=lru_bx, lru_lambda=lru_lambda, w_branch=w_branch,
              w_gate=w_gate, b_gate=b_gate, w_out=w_out, w_mlp1=w_mlp1, w_mlp2=w_mlp2)
    sample_states = (state_hgrn, state_rwkv, state_rwkv_shift, state_conv, state_lru_h, state_lru_conv)
    return _run(x_prompt, x_sample, sample_states, c_prompt, c_sample, wt)
```

```python
import functools
import math

import jax
import jax.numpy as jnp
from jax import lax
from jax.experimental import pallas as pl
from jax.experimental.pallas import tpu as pltpu

D = 1024
W = 512
HG_H = 4
HG_D = 128
RW_H = 8
RW_D = 64
RW_COLS = 1792
RW_PAD = 2048
CF_K = 31
LRU_K = 4
HID = 4096
EPS = 1e-6
RW_GN_EPS = 64e-5
RW_DECAY = 0.606531
LRU_C = 8.0
LANES = 128

P_COLS = 6144
SEQ_UNROLL = 2
VMEM_LIMIT = 56 * 1024 * 1024

F32 = jnp.float32
BF16 = jnp.bfloat16


def _bf(x):
    return x.astype(BF16)


def _mm(a, b):
    return jnp.dot(_bf(a), _bf(b), preferred_element_type=F32)


def _mm_nt(a, b):
    return lax.dot_general(_bf(a), _bf(b), (((1,), (1,)), ((), ())), preferred_element_type=F32)


def _mm_tn(a, b):
    return lax.dot_general(_bf(a), _bf(b), (((0,), (0,)), ((), ())), preferred_element_type=F32)


def _sigmoid(x):
    return 1.0 / (1.0 + jnp.exp(-x))


def _silu(x):
    return x * _sigmoid(x)


def _local_row(shape, seg):
    return jnp.bitwise_and(lax.broadcasted_iota(jnp.int32, shape, 0), seg - 1)


def _shift_rows(x, d, fill, seg):
    return jnp.where(_local_row(x.shape, seg) >= d, pltpu.roll(x, d, axis=0), fill)


def _scan_mats(chunk, seg):
    r = jnp.arange(chunk, dtype=jnp.int32)
    same = (r[:, None] // seg) == (r[None, :] // seg)
    tri = same & (r[None, :] <= r[:, None])
    return _bf(jnp.concatenate([tri, same], axis=0).astype(F32))


def _cumsum_last(x, te):
    c = x.shape[0]
    p1 = _bf(x)
    r1 = x - p1.astype(F32)
    p2 = _bf(r1)
    p3 = _bf(r1 - p2.astype(F32))
    out = (jnp.dot(te, p1, preferred_element_type=F32) + jnp.dot(te, p2, preferred_element_type=F32)
           + jnp.dot(te, p3, preferred_element_type=F32))
    return out[:c], out[c:]


def _seg_id(shape, axis, log2_seg):
    return lax.shift_right_logical(lax.broadcasted_iota(jnp.int32, shape, axis), log2_seg)


def _headsum(x, ones_bd):
    hi = _bf(x)
    lo = _bf(x - hi.astype(F32))
    return (jnp.dot(hi, ones_bd, preferred_element_type=F32)
            + jnp.dot(lo, ones_bd, preferred_element_type=F32))


def _cparams(sem):
    return pltpu.CompilerParams(dimension_semantics=sem, vmem_limit_bytes=VMEM_LIMIT)


def _tile(bsz, t, rows):
    if t >= rows:
        return 1, rows
    return min(bsz, rows // t), t


def _state_specs(l, sb, tail):
    zeros = (0,) * len(tail)
    in_spec = pl.BlockSpec((None, sb) + tail, lambda i, j: (l, i) + zeros)
    prev_spec = pl.BlockSpec((l, sb) + tail, lambda i, j: (0, i) + zeros)
    out_spec = pl.BlockSpec((l + 1, sb) + tail, lambda i, j: (0, i) + zeros)
    return in_spec, prev_spec, out_spec


def _emit_state(l, out_ref, prev_ref, new):
    for q in range(l):
        out_ref[q] = prev_ref[q]
    out_ref[l] = new


def _mod_kernel(c_ref, w_ref, b_ref, o_ref):
    c = c_ref[...]
    o_ref[0, 0] = _mm(_silu(c), w_ref[0]) + b_ref[0, 0]


def _mod_call(c, ada_w, ada_b):
    nl = ada_w.shape[0]
    bsz = c.shape[0]
    return pl.pallas_call(
        _mod_kernel,
        out_shape=jax.ShapeDtypeStruct((nl, 6, bsz, D), F32),
        grid=(nl, 6),
        in_specs=[
            pl.BlockSpec((bsz, D), lambda l, j: (0, 0)),
            pl.BlockSpec((1, D, D), lambda l, j: (l, 0, j)),
            pl.BlockSpec((1, 1, 1, D), lambda l, j: (l, j, 0, 0)),
        ],
        out_specs=pl.BlockSpec((1, 1, bsz, D), lambda l, j: (l, j, 0, 0)),
        compiler_params=_cparams(("arbitrary", "arbitrary")),
        name="adaln_mod",
    )(c, ada_w, ada_b.reshape(nl, 6, 1, D))


def _adaln(x, g, sc, sh):
    ms = jnp.mean(x * x, axis=-1, keepdims=True)
    y = x * lax.rsqrt(ms + EPS) * g
    return y * (1.0 + sc) + sh


CF_PAD = 32
CF_RB = 64
SUBLANES = 8
LRU_PAD = 8
REC_COLS = 4096
REC_PIECE = 256


def _gelu_tanh(x):
    return 0.5 * x * (1.0 + jnp.tanh(0.7978845608028654 * (x + 0.044715 * (x * x * x))))


def _conformer_prep(pcf, s, slot, ext, shifted):
    tt = pcf.shape[0]
    span = tt + CF_PAD - SUBLANES
    ext[s, CF_PAD:CF_PAD + tt, :] = pcf[:, 0:W] * _sigmoid(pcf[:, W:2 * W])
    for q in range(1, SUBLANES):
        shifted[slot, q, 0:span, :] = ext[s, q:q + span, :]


def _conformer_rows(r0, rb, s, slot, ext, shifted, y_ref, dw, bias, ln_g, ln_b):
    off = CF_PAD - (CF_K - 1)
    acc = jnp.zeros((rb, W), F32) + bias
    for j in range(CF_K):
        a8, q = divmod(off + j, SUBLANES)
        lo = r0 + a8 * SUBLANES
        tap = ext[s, lo:lo + rb, :] if q == 0 else shifted[slot, q, lo:lo + rb, :]
        acc = acc + dw[j:j + 1, :] * tap
    mean = jnp.mean(acc, axis=-1, keepdims=True)
    xc = acc - mean
    var = jnp.mean(xc * xc, axis=-1, keepdims=True)
    yn = xc * lax.rsqrt(var + 1e-5) * ln_g + ln_b
    y_ref[s, r0:r0 + rb, :] = _silu(yn)


def _linear_scan(a, b, h):
    n = a.shape[0]
    d = 1
    while d < SUBLANES:
        a_s = _shift_rows(a, d, 1.0, SUBLANES)
        b_s = _shift_rows(b, d, 0.0, SUBLANES)
        b = a * b_s + b
        a = a * a_s
        d *= 2
    out = []
    for g in range(n // SUBLANES):
        hs = a[g * SUBLANES:(g + 1) * SUBLANES] * h + b[g * SUBLANES:(g + 1) * SUBLANES]
        h = hs[SUBLANES - 1:SUBLANES, :]
        out.append(hs)
    return (jnp.concatenate(out, axis=0) if len(out) > 1 else out[0]), h


def _lru_rows(r0, rb, plr, s, ext, h, y_ref, cw, cb, ba, bx, sp, wax):
    off = LRU_PAD - (LRU_K - 1)
    xc = jnp.zeros((rb, W), F32) + cb
    for j in range(LRU_K):
        xc = xc + cw[j:j + 1, :] * ext[s, r0 + off + j:r0 + off + j + rb, :]
    pre = jnp.dot(_bf(xc), wax, preferred_element_type=F32)
    rg = _sigmoid(pre[:, 0:W] + ba)
    ig = _sigmoid(pre[:, W:2 * W] + bx)
    a = jnp.exp(-LRU_C * rg * sp)
    hs, h = _linear_scan(a, jnp.sqrt(1.0 - a * a) * (ig * xc), h)
    y_ref[s, r0:r0 + rb, :] = hs * _gelu_tanh(plr[r0:r0 + rb, W:2 * W])
    return h


def _inproj_kernel(cfg, *refs):
    l, zero_init = cfg
    refs = list(refs)
    x_ref, sc_ref, sh_ref, g_ref, w_ref = [refs.pop(0) for _ in range(5)]
    cf0_ref, lh0_ref, lc0_ref = (None,) * 3 if zero_init else [refs.pop(0) for _ in range(3)]
    dw_ref, cfv_ref, cw_ref, lrv_ref, wax_ref = [refs.pop(0) for _ in range(5)]
    cfp_ref, lhp_ref, lcp_ref = [refs.pop(0) for _ in range(3)] if l > 0 else (None,) * 3
    p_ref, ycf_ref, ylr_ref, cfo_ref, lho_ref, lco_ref = [refs.pop(0) for _ in range(6)]
    cf_ext, shifted, lr_ext, hcar, pc_scr = refs
    sb, tt, _ = x_ref.shape
    tm = sb * tt
    tstep = pl.program_id(1)
    cf_off = CF_PAD - (CF_K - 1)
    lr_off = LRU_PAD - (LRU_K - 1)

    @pl.when(tstep == 0)
    def _():
        if zero_init:
            cf_ext[:, 0:CF_PAD, :] = jnp.zeros((sb, CF_PAD, W), F32)
            lr_ext[:, 0:LRU_PAD, :] = jnp.zeros((sb, LRU_PAD, W), F32)
            hcar[...] = jnp.zeros_like(hcar)
        else:
            cf_ext[:, cf_off:CF_PAD, :] = cf0_ref[...]
            lr_ext[:, lr_off:LRU_PAD, :] = lc0_ref[...]
            hcar[...] = lh0_ref[...]

    h = _bf(_adaln(x_ref[...], g_ref[...], sc_ref[...], sh_ref[...]).reshape(tm, D))
    pc_scr[...] = jnp.dot(h, w_ref[:, REC_COLS:P_COLS], preferred_element_type=F32).reshape(sb, tt, 4 * W)

    def rec_piece(n):
        cols = slice(n * REC_PIECE, (n + 1) * REC_PIECE)
        p_ref[:, :, cols] = jnp.dot(h, w_ref[:, cols], preferred_element_type=F32).reshape(sb, tt, REC_PIECE)
    rec_pieces = list(range(REC_COLS // REC_PIECE))

    cfv = cfv_ref[...]
    dw = dw_ref[...]
    lrv = lrv_ref[...]
    lam = lrv[3:4]
    sp = jnp.maximum(-lam, 0.0) + jnp.log1p(jnp.exp(-jnp.abs(lam)))
    cw = cw_ref[...]
    wax = wax_ref[...]
    nslot = shifted.shape[0]

    rb = min(CF_RB, tt)

    def one_seq(s, slot, pieces):
        blk = pc_scr[s]
        plr = blk[:, 2 * W:4 * W]
        _conformer_prep(blk[:, 0:2 * W], s, slot, cf_ext, shifted)
        lr_ext[s, LRU_PAD:LRU_PAD + tt, :] = plr[:, 0:W]
        hstate = hcar[s]
        per_blk = -(-len(pieces) // (2 * (tt // rb)))
        for r0 in range(0, tt, rb):
            for _ in range(min(per_blk, len(pieces))):
                rec_piece(pieces.pop(0))
            _conformer_rows(r0, rb, s, slot, cf_ext, shifted, ycf_ref, dw, cfv[0:1], cfv[1:2], cfv[2:3])
            for _ in range(min(per_blk, len(pieces))):
                rec_piece(pieces.pop(0))
            hstate = _lru_rows(r0, rb, plr, s, lr_ext, hstate, ylr_ref, cw, lrv[0:1], lrv[1:2], lrv[2:3], sp, wax)
        hcar[s] = hstate
        cf_tail = cf_ext[s, tt:tt + CF_PAD, :]
        cf_ext[s, 0:CF_PAD, :] = cf_tail
        lr_tail = lr_ext[s, tt:tt + LRU_PAD, :]
        lr_ext[s, 0:LRU_PAD, :] = lr_tail

    if sb == 1:
        one_seq(0, 0, rec_pieces)
    for n in list(rec_pieces):
        rec_piece(n)
    if sb > 1:
        def body(i, carry):
            for slot in range(nslot):
                one_seq(i * nslot + slot, slot, [])
            return carry
        lax.fori_loop(0, sb // nslot, body, 0)

    @pl.when(tstep == pl.num_programs(1) - 1)
    def _():
        _emit_state(l, cfo_ref, cfp_ref, cf_ext[:, cf_off:CF_PAD, :])
        _emit_state(l, lho_ref, lhp_ref, hcar[...])
        _emit_state(l, lco_ref, lcp_ref, lr_ext[:, lr_off:LRU_PAD, :])


def _inproj_call(x, sc, sh, g, wcat, conv_in, conv_prev, dw, cfv, cw, lrv, wax, l, rows=256):
    bsz, t, _ = x.shape
    sb, tt = _tile(bsz, t, rows)
    zero_init = conv_in is None
    tails = ((CF_K - 1, W), (1, W), (LRU_K - 1, W))
    sspecs = [_state_specs(l, sb, tail) for tail in tails]
    tok = lambda width: pl.BlockSpec((sb, tt, width), lambda i, j: (i, j, 0))
    seq = pl.BlockSpec((sb, 1, D), lambda i, j: (i, 0, 0))
    const = lambda shape: pl.BlockSpec(shape, lambda i, j: (0,) * len(shape), pipeline_mode=pl.Buffered(1))
    args = ([x, sc, sh, g, wcat] + ([] if zero_init else list(conv_in)) + [dw, cfv, cw, lrv, wax]
            + (list(conv_prev) if l > 0 else []))
    specs = ([tok(D), seq, seq, const((1, D)), const((D, P_COLS))]
             + ([] if zero_init else [sp[0] for sp in sspecs])
             + [const((32, W)), const((8, W)), const((8, W)), const((8, W)), const((W, 2 * W))]
             + ([sp[1] for sp in sspecs] if l > 0 else []))
    nslot = min(sb, SEQ_UNROLL)
    return pl.pallas_call(
        functools.partial(_inproj_kernel, (l, zero_init)),
        out_shape=(jax.ShapeDtypeStruct((bsz, t, REC_COLS), F32),
                   jax.ShapeDtypeStruct((bsz, t, W), F32),
                   jax.ShapeDtypeStruct((bsz, t, W), F32))
        + tuple(jax.ShapeDtypeStruct((l + 1, bsz) + tail, F32) for tail in tails),
        grid=(bsz // sb, t // tt),
        in_specs=specs,
        out_specs=(tok(REC_COLS), tok(W), tok(W)) + tuple(sp[2] for sp in sspecs),
        scratch_shapes=[pltpu.VMEM((sb, CF_PAD + tt, W), F32),
                        pltpu.VMEM((nslot, SUBLANES, CF_PAD + tt, W), F32),
                        pltpu.VMEM((sb, LRU_PAD + tt, W), F32),
                        pltpu.VMEM((sb, 1, W), F32),
                        pltpu.VMEM((sb, tt, 4 * W), F32)],
        compiler_params=_cparams(("arbitrary", "arbitrary")),
        name="inproj",
    )(*args)


HG_SUB = 16


def _hgrn_chunk(qr, fz, iv, og, lb, gn, te, states, seg):
    c, width = qr.shape
    nseg = c // seg
    heads = range(width // HG_D)
    q = _silu(qr)
    f = lb + (1.0 - lb) * _sigmoid(fz)
    logf = jnp.log(f)
    kf = (1.0 - lb) * _sigmoid(-fz)
    b, b_last = _cumsum_last(logf, te)
    qe = q * jnp.exp(b)
    kdec = kf * jnp.exp(b_last - b)
    e_last = jnp.exp(b_last)
    sl = [slice(h * HG_D, (h + 1) * HG_D) for h in heads]

    if nseg == 1:
        sub = min(HG_SUB, c)
        nsub = c // sub
        o_inter = [_mm_nt(qe[:, sl[h]], states[h]) for h in heads]
        new_states = [states[h] * e_last[0:1, sl[h]] + _mm_tn(iv[:, sl[h]], kdec[:, sl[h]]) for h in heads]
        pieces = [[] for _ in heads]
        for i in range(nsub):
            r0 = i * sub
            m = b[r0 - 1:r0, :] if i > 0 else jnp.zeros((1, width), F32)
            qs = q[r0:r0 + sub] * jnp.exp(b[r0:r0 + sub] - m)
            kd = kf[r0:r0 + sub] * jnp.exp(jnp.minimum(m - b[r0:r0 + sub], 80.0))
            if i > 0:
                kall = jnp.concatenate([kf[:r0] * jnp.exp(m - b[:r0]), kd], axis=0)
            else:
                kall = kd
            row = lax.broadcasted_iota(jnp.int32, (sub, r0 + sub), 0) + r0
            col = lax.broadcasted_iota(jnp.int32, (sub, r0 + sub), 1)
            causal = col <= row
            scs = [jnp.where(causal, _mm_nt(qs[:, sl[h]], kall[:, sl[h]]), 0.0) for h in heads]
            for h in heads:
                pieces[h].append(_mm(scs[h], iv[:r0 + sub, sl[h]]))
        o_intra = [jnp.concatenate(pieces[h], axis=0) if nsub > 1 else pieces[h][0] for h in heads]
    else:
        lg = int(math.log2(seg))
        kd = kf * jnp.exp(jnp.minimum(-b, 80.0))
        rr = lax.broadcasted_iota(jnp.int32, (c, c), 0)
        cc = lax.broadcasted_iota(jnp.int32, (c, c), 1)
        causal = (lax.shift_right_logical(rr, lg) == lax.shift_right_logical(cc, lg)) & (cc <= rr)
        rowseg = _seg_id((c, HG_D), 0, lg)
        blockmask = _seg_id((c, nseg * HG_D), 0, lg) == _seg_id((c, nseg * HG_D), 1, 7)
        e3 = e_last.reshape(nseg, seg, width)[:, 0:1, :]
        full = [_mm_nt(qe[:, sl[h]], states[h].reshape(nseg * HG_D, HG_D)) for h in heads]
        o_inter = []
        for h in heads:
            o = full[h][:, 0:HG_D]
            for s in range(1, nseg):
                o = jnp.where(rowseg == s, full[h][:, s * HG_D:(s + 1) * HG_D], o)
            o_inter.append(o)
        scs = [jnp.where(causal, _mm_nt(qe[:, sl[h]], kd[:, sl[h]]), 0.0) for h in heads]
        o_intra = [_mm(scs[h], iv[:, sl[h]]) for h in heads]
        new_states = []
        for h in heads:
            ivexp = jnp.where(blockmask, jnp.concatenate([iv[:, sl[h]]] * nseg, axis=1), 0.0)
            upd = _mm_tn(ivexp, kdec[:, sl[h]]).reshape(nseg, HG_D, HG_D)
            new_states.append(states[h] * e3[:, :, sl[h]] + upd)
    outs = []
    for h in heads:
        o = o_inter[h] + o_intra[h]
        outs.append(o * lax.rsqrt(jnp.mean(o * o, axis=-1, keepdims=True) + EPS))
    y = jnp.concatenate(outs, axis=1) * gn * _silu(og)
    return y, new_states


def _hgrn_kernel(cfg, *refs):
    l, chunk, seg, zero_init = cfg
    refs = list(refs)
    p_ref = refs.pop(0)
    s0_ref = None if zero_init else refs.pop(0)
    lower_ref, gn_ref, te_ref = refs.pop(0), refs.pop(0), refs.pop(0)
    prev_ref = refs.pop(0) if l > 0 else None
    y_ref, sout_ref, s_scr = refs
    sb, tt, _ = p_ref.shape
    nseg = chunk // seg
    tstep = pl.program_id(1)

    @pl.when(tstep == 0)
    def _():
        if zero_init:
            s_scr[...] = jnp.zeros_like(s_scr)
        else:
            def init(i, carry):
                for h in range(HG_H):
                    s_scr[i, h] = s0_ref[i, h].T
                return carry
            lax.fori_loop(0, sb, init, 0)

    low = lower_ref[...]
    e = jnp.exp(low - jnp.max(low, axis=0, keepdims=True))
    sm = e / jnp.sum(e, axis=0, keepdims=True)
    lb = jnp.sum(sm[:l + 1], axis=0, keepdims=True) - sm[0:1]
    gn = gn_ref[...]
    te = te_ref[...]

    def run(blk, states):
        return _hgrn_chunk(blk[:, 0:W], blk[:, W:2 * W], blk[:, 2 * W:3 * W], blk[:, 3 * W:4 * W], lb, gn, te,
                           states, seg)

    def body_long(i, carry):
        rs = pl.ds(pl.multiple_of(i * chunk, chunk), chunk)
        cols = [jnp.concatenate([p_ref[q, rs, j * W:(j + 1) * W] for q in range(sb)], axis=1) for j in range(4)]
        wide = lambda v: jnp.concatenate([v] * sb, axis=1)
        y, states = _hgrn_chunk(*cols, wide(lb), wide(gn), te,
                                [s_scr[q, h] for q in range(sb) for h in range(HG_H)], seg)
        for q in range(sb):
            for h in range(HG_H):
                s_scr[q, h] = states[q * HG_H + h]
            y_ref[q, rs, :] = y[:, q * W:(q + 1) * W]
        return carry

    def body_short(i, carry):
        s0 = pl.multiple_of(i * nseg, nseg)
        y, states = run(p_ref[pl.ds(s0, nseg)].reshape(chunk, 4 * W),
                        [s_scr[pl.ds(s0, nseg), h] for h in range(HG_H)])
        for h in range(HG_H):
            s_scr[pl.ds(s0, nseg), h] = states[h]
        y_ref[pl.ds(s0, nseg)] = y.reshape(nseg, seg, W)
        return carry

    if nseg == 1:
        lax.fori_loop(0, tt // chunk, body_long, 0)
    else:
        lax.fori_loop(0, sb * tt // chunk, body_short, 0)

    @pl.when(tstep == pl.num_programs(1) - 1)
    def _():
        for q in range(l):
            sout_ref[q] = prev_ref[q]

        def fin(i, carry):
            for h in range(HG_H):
                sout_ref[l, i, h] = s_scr[i, h].T
            return carry
        lax.fori_loop(0, sb, fin, 0)


def _rec_tile(bsz, t, rows, nlong):
    sb, tt = _tile(bsz, t, rows)
    return (min(nlong, bsz), tt) if t >= rows else (sb, tt)


def _hgrn_call(p, s_in, prev, hg_lower, gn, te, l, rows, nlong, chunk, seg):
    bsz, t, _ = p.shape
    sb, tt = _rec_tile(bsz, t, rows, nlong)
    zero_init = s_in is None
    in_spec, prev_spec, out_spec = _state_specs(l, sb, (HG_H, HG_D, HG_D))
    args = [p] + ([] if zero_init else [s_in]) + [hg_lower, gn, te] + ([prev] if l > 0 else [])
    specs = ([pl.BlockSpec((sb, tt, 4 * W), lambda i, j: (i, j, 0))] + ([] if zero_init else [in_spec])
             + [pl.BlockSpec(hg_lower.shape, lambda i, j: (0, 0)), pl.BlockSpec((1, W), lambda i, j: (0, 0)),
                pl.BlockSpec(te.shape, lambda i, j: (0, 0))]
             + ([prev_spec] if l > 0 else []))
    return pl.pallas_call(
        functools.partial(_hgrn_kernel, (l, chunk, seg, zero_init)),
        out_shape=(jax.ShapeDtypeStruct((bsz, t, W), F32),
                   jax.ShapeDtypeStruct((l + 1, bsz, HG_H, HG_D, HG_D), F32)),
        grid=(bsz // sb, t // tt),
        in_specs=specs,
        out_specs=(pl.BlockSpec((sb, tt, W), lambda i, j: (i, j, 0)), out_spec),
        scratch_shapes=[pltpu.VMEM((sb, HG_H, HG_D, HG_D), F32)],
        compiler_params=_cparams(("arbitrary", "arbitrary")),
        name="hgrn2",
    )(*args)


def _pick64(full, rowseg):
    nb = full.shape[1] // LANES
    half = lax.shift_right_logical(rowseg, 1)
    sel = full[:, 0:LANES]
    for j in range(1, nb):
        sel = jnp.where(half == j, full[:, j * LANES:(j + 1) * LANES], sel)
    sel = jnp.where(jnp.bitwise_and(rowseg, 1) == 1, pltpu.roll(sel, RW_D, axis=1), sel)
    return sel[:, 0:RW_D]


def _rwkv_recur(at, rt, bi, ki, bd, kd, v, states, pc, seg):
    c = at.shape[0]
    nseg = c // seg
    lg = int(math.log2(seg))
    rr = lax.broadcasted_iota(jnp.int32, (c, c), 0)
    cc = lax.broadcasted_iota(jnp.int32, (c, c), 1)
    if nseg == 1:
        strict = rr > cc
        incl = rr >= cc
    else:
        same = lax.shift_right_logical(rr, lg) == lax.shift_right_logical(cc, lg)
        strict = same & (rr > cc)
        incl = same & (rr >= cc)
        r2 = jnp.bitwise_and(lax.broadcasted_iota(jnp.int32, (2 * c, LANES), 0), c - 1)
        rowseg = lax.shift_right_logical(r2, lg)
        rb = jnp.bitwise_and(lax.broadcasted_iota(jnp.int32, (2 * c, nseg * RW_D), 0), c - 1)
        blockmask = lax.shift_right_logical(rb, lg) == _seg_id((2 * c, nseg * RW_D), 1, 6)
    heads = range(at.shape[1] // RW_D)
    sl = [slice(h * RW_D, (h + 1) * RW_D) for h in heads]
    ar = [jnp.concatenate([at[:, sl[h]], rt[:, sl[h]]], axis=0) for h in heads]
    bk = [jnp.concatenate([bi[:, sl[h]], ki[:, sl[h]]], axis=0) for h in heads]
    g = [_mm_nt(ar[h], bk[h]) for h in heads]
    if nseg == 1:
        a_s = [_mm_nt(ar[h], states[h]) for h in heads]
    else:
        a_s = [_pick64(_mm_nt(ar[h], states[h].reshape(nseg * RW_D, RW_D)), rowseg) for h in heads]
    vh = [v[:, sl[h]] for h in heads]
    lp = [jnp.where(strict, g[h][:c, :c], 0.0) for h in heads]
    x = [a_s[h][:c] + _mm(jnp.where(strict, g[h][:c, c:], 0.0), vh[h]) for h in heads]
    for j in range(lg):
        x = [x[h] + _mm(lp[h], x[h]) for h in heads]
        if j < lg - 1:
            lp = [_mm(lp[h], lp[h]) for h in heads]
    uv = [jnp.concatenate([x[h], vh[h]], axis=0) for h in heads]
    mrbk = [jnp.concatenate([jnp.where(incl, g[h][c:, :c], 0.0), jnp.where(incl, g[h][c:, c:], 0.0)], axis=1)
            for h in heads]
    ys = [a_s[h][c:] + _mm(mrbk[h], uv[h]) for h in heads]
    bkd = [jnp.concatenate([bd[:, sl[h]], kd[:, sl[h]]], axis=0) for h in heads]
    if nseg == 1:
        new_states = [states[h] * pc[:, sl[h]] + _mm_tn(uv[h], bkd[h]) for h in heads]
    else:
        new_states = []
        for h in heads:
            u2 = jnp.concatenate([uv[h], uv[h]], axis=1)
            uvexp = jnp.where(blockmask, jnp.concatenate([u2] * (nseg // 2), axis=1), 0.0)
            upd = _mm_tn(uvexp, bkd[h]).reshape(nseg, RW_D, RW_D)
            new_states.append(states[h] * pc[:, :, sl[h]] + upd)
    return ys, new_states


def _rwkv_kernel(cfg, *refs):
    l, chunk, seg, zero_init = cfg
    refs = list(refs)
    p_ref = refs.pop(0)
    s0_ref, sh0_ref = (None, None) if zero_init else (refs.pop(0), refs.pop(0))
    mu_ref, vec_ref, wup_ref, ones_ref, te_ref = [refs.pop(0) for _ in range(5)]
    sprev_ref, shprev_ref = (refs.pop(0), refs.pop(0)) if l > 0 else (None, None)
    y_ref, sout_ref, shout_ref = refs.pop(0), refs.pop(0), refs.pop(0)
    s_scr, prev_scr, at_s, rt_s, bi_s, ki_s, bd_s, kd_s, v_s, lpl_s, yr_s, bon_s, gate_s = refs
    sb, tt, _ = p_ref.shape
    rows = sb * tt
    nseg = chunk // seg
    tstep = pl.program_id(1)

    @pl.when(tstep == 0)
    def _():
        if zero_init:
            s_scr[...] = jnp.zeros_like(s_scr)
            prev_scr[...] = jnp.zeros_like(prev_scr)
        else:
            s_scr[...] = s0_ref[...]
            prev_scr[:, :, 0:RW_COLS] = sh0_ref[...]
            prev_scr[:, :, RW_COLS:RW_PAD] = jnp.zeros((sb, 1, RW_PAD - RW_COLS), F32)

    vec = vec_ref[...]
    w0, a0, k_k, k_a, r_k, ln_g, ln_b = [vec[i:i + 1, :] for i in range(7)]
    ones_bd = ones_ref[...]

    rw3 = p_ref[...]
    rw = rw3.reshape(rows, RW_PAD)
    prev_rows = jnp.broadcast_to(prev_scr[...], (sb, tt, RW_PAD)).reshape(rows, RW_PAD)
    prev = jnp.where(_local_row(rw.shape, tt) == 0, prev_rows, pltpu.roll(rw, 1, axis=0))
    prev_scr[...] = rw3[:, tt - 1:tt, :]
    rwm = rw + (prev - rw) * mu_ref[...]
    r, k, v = rwm[:, 0:W], rwm[:, W:2 * W], rwm[:, 2 * W:3 * W]
    lr = rwm[:, 3 * W:3 * W + 256]
    lane = lax.broadcasted_iota(jnp.int32, lr.shape, 1)
    act = jnp.where(lane < 64, jnp.tanh(lr), jnp.where(lane < 128, lr, _sigmoid(lr)))
    up = jnp.dot(_bf(act), wup_ref[...], preferred_element_type=F32)
    log_w = -RW_DECAY * _sigmoid(w0 + up[:, 0:W])
    a = _sigmoid(a0 + up[:, W:2 * W])
    gate_s[...] = up[:, 2 * W:3 * W]
    kk = k * k_k
    kk = kk / jnp.maximum(jnp.sqrt(_headsum(kk * kk, ones_bd)), 1e-12)
    k2 = k * (1.0 + (a - 1.0) * k_a)
    kka = kk * a
    bon_s[...] = _headsum(r * k2 * r_k, ones_bd) * v
    v_s[...] = v
    te = te_ref[...]
    scans = [_cumsum_last(log_w[i * chunk:(i + 1) * chunk], te) for i in range(rows // chunk)]
    logp = jnp.concatenate([s[0] for s in scans], axis=0) if len(scans) > 1 else scans[0][0]
    lpl = jnp.concatenate([s[1] for s in scans], axis=0) if len(scans) > 1 else scans[0][1]
    lpl_s[...] = lpl
    at_s[...] = -kk * jnp.exp(logp - log_w)
    rt_s[...] = r * jnp.exp(logp)
    einv = jnp.exp(-logp)
    bi_s[...] = kka * einv
    ki_s[...] = k2 * einv
    elast = jnp.exp(lpl - logp)
    bd_s[...] = kka * elast
    kd_s[...] = k2 * elast

    def run(rs, states, pc):
        return _rwkv_recur(at_s[rs, :], rt_s[rs, :], bi_s[rs, :], ki_s[rs, :], bd_s[rs, :], kd_s[rs, :],
                           v_s[rs, :], states, pc, seg)

    def body_long(ch, carry):
        rss = [pl.ds(pl.multiple_of(q * tt + ch * chunk, chunk), chunk) for q in range(sb)]
        wide = lambda ref: jnp.concatenate([ref[rs, :] for rs in rss], axis=1)
        pc = jnp.exp(jnp.concatenate([lpl_s[pl.ds(rs.start, 1), :] for rs in rss], axis=1))
        ys, states = _rwkv_recur(wide(at_s), wide(rt_s), wide(bi_s), wide(ki_s), wide(bd_s), wide(kd_s),
                                 wide(v_s), [s_scr[q, h] for q in range(sb) for h in range(RW_H)], pc, seg)
        for q in range(sb):
            for h in range(RW_H):
                s_scr[q, h] = states[q * RW_H + h]
            yr_s[rss[q], :] = jnp.concatenate(ys[q * RW_H:(q + 1) * RW_H], axis=1)
        return carry

    def body_short(ch, carry):
        rs = pl.ds(pl.multiple_of(ch * chunk, chunk), chunk)
        s0 = pl.multiple_of(ch * nseg, nseg)
        pc = jnp.exp(lpl_s[rs, :].reshape(nseg, seg, W)[:, 0:1, :])
        ys, states = run(rs, [s_scr[pl.ds(s0, nseg), h] for h in range(RW_H)], pc)
        for h in range(RW_H):
            s_scr[pl.ds(s0, nseg), h] = states[h]
        yr_s[rs, :] = jnp.concatenate(ys, axis=1)
        return carry

    if nseg == 1:
        lax.fori_loop(0, tt // chunk, body_long, 0)
    else:
        lax.fori_loop(0, rows // chunk, body_short, 0)

    y = yr_s[...]
    mean = _headsum(y, ones_bd) * (1.0 / RW_D)
    yc = y - mean
    var = _headsum(yc * yc, ones_bd) * (1.0 / RW_D)
    yn = yc * lax.rsqrt(var + RW_GN_EPS) * ln_g + ln_b
    y_ref[...] = ((yn + bon_s[...]) * gate_s[...]).reshape(sb, tt, W)

    @pl.when(tstep == pl.num_programs(1) - 1)
    def _():
        _emit_state(l, sout_ref, sprev_ref, s_scr[...])
        _emit_state(l, shout_ref, shprev_ref, prev_scr[:, :, 0:RW_COLS])


def _rwkv_call(p, s_in, sh_in, prev, mu, vec, wup, ones_bd, te, l, rows, nlong, chunk, seg):
    bsz, t, _ = p.shape
    sb, tt = _rec_tile(bsz, t, rows, nlong)
    zero_init = s_in is None
    s_specs = _state_specs(l, sb, (RW_H, RW_D, RW_D))
    sh_specs = _state_specs(l, sb, (1, RW_COLS))
    args = ([p] + ([] if zero_init else [s_in, sh_in]) + [mu, vec, wup, ones_bd, te]
            + (list(prev) if l > 0 else []))
    specs = ([pl.BlockSpec((sb, tt, RW_PAD), lambda i, j: (i, j, 1))]
             + ([] if zero_init else [s_specs[0], sh_specs[0]])
             + [pl.BlockSpec((1, RW_PAD), lambda i, j: (0, 0)),
                pl.BlockSpec((8, W), lambda i, j: (0, 0)),
                pl.BlockSpec((256, 3 * W), lambda i, j: (0, 0)),
                pl.BlockSpec((W, W), lambda i, j: (0, 0)),
                pl.BlockSpec(te.shape, lambda i, j: (0, 0))]
             + ([s_specs[1], sh_specs[1]] if l > 0 else []))
    tile_scr = [pltpu.VMEM((sb * tt, W), F32) for _ in range(11)]
    return pl.pallas_call(
        functools.partial(_rwkv_kernel, (l, chunk, seg, zero_init)),
        out_shape=(jax.ShapeDtypeStruct((bsz, t, W), F32),
                   jax.ShapeDtypeStruct((l + 1, bsz, RW_H, RW_D, RW_D), F32),
                   jax.ShapeDtypeStruct((l + 1, bsz, 1, RW_COLS), F32)),
        grid=(bsz // sb, t // tt),
        in_specs=specs,
        out_specs=(pl.BlockSpec((sb, tt, W), lambda i, j: (i, j, 0)), s_specs[2], sh_specs[2]),
        scratch_shapes=[pltpu.VMEM((sb, RW_H, RW_D, RW_D), F32), pltpu.VMEM((sb, 1, RW_PAD), F32)] + tile_scr,
        compiler_params=_cparams(("arbitrary", "arbitrary")),
        name="rwkv7",
    )(*args)


CF_PAD = 32
CF_RB = 64
SUBLANES = 8


def _conf_kernel(cfg, *refs):
    l, zero_init = cfg
    refs = list(refs)
    p_ref = refs.pop(0)
    s0_ref = None if zero_init else refs.pop(0)
    dw_ref, vec_ref = refs.pop(0), refs.pop(0)
    prev_ref = refs.pop(0) if l > 0 else None
    y_ref, sout_ref, ext, shifted = refs
    sb, tt, _ = p_ref.shape
    tstep = pl.program_id(1)
    off = CF_PAD - (CF_K - 1)
    span = tt + CF_PAD - SUBLANES

    @pl.when(tstep == 0)
    def _():
        if zero_init:
            ext[:, 0:CF_PAD, :] = jnp.zeros((sb, CF_PAD, W), F32)
        else:
            ext[:, off:CF_PAD, :] = s0_ref[...]

    vec = vec_ref[...]
    bias, ln_g, ln_b = vec[0:1], vec[1:2], vec[2:3]
    dw = dw_ref[...]
    rb = min(CF_RB, tt)

    nslot = shifted.shape[0]

    def one_seq(s, slot):
        blk = p_ref[s]
        ext[s, CF_PAD:CF_PAD + tt, :] = blk[:, 0:W] * _sigmoid(blk[:, W:2 * W])
        for q in range(1, SUBLANES):
            shifted[slot, q, 0:span, :] = ext[s, q:q + span, :]
        for r in range(tt // rb):
            acc = jnp.zeros((rb, W), F32) + bias
            for j in range(CF_K):
                a8, q = divmod(off + j, SUBLANES)
                lo = r * rb + a8 * SUBLANES
                tap = ext[s, lo:lo + rb, :] if q == 0 else shifted[slot, q, lo:lo + rb, :]
                acc = acc + dw[j:j + 1, :] * tap
            mean = jnp.mean(acc, axis=-1, keepdims=True)
            xc = acc - mean
            var = jnp.mean(xc * xc, axis=-1, keepdims=True)
            yn = xc * lax.rsqrt(var + 1e-5) * ln_g + ln_b
            y_ref[s, r * rb:(r + 1) * rb, :] = _silu(yn)
        tail = ext[s, tt:tt + CF_PAD, :]
        ext[s, 0:CF_PAD, :] = tail

    def body(i, carry):
        for slot in range(nslot):
            one_seq(i * nslot + slot, slot)
        return carry
    lax.fori_loop(0, sb // nslot, body, 0)

    @pl.when(tstep == pl.num_programs(1) - 1)
    def _():
        _emit_state(l, sout_ref, prev_ref, ext[:, off:CF_PAD, :])


def _conf_call(p, s_in, prev, dw, vec, l, rows):
    bsz, t, _ = p.shape
    sb, tt = _tile(bsz, t, rows)
    zero_init = s_in is None
    in_spec, prev_spec, out_spec = _state_specs(l, sb, (CF_K - 1, W))
    args = [p] + ([] if zero_init else [s_in]) + [dw, vec] + ([prev] if l > 0 else [])
    specs = ([pl.BlockSpec((sb, tt, 2 * W), lambda i, j: (i, j, 4))] + ([] if zero_init else [in_spec])
             + [pl.BlockSpec((32, W), lambda i, j: (0, 0)), pl.BlockSpec((8, W), lambda i, j: (0, 0))]
             + ([prev_spec] if l > 0 else []))
    return pl.pallas_call(
        functools.partial(_conf_kernel, (l, zero_init)),
        out_shape=(jax.ShapeDtypeStruct((bsz, t, W), F32),
                   jax.ShapeDtypeStruct((l + 1, bsz, CF_K - 1, W), F32)),
        grid=(bsz // sb, t // tt),
        in_specs=specs,
        out_specs=(pl.BlockSpec((sb, tt, W), lambda i, j: (i, j, 0)), out_spec),
        scratch_shapes=[pltpu.VMEM((sb, CF_PAD + tt, W), F32),
                        pltpu.VMEM((min(sb, SEQ_UNROLL), SUBLANES, CF_PAD + tt, W), F32)],
        compiler_params=_cparams(("arbitrary", "arbitrary")),
        name="conformer",
    )(*args)


LRU_PAD = 8


def _gelu_tanh(x):
    return 0.5 * x * (1.0 + jnp.tanh(0.7978845608028654 * (x + 0.044715 * (x * x * x))))


def _lru_kernel(cfg, *refs):
    l, zero_init = cfg
    refs = list(refs)
    p_ref = refs.pop(0)
    h0_ref, c0_ref = (None, None) if zero_init else (refs.pop(0), refs.pop(0))
    cw_ref, vec_ref, wax_ref = refs.pop(0), refs.pop(0), refs.pop(0)
    hprev_ref, cprev_ref = (refs.pop(0), refs.pop(0)) if l > 0 else (None, None)
    y_ref, hout_ref, cout_ref, ext, hcar = refs
    sb, tt, _ = p_ref.shape
    tstep = pl.program_id(1)
    off = LRU_PAD - (LRU_K - 1)

    @pl.when(tstep == 0)
    def _():
        if zero_init:
            ext[:, 0:LRU_PAD, :] = jnp.zeros((sb, LRU_PAD, W), F32)
            hcar[...] = jnp.zeros_like(hcar)
        else:
            ext[:, off:LRU_PAD, :] = c0_ref[...]
            hcar[...] = h0_ref[...]

    vec = vec_ref[...]
    cb, ba, bx, lam = vec[0:1], vec[1:2], vec[2:3], vec[3:4]
    sp = jnp.maximum(-lam, 0.0) + jnp.log1p(jnp.exp(-jnp.abs(lam)))
    cw = cw_ref[...]
    wax = wax_ref[...]

    def body(s, carry):
        blk = p_ref[s]
        ext[s, LRU_PAD:LRU_PAD + tt, :] = blk[:, 0:W]
        xc = jnp.zeros((tt, W), F32) + cb
        for j in range(LRU_K):
            xc = xc + cw[j:j + 1, :] * ext[s, pl.ds(off + j, tt), :]
        pre = jnp.dot(_bf(xc), wax, preferred_element_type=F32)
        rg = _sigmoid(pre[:, 0:W] + ba)
        ig = _sigmoid(pre[:, W:2 * W] + bx)
        log_a = -LRU_C * rg * sp
        a = jnp.exp(log_a)
        bv = jnp.sqrt(1.0 - a * a) * (ig * xc)
        d = 1
        while d < tt:
            a_s = _shift_rows(a, d, 1.0, tt)
            b_s = _shift_rows(bv, d, 0.0, tt)
            bv = a * b_s + bv
            a = a * a_s
            d *= 2
        hs = a * hcar[s] + bv
        hcar[s] = hs[tt - 1:tt, :]
        y_ref[s] = hs * _gelu_tanh(blk[:, W:2 * W])
        tail = ext[s, tt:tt + LRU_PAD, :]
        ext[s, 0:LRU_PAD, :] = tail
        return carry
    lax.fori_loop(0, sb, body, 0, unroll=min(sb, SEQ_UNROLL))

    @pl.when(tstep == pl.num_programs(1) - 1)
    def _():
        _emit_state(l, hout_ref, hprev_ref, hcar[...])
        _emit_state(l, cout_ref, cprev_ref, ext[:, off:LRU_PAD, :])


def _lru_call(p, h_in, c_in, prev, cw, vec, wax, l, rows):
    bsz, t, _ = p.shape
    sb, tt = _tile(bsz, t, rows)
    zero_init = h_in is None
    h_specs = _state_specs(l, sb, (1, W))
    c_specs = _state_specs(l, sb, (LRU_K - 1, W))
    args = [p] + ([] if zero_init else [h_in, c_in]) + [cw, vec, wax] + (list(prev) if l > 0 else [])
    specs = ([pl.BlockSpec((sb, tt, 2 * W), lambda i, j: (i, j, 5))]
             + ([] if zero_init else [h_specs[0], c_specs[0]])
             + [pl.BlockSpec((8, W), lambda i, j: (0, 0)), pl.BlockSpec((8, W), lambda i, j: (0, 0)),
                pl.BlockSpec((W, 2 * W), lambda i, j: (0, 0))]
             + ([h_specs[1], c_specs[1]] if l > 0 else []))
    return pl.pallas_call(
        functools.partial(_lru_kernel, (l, zero_init)),
        out_shape=(jax.ShapeDtypeStruct((bsz, t, W), F32),
                   jax.ShapeDtypeStruct((l + 1, bsz, 1, W), F32),
                   jax.ShapeDtypeStruct((l + 1, bsz, LRU_K - 1, W), F32)),
        grid=(bsz // sb, t // tt),
        in_specs=specs,
        out_specs=(pl.BlockSpec((sb, tt, W), lambda i, j: (i, j, 0)), h_specs[2], c_specs[2]),
        scratch_shapes=[pltpu.VMEM((sb, LRU_PAD + tt, W), F32), pltpu.VMEM((sb, 1, W), F32)],
        compiler_params=_cparams(("arbitrary", "arbitrary")),
        name="rglru",
    )(*args)


def _merge_kernel(yhg_ref, yrw_ref, ycf_ref, ylr_ref, x_ref, sc1_ref, sh1_ref, g1_ref, sc2_ref, sh2_ref,
                  nmix_ref, nmlp_ref, wg_ref, bg_ref, wb_ref, wo_ref, x1_ref, h2_ref):
    sb, tt, _ = x_ref.shape
    tm = sb * tt
    x = x_ref[...]
    h = _bf(_adaln(x, nmix_ref[...], sc1_ref[...], sh1_ref[...]).reshape(tm, D))
    z = jnp.zeros((tm, D), F32)
    for b, y_ref in enumerate((yhg_ref, yrw_ref, ycf_ref, ylr_ref)):
        bo = jnp.dot(_bf(y_ref[...].reshape(tm, W)), wb_ref[b], preferred_element_type=F32)
        logit = jnp.dot(h, wg_ref[:, b * D:(b + 1) * D], preferred_element_type=F32) + bg_ref[:, b * D:(b + 1) * D]
        z = z + _sigmoid(logit) * bo
    out = jnp.dot(_bf(z), wo_ref[...], preferred_element_type=F32).reshape(sb, tt, D)
    x1 = x + g1_ref[...] * out
    x1_ref[...] = x1
    h2_ref[...] = _bf(_adaln(x1, nmlp_ref[...], sc2_ref[...], sh2_ref[...]))


def _merge_call(ys, x, mods, nmix, nmlp, wg, bg, wb, wo, rows=512):
    bsz, t, _ = x.shape
    sb, tt = _tile(bsz, t, rows)
    tok = lambda width: pl.BlockSpec((sb, tt, width), lambda i, j: (i, j, 0))
    seq = pl.BlockSpec((sb, 1, D), lambda i, j: (i, 0, 0))
    const = lambda shape: pl.BlockSpec(shape, lambda i, j: (0,) * len(shape), pipeline_mode=pl.Buffered(1))
    return pl.pallas_call(
        _merge_kernel,
        out_shape=(jax.ShapeDtypeStruct((bsz, t, D), F32),
                   jax.ShapeDtypeStruct((bsz, t, D), BF16)),
        grid=(bsz // sb, t // tt),
        in_specs=[tok(W), tok(W), tok(W), tok(W), tok(D), seq, seq, seq, seq, seq,
                  const((1, D)), const((1, D)), const((D, 4 * D)), const((1, 4 * D)),
                  const((4, W, D)), const((D, D))],
        out_specs=(tok(D), tok(D)),
        compiler_params=_cparams(("arbitrary", "arbitrary")),
        name="merge",
    )(*ys, x, *mods, nmix, nmlp, wg, bg, wb, wo)


def _mlp_kernel(final, h2_ref, x1_ref, g2_ref, fg_ref, w1_ref, w2_ref, o_ref, acc):
    sb, tt, _ = x1_ref.shape
    k = pl.program_id(2)

    @pl.when(k == 0)
    def _():
        acc[...] = jnp.zeros_like(acc)

    hid = jnp.dot(h2_ref[...].reshape(sb * tt, D), w1_ref[...], preferred_element_type=F32)
    act = jnp.square(jnp.maximum(hid, 0.0))
    acc[...] += jnp.dot(_bf(act), w2_ref[...], preferred_element_type=F32)

    @pl.when(k == pl.num_programs(2) - 1)
    def _():
        x2 = x1_ref[...] + g2_ref[...] * acc[...].reshape(sb, tt, D)
        if final:
            ms = jnp.mean(x2 * x2, axis=-1, keepdims=True)
            x2 = x2 * lax.rsqrt(ms + EPS) * fg_ref[...]
        o_ref[...] = x2


def _mlp_call(h2, x1, g2, fg, w1, w2, final, rows=1024, th=1024):
    bsz, t, _ = x1.shape
    sb, tt = _tile(bsz, t, rows)
    tok = pl.BlockSpec((sb, tt, D), lambda i, j, k: (i, j, 0))
    return pl.pallas_call(
        functools.partial(_mlp_kernel, final),
        out_shape=jax.ShapeDtypeStruct((bsz, t, D), F32),
        grid=(bsz // sb, t // tt, HID // th),
        in_specs=[tok, tok,
                  pl.BlockSpec((sb, 1, D), lambda i, j, k: (i, 0, 0)),
                  pl.BlockSpec((1, D), lambda i, j, k: (0, 0)),
                  pl.BlockSpec((D, th), lambda i, j, k: (0, k)),
                  pl.BlockSpec((th, D), lambda i, j, k: (k, 0))],
        out_specs=tok,
        scratch_shapes=[pltpu.VMEM((sb * tt, D), F32)],
        compiler_params=_cparams(("arbitrary", "arbitrary", "arbitrary")),
        name="mlp",
    )(h2, x1, g2, fg, w1, w2)


def _block_diag(w):
    n, c, d = w.shape
    eye = jnp.eye(n, dtype=w.dtype)
    return (eye[:, None, :, None] * w[:, :, None, :]).reshape(n * c, n * d)


def _prep_layer(wt, l):
    w_in = wt["w_in"][l]
    zpad = jnp.zeros((D, RW_PAD - RW_COLS), F32)
    wcat = jnp.concatenate([w_in[:, 0:2048], w_in[:, 2048:2048 + RW_COLS], zpad, w_in[:, 2048 + RW_COLS:]], axis=1)
    wup = jnp.zeros((256, 3 * W), F32)
    wup = wup.at[0:64, 0:W].set(wt["rw_w_up"][l])
    wup = wup.at[64:128, W:2 * W].set(wt["rw_a_up"][l])
    wup = wup.at[128:256, 2 * W:3 * W].set(wt["rw_g_up"][l])
    zrow = jnp.zeros((W,), F32)
    rw_vec = jnp.stack([wt["rw_w0"][l], wt["rw_a0"][l], wt["rw_k_k"][l], wt["rw_k_a"][l], wt["rw_r_k"][l],
                        wt["rw_ln_g"][l], wt["rw_ln_b"][l], zrow])
    mu = jnp.concatenate([wt["rw_mu"][l], jnp.zeros((RW_PAD - RW_COLS,), F32)])[None, :]
    cf_vec = jnp.stack([wt["cf_dw_b"][l], wt["cf_ln_g"][l], wt["cf_ln_b"][l]] + [zrow] * 5)
    cf_dw = jnp.concatenate([wt["cf_dw"][l], jnp.zeros((1, W), F32)], axis=0)
    lru_vec = jnp.stack([wt["lru_conv_b"][l], wt["lru_ba"][l], wt["lru_bx"][l], wt["lru_lambda"][l]] + [zrow] * 4)
    lru_cw = jnp.concatenate([wt["lru_conv_w"][l], jnp.zeros((8 - LRU_K, W), F32)], axis=0)
    wax = jnp.concatenate([_block_diag(wt["lru_wa"][l]), _block_diag(wt["lru_wx"][l])], axis=1)
    return dict(
        wcat=_bf(wcat), wup=_bf(wup), rw_vec=rw_vec, mu=mu, cf_vec=cf_vec, cf_dw=cf_dw,
        lru_vec=lru_vec, lru_cw=lru_cw, wax=_bf(wax), wg=_bf(wt["w_gate"][l]), bg=wt["b_gate"][l][None, :],
        wb=_bf(wt["w_branch"][l]), wo=_bf(wt["w_out"][l]), w1=_bf(wt["w_mlp1"][l]), w2=_bf(wt["w_mlp2"][l]),
        hg_gn=wt["hg_norm_g"][l][None, :], nmix=wt["norm_mix_g"][l][None, :], nmlp=wt["norm_mlp_g"][l][None, :],
    )


def _trunk(x, c, states, wt, layers, ones_bd, mix_rows, rec_rows, nlong, chunk, seg):
    nl = len(layers)
    te = _scan_mats(chunk, seg)
    if states is None:
        s_hg = s_rw = s_shift = s_cf = s_lh = s_lc = None
    else:
        s_hg, s_rw, s_shift, s_cf, s_lh, s_lc = states
        s_shift = s_shift[:, :, None, :]
        s_lh = s_lh[:, :, None, :]
    mod = _mod_call(c, wt["ada_w"], wt["ada_b"])
    fg = wt["norm_final_g"][None, :]
    n_hg = n_rw = n_sh = n_cf = n_lh = n_lc = None
    for l, lw in enumerate(layers):
        sh1, sc1, g1, sh2, sc2, g2 = [mod[l, i][:, None, :] for i in range(6)]
        conv_in = None if states is None else (s_cf, s_lh, s_lc)
        p, y_cf, y_lr, n_cf, n_lh, n_lc = _inproj_call(
            x, sc1, sh1, lw["nmix"], lw["wcat"], conv_in, (n_cf, n_lh, n_lc), lw["cf_dw"], lw["cf_vec"],
            lw["lru_cw"], lw["lru_vec"], lw["wax"], l, mix_rows)
        y_hg, n_hg = _hgrn_call(p, s_hg, n_hg, wt["hg_lower"], lw["hg_gn"], te, l, rec_rows, nlong, chunk, seg)
        y_rw, n_rw, n_sh = _rwkv_call(p, s_rw, s_shift, (n_rw, n_sh), lw["mu"], lw["rw_vec"], lw["wup"], ones_bd,
                                      te, l, rec_rows, nlong, chunk, seg)
        x1, h2 = _merge_call((y_hg, y_rw, y_cf, y_lr), x, (sc1, sh1, g1, sc2, sh2), lw["nmix"], lw["nmlp"],
                             lw["wg"], lw["bg"], lw["wb"], lw["wo"])
        x = _mlp_call(h2, x1, g2, fg, lw["w1"], lw["w2"], final=(l == nl - 1))
    return x, [n_hg, n_rw, n_sh[:, :, 0, :], n_cf, n_lh[:, :, 0, :], n_lc]


def _run(x_prompt, x_sample, sample_states, c_prompt, c_sample, wt):
    nl = wt["w_in"].shape[0]
    layers = [_prep_layer(wt, l) for l in range(nl)]
    head = jnp.arange(W, dtype=jnp.int32) // RW_D
    ones_bd = _bf((head[:, None] == head[None, :]).astype(F32))
    tp = x_prompt.shape[1]
    ts = x_sample.shape[1]
    rows_p = min(256, tp)
    chunk_p = min(64, tp)
    y_p, st_p = _trunk(x_prompt, c_prompt, None, wt, layers, ones_bd, rows_p, rows_p // 2, 4, chunk_p, chunk_p)
    y_s, st_s = _trunk(x_sample, c_sample, sample_states, wt, layers, ones_bd, rows_p, 8 * ts, 1, 8 * ts, ts)
    return (y_p, y_s, *st_p, *st_s)


def kernel(x_prompt, x_sample, state_hgrn, state_rwkv, state_rwkv_shift, state_conv, state_lru_h, state_lru_conv, c_prompt, c_sample, ada_w, ada_b, norm_mix_g, norm_mlp_g, norm_final_g, w_in, hg_lower, hg_norm_g, rw_mu, rw_w0, rw_w_up, rw_a0, rw_a_up, rw_g_up, rw_k_k, rw_k_a, rw_r_k, rw_ln_g, rw_ln_b, cf_dw, cf_dw_b, cf_ln_g, cf_ln_b, lru_conv_w, lru_conv_b, lru_wa, lru_ba, lru_wx, lru_bx, lru_lambda, w_branch, w_gate, b_gate, w_out, w_mlp1, w_mlp2):
    wt = dict(ada_w=ada_w, ada_b=ada_b, norm_mix_g=norm_mix_g, norm_mlp_g=norm_mlp_g,
              norm_final_g=norm_final_g, w_in=w_in, hg_lower=hg_lower, hg_norm_g=hg_norm_g, rw_mu=rw_mu,
              rw_w0=rw_w0, rw_w_up=rw_w_up, rw_a0=rw_a0, rw_a_up=rw_a_up, rw_g_up=rw_g_up, rw_k_k=rw_k_k,
              rw_k_a=rw_k_a, rw_r_k=rw_r_k, rw_ln_g=rw_ln_g, rw_ln_b=rw_ln_b, cf_dw=cf_dw, cf_dw_b=cf_dw_b,
              cf_ln_g=cf_ln_g, cf_ln_b=cf_ln_b, lru_conv_w=lru_conv_w, lru_conv_b=lru_conv_b, lru_wa=lru_wa,
              lru_ba=lru_ba, lru_wx=lru_wx, lru_bx=lru_bx, lru_lambda=lru_lambda, w_branch=w_branch,
              w_gate=w_gate, b_gate=b_gate, w_out=w_out, w_mlp1=w_mlp1, w_mlp2=w_mlp2)
    sample_states = (state_hgrn, state_rwkv, state_rwkv_shift, state_conv, state_lru_h, state_lru_conv)
    return _run(x_prompt, x_sample, sample_states, c_prompt, c_sample, wt)
```

```python
import functools
import math

import jax
import jax.numpy as jnp
from jax import lax
from jax.experimental import pallas as pl
from jax.experimental.pallas import tpu as pltpu

D = 1024
W = 512
HG_H = 4
HG_D = 128
RW_H = 8
RW_D = 64
RW_COLS = 1792
RW_PAD = 2048
CF_K = 31
LRU_K = 4
HID = 4096
EPS = 1e-6
RW_GN_EPS = 64e-5
RW_DECAY = 0.606531
LRU_C = 8.0
LANES = 128

P_COLS = 6144
SEQ_UNROLL = 2
VMEM_LIMIT = 56 * 1024 * 1024

F32 = jnp.float32
BF16 = jnp.bfloat16


def _bf(x):
    return x.astype(BF16)


def _mm(a, b):
    return jnp.dot(_bf(a), _bf(b), preferred_element_type=F32)


def _mm_nt(a, b):
    return lax.dot_general(_bf(a), _bf(b), (((1,), (1,)), ((), ())), preferred_element_type=F32)


def _mm_tn(a, b):
    return lax.dot_general(_bf(a), _bf(b), (((0,), (0,)), ((), ())), preferred_element_type=F32)


def _sigmoid(x):
    return 1.0 / (1.0 + jnp.exp(-x))


def _silu(x):
    return x * _sigmoid(x)


def _local_row(shape, seg):
    return jnp.bitwise_and(lax.broadcasted_iota(jnp.int32, shape, 0), seg - 1)


def _shift_rows(x, d, fill, seg):
    return jnp.where(_local_row(x.shape, seg) >= d, pltpu.roll(x, d, axis=0), fill)


def _scan_mats(chunk, seg):
    r = jnp.arange(chunk, dtype=jnp.int32)
    same = (r[:, None] // seg) == (r[None, :] // seg)
    tri = same & (r[None, :] <= r[:, None])
    return _bf(jnp.concatenate([tri, same], axis=0).astype(F32))


def _cumsum_last(x, te):
    c = x.shape[0]
    p1 = _bf(x)
    r1 = x - p1.astype(F32)
    p2 = _bf(r1)
    p3 = _bf(r1 - p2.astype(F32))
    out = (jnp.dot(te, p1, preferred_element_type=F32) + jnp.dot(te, p2, preferred_element_type=F32)
           + jnp.dot(te, p3, preferred_element_type=F32))
    return out[:c], out[c:]


def _seg_id(shape, axis, log2_seg):
    return lax.shift_right_logical(lax.broadcasted_iota(jnp.int32, shape, axis), log2_seg)


def _headsum(x, ones_bd):
    hi = _bf(x)
    lo = _bf(x - hi.astype(F32))
    return (jnp.dot(hi, ones_bd, preferred_element_type=F32)
            + jnp.dot(lo, ones_bd, preferred_element_type=F32))


def _cparams(sem):
    return pltpu.CompilerParams(dimension_semantics=sem, vmem_limit_bytes=VMEM_LIMIT)


def _tile(bsz, t, rows):
    if t >= rows:
        return 1, rows
    return min(bsz, rows // t), t


def _state_specs(l, sb, tail):
    zeros = (0,) * len(tail)
    in_spec = pl.BlockSpec((None, sb) + tail, lambda i, j: (l, i) + zeros)
    prev_spec = pl.BlockSpec((l, sb) + tail, lambda i, j: (0, i) + zeros)
    out_spec = pl.BlockSpec((l + 1, sb) + tail, lambda i, j: (0, i) + zeros)
    return in_spec, prev_spec, out_spec


def _emit_state(l, out_ref, prev_ref, new):
    for q in range(l):
        out_ref[q] = prev_ref[q]
    out_ref[l] = new


def _mod_kernel(c_ref, w_ref, b_ref, o_ref):
    c = c_ref[...]
    o_ref[0, 0] = _mm(_silu(c), w_ref[0]) + b_ref[0, 0]


def _mod_call(c, ada_w, ada_b):
    nl = ada_w.shape[0]
    bsz = c.shape[0]
    return pl.pallas_call(
        _mod_kernel,
        out_shape=jax.ShapeDtypeStruct((nl, 6, bsz, D), F32),
        grid=(nl, 6),
        in_specs=[
            pl.BlockSpec((bsz, D), lambda l, j: (0, 0)),
            pl.BlockSpec((1, D, D), lambda l, j: (l, 0, j)),
            pl.BlockSpec((1, 1, 1, D), lambda l, j: (l, j, 0, 0)),
        ],
        out_specs=pl.BlockSpec((1, 1, bsz, D), lambda l, j: (l, j, 0, 0)),
        compiler_params=_cparams(("arbitrary", "arbitrary")),
        name="adaln_mod",
    )(c, ada_w, ada_b.reshape(nl, 6, 1, D))


def _adaln(x, g, sc, sh):
    ms = jnp.mean(x * x, axis=-1, keepdims=True)
    y = x * lax.rsqrt(ms + EPS) * g
    return y * (1.0 + sc) + sh


CF_PAD = 32
CF_RB = 64
SUBLANES = 8
LRU_PAD = 8
REC_COLS = 4096
REC_PIECE = 1024


def _gelu_tanh(x):
    return 0.5 * x * (1.0 + jnp.tanh(0.7978845608028654 * (x + 0.044715 * (x * x * x))))


def _conformer_prep(pcf, s, slot, ext, shifted):
    tt = pcf.shape[0]
    span = tt + CF_PAD - SUBLANES
    ext[s, CF_PAD:CF_PAD + tt, :] = pcf[:, 0:W] * _sigmoid(pcf[:, W:2 * W])
    for q in range(1, SUBLANES):
        shifted[slot, q, 0:span, :] = ext[s, q:q + span, :]


def _rows(start, size):
    if isinstance(start, int):
        return pl.ds(start, size)
    return pl.ds(pl.multiple_of(start, SUBLANES), size)


def _conformer_rows(r0, rb, s, slot, ext, shifted, y_ref, dw, bias, ln_g, ln_b):
    off = CF_PAD - (CF_K - 1)
    acc = jnp.zeros((rb, W), F32) + bias
    for j in range(CF_K):
        a8, q = divmod(off + j, SUBLANES)
        rows = _rows(r0 + a8 * SUBLANES, rb)
        tap = ext[s, rows, :] if q == 0 else shifted[slot, q, rows, :]
        acc = acc + dw[j:j + 1, :] * tap
    mean = jnp.mean(acc, axis=-1, keepdims=True)
    xc = acc - mean
    var = jnp.mean(xc * xc, axis=-1, keepdims=True)
    yn = xc * lax.rsqrt(var + 1e-5) * ln_g + ln_b
    y_ref[s, _rows(r0, rb), :] = _silu(yn)


def _linear_scan(a, b, h):
    n = a.shape[0]
    d = 1
    while d < SUBLANES:
        a_s = _shift_rows(a, d, 1.0, SUBLANES)
        b_s = _shift_rows(b, d, 0.0, SUBLANES)
        b = a * b_s + b
        a = a * a_s
        d *= 2
    out = []
    for g in range(n // SUBLANES):
        hs = a[g * SUBLANES:(g + 1) * SUBLANES] * h + b[g * SUBLANES:(g + 1) * SUBLANES]
        h = hs[SUBLANES - 1:SUBLANES, :]
        out.append(hs)
    return (jnp.concatenate(out, axis=0) if len(out) > 1 else out[0]), h


def _lru_conv(s, tt, ext, cw, cb):
    off = LRU_PAD - (LRU_K - 1)
    xc = jnp.zeros((tt, W), F32) + cb
    for j in range(LRU_K):
        xc = xc + cw[j:j + 1, :] * ext[s, off + j:off + j + tt, :]
    return xc


def _lru_rows(xc, gate, h, ba, bx, sp, wax):
    pre = jnp.dot(_bf(xc), wax, preferred_element_type=F32)
    rg = _sigmoid(pre[:, 0:W] + ba)
    ig = _sigmoid(pre[:, W:2 * W] + bx)
    a = jnp.exp(-LRU_C * rg * sp)
    hs, h = _linear_scan(a, jnp.sqrt(1.0 - a * a) * (ig * xc), h)
    return hs * _gelu_tanh(gate), h


def _inproj_kernel(cfg, *refs):
    l, zero_init = cfg
    refs = list(refs)
    x_ref, sc_ref, sh_ref, g_ref, w_ref = [refs.pop(0) for _ in range(5)]
    cf0_ref, lh0_ref, lc0_ref = (None,) * 3 if zero_init else [refs.pop(0) for _ in range(3)]
    dw_ref, cfv_ref, cw_ref, lrv_ref, wax_ref = [refs.pop(0) for _ in range(5)]
    cfp_ref, lhp_ref, lcp_ref = [refs.pop(0) for _ in range(3)] if l > 0 else (None,) * 3
    p_ref, ycf_ref, ylr_ref, cfo_ref, lho_ref, lco_ref = [refs.pop(0) for _ in range(6)]
    cf_ext, shifted, lr_ext, hcar, pc_scr, h_scr, xc_scr = refs
    sb, tt, _ = x_ref.shape
    tm = sb * tt
    tstep = pl.program_id(1)
    cf_off = CF_PAD - (CF_K - 1)
    lr_off = LRU_PAD - (LRU_K - 1)

    @pl.when(tstep == 0)
    def _():
        if zero_init:
            cf_ext[:, 0:CF_PAD, :] = jnp.zeros((sb, CF_PAD, W), F32)
            lr_ext[:, 0:LRU_PAD, :] = jnp.zeros((sb, LRU_PAD, W), F32)
            hcar[...] = jnp.zeros_like(hcar)
        else:
            cf_ext[:, cf_off:CF_PAD, :] = cf0_ref[...]
            lr_ext[:, lr_off:LRU_PAD, :] = lc0_ref[...]
            hcar[...] = lh0_ref[...]

    h_scr[...] = _bf(_adaln(x_ref[...], g_ref[...], sc_ref[...], sh_ref[...]).reshape(tm, D))
    pc_scr[...] = jnp.dot(h_scr[...], w_ref[:, REC_COLS:P_COLS], preferred_element_type=F32).reshape(sb, tt, 4 * W)

    def rec_piece(n):
        cols = slice(n * REC_PIECE, (n + 1) * REC_PIECE)
        p_ref[:, :, cols] = jnp.dot(h_scr[...], w_ref[:, cols], preferred_element_type=F32).reshape(sb, tt, REC_PIECE)

    cfv = cfv_ref[...]
    dw = dw_ref[...]
    lrv = lrv_ref[...]
    lam = lrv[3:4]
    sp = jnp.maximum(-lam, 0.0) + jnp.log1p(jnp.exp(-jnp.abs(lam)))
    cw = cw_ref[...]
    wax = wax_ref[...]
    nslot = shifted.shape[0]

    rb = min(CF_RB, tt)

    def prep(s, slot):
        _conformer_prep(pc_scr[s, :, 0:2 * W], s, slot, cf_ext, shifted)
        lr_ext[s, LRU_PAD:LRU_PAD + tt, :] = pc_scr[s, :, 2 * W:3 * W]
        xc_scr[s] = _lru_conv(s, tt, lr_ext, cw, lrv[0:1])

    def row_block(s, slot, r0, hstate):
        _conformer_rows(r0, rb, s, slot, cf_ext, shifted, ycf_ref, dw, cfv[0:1], cfv[1:2], cfv[2:3])
        rows = _rows(r0, rb)
        y, hstate = _lru_rows(xc_scr[s, rows, :], pc_scr[s, rows, 3 * W:4 * W], hstate, lrv[1:2], lrv[2:3], sp, wax)
        ylr_ref[s, rows, :] = y
        return hstate

    def finish(s, hstate):
        hcar[s] = hstate
        cf_tail = cf_ext[s, tt:tt + CF_PAD, :]
        cf_ext[s, 0:CF_PAD, :] = cf_tail
        lr_tail = lr_ext[s, tt:tt + LRU_PAD, :]
        lr_ext[s, 0:LRU_PAD, :] = lr_tail

    for n in range(REC_COLS // REC_PIECE):
        rec_piece(n)

    def body(i, carry):
        for slot in range(nslot):
            s = i * nslot + slot
            prep(s, slot)
            hstate = hcar[s]
            for r0 in range(0, tt, rb):
                hstate = row_block(s, slot, r0, hstate)
            finish(s, hstate)
        return carry
    lax.fori_loop(0, sb // nslot, body, 0)

    @pl.when(tstep == pl.num_programs(1) - 1)
    def _():
        _emit_state(l, cfo_ref, cfp_ref, cf_ext[:, cf_off:CF_PAD, :])
        _emit_state(l, lho_ref, lhp_ref, hcar[...])
        _emit_state(l, lco_ref, lcp_ref, lr_ext[:, lr_off:LRU_PAD, :])


def _inproj_call(x, sc, sh, g, wcat, conv_in, conv_prev, dw, cfv, cw, lrv, wax, l, rows=256):
    bsz, t, _ = x.shape
    sb, tt = _tile(bsz, t, rows)
    zero_init = conv_in is None
    tails = ((CF_K - 1, W), (1, W), (LRU_K - 1, W))
    sspecs = [_state_specs(l, sb, tail) for tail in tails]
    tok = lambda width: pl.BlockSpec((sb, tt, width), lambda i, j: (i, j, 0))
    seq = pl.BlockSpec((sb, 1, D), lambda i, j: (i, 0, 0))
    const = lambda shape: pl.BlockSpec(shape, lambda i, j: (0,) * len(shape), pipeline_mode=pl.Buffered(1))
    args = ([x, sc, sh, g, wcat] + ([] if zero_init else list(conv_in)) + [dw, cfv, cw, lrv, wax]
            + (list(conv_prev) if l > 0 else []))
    specs = ([tok(D), seq, seq, const((1, D)), const((D, P_COLS))]
             + ([] if zero_init else [sp[0] for sp in sspecs])
             + [const((32, W)), const((8, W)), const((8, W)), const((8, W)), const((W, 2 * W))]
             + ([sp[1] for sp in sspecs] if l > 0 else []))
    nslot = min(sb, SEQ_UNROLL)
    return pl.pallas_call(
        functools.partial(_inproj_kernel, (l, zero_init)),
        out_shape=(jax.ShapeDtypeStruct((bsz, t, REC_COLS), F32),
                   jax.ShapeDtypeStruct((bsz, t, W), F32),
                   jax.ShapeDtypeStruct((bsz, t, W), F32))
        + tuple(jax.ShapeDtypeStruct((l + 1, bsz) + tail, F32) for tail in tails),
        grid=(bsz // sb, t // tt),
        in_specs=specs,
        out_specs=(tok(REC_COLS), tok(W), tok(W)) + tuple(sp[2] for sp in sspecs),
        scratch_shapes=[pltpu.VMEM((sb, CF_PAD + tt, W), F32),
                        pltpu.VMEM((nslot, SUBLANES, CF_PAD + tt, W), F32),
                        pltpu.VMEM((sb, LRU_PAD + tt, W), F32),
                        pltpu.VMEM((sb, 1, W), F32),
                        pltpu.VMEM((sb, tt, 4 * W), F32),
                        pltpu.VMEM((sb * tt, D), BF16),
                        pltpu.VMEM((sb, tt, W), F32)],
        compiler_params=_cparams(("arbitrary", "arbitrary")),
        name="inproj",
    )(*args)


HG_SUB = 16


def _hgrn_chunk(qr, fz, iv, og, lb, gn, te, states, seg):
    c, width = qr.shape
    nseg = c // seg
    heads = range(width // HG_D)
    q = _silu(qr)
    f = lb + (1.0 - lb) * _sigmoid(fz)
    logf = jnp.log(f)
    kf = (1.0 - lb) * _sigmoid(-fz)
    b, b_last = _cumsum_last(logf, te)
    qe = q * jnp.exp(b)
    kdec = kf * jnp.exp(b_last - b)
    e_last = jnp.exp(b_last)
    sl = [slice(h * HG_D, (h + 1) * HG_D) for h in heads]

    if nseg == 1:
        sub = min(HG_SUB, c)
        nsub = c // sub
        o_inter = [_mm_nt(qe[:, sl[h]], states[h]) for h in heads]
        new_states = [states[h] * e_last[0:1, sl[h]] + _mm_tn(iv[:, sl[h]], kdec[:, sl[h]]) for h in heads]
        pieces = [[] for _ in heads]
        for i in range(nsub):
            r0 = i * sub
            m = b[r0 - 1:r0, :] if i > 0 else jnp.zeros((1, width), F32)
            qs = q[r0:r0 + sub] * jnp.exp(b[r0:r0 + sub] - m)
            kd = kf[r0:r0 + sub] * jnp.exp(jnp.minimum(m - b[r0:r0 + sub], 80.0))
            if i > 0:
                kall = jnp.concatenate([kf[:r0] * jnp.exp(m - b[:r0]), kd], axis=0)
            else:
                kall = kd
            row = lax.broadcasted_iota(jnp.int32, (sub, r0 + sub), 0) + r0
            col = lax.broadcasted_iota(jnp.int32, (sub, r0 + sub), 1)
            causal = col <= row
            scs = [jnp.where(causal, _mm_nt(qs[:, sl[h]], kall[:, sl[h]]), 0.0) for h in heads]
            for h in heads:
                pieces[h].append(_mm(scs[h], iv[:r0 + sub, sl[h]]))
        o_intra = [jnp.concatenate(pieces[h], axis=0) if nsub > 1 else pieces[h][0] for h in heads]
    else:
        lg = int(math.log2(seg))
        kd = kf * jnp.exp(jnp.minimum(-b, 80.0))
        rr = lax.broadcasted_iota(jnp.int32, (c, c), 0)
        cc = lax.broadcasted_iota(jnp.int32, (c, c), 1)
        causal = (lax.shift_right_logical(rr, lg) == lax.shift_right_logical(cc, lg)) & (cc <= rr)
        rowseg = _seg_id((c, HG_D), 0, lg)
        blockmask = _seg_id((c, nseg * HG_D), 0, lg) == _seg_id((c, nseg * HG_D), 1, 7)
        e3 = e_last.reshape(nseg, seg, width)[:, 0:1, :]
        full = [_mm_nt(qe[:, sl[h]], states[h].reshape(nseg * HG_D, HG_D)) for h in heads]
        o_inter = []
        for h in heads:
            o = full[h][:, 0:HG_D]
            for s in range(1, nseg):
                o = jnp.where(rowseg == s, full[h][:, s * HG_D:(s + 1) * HG_D], o)
            o_inter.append(o)
        scs = [jnp.where(causal, _mm_nt(qe[:, sl[h]], kd[:, sl[h]]), 0.0) for h in heads]
        o_intra = [_mm(scs[h], iv[:, sl[h]]) for h in heads]
        new_states = []
        for h in heads:
            ivexp = jnp.where(blockmask, jnp.concatenate([iv[:, sl[h]]] * nseg, axis=1), 0.0)
            upd = _mm_tn(ivexp, kdec[:, sl[h]]).reshape(nseg, HG_D, HG_D)
            new_states.append(states[h] * e3[:, :, sl[h]] + upd)
    outs = []
    for h in heads:
        o = o_inter[h] + o_intra[h]
        outs.append(o * lax.rsqrt(jnp.mean(o * o, axis=-1, keepdims=True) + EPS))
    y = jnp.concatenate(outs, axis=1) * gn * _silu(og)
    return y, new_states


def _hgrn_kernel(cfg, *refs):
    l, chunk, seg, zero_init = cfg
    refs = list(refs)
    p_ref = refs.pop(0)
    s0_ref = None if zero_init else refs.pop(0)
    lower_ref, gn_ref, te_ref = refs.pop(0), refs.pop(0), refs.pop(0)
    prev_ref = refs.pop(0) if l > 0 else None
    y_ref, sout_ref, s_scr = refs
    sb, tt, _ = p_ref.shape
    nseg = chunk // seg
    tstep = pl.program_id(1)

    @pl.when(tstep == 0)
    def _():
        if zero_init:
            s_scr[...] = jnp.zeros_like(s_scr)
        else:
            def init(i, carry):
                for h in range(HG_H):
                    s_scr[i, h] = s0_ref[i, h].T
                return carry
            lax.fori_loop(0, sb, init, 0)

    low = lower_ref[...]
    e = jnp.exp(low - jnp.max(low, axis=0, keepdims=True))
    sm = e / jnp.sum(e, axis=0, keepdims=True)
    lb = jnp.sum(sm[:l + 1], axis=0, keepdims=True) - sm[0:1]
    gn = gn_ref[...]
    te = te_ref[...]

    def run(blk, states):
        return _hgrn_chunk(blk[:, 0:W], blk[:, W:2 * W], blk[:, 2 * W:3 * W], blk[:, 3 * W:4 * W], lb, gn, te,
                           states, seg)

    def body_long(i, carry):
        rs = pl.ds(pl.multiple_of(i * chunk, chunk), chunk)
        cols = [jnp.concatenate([p_ref[q, rs, j * W:(j + 1) * W] for q in range(sb)], axis=1) for j in range(4)]
        wide = lambda v: jnp.concatenate([v] * sb, axis=1)
        y, states = _hgrn_chunk(*cols, wide(lb), wide(gn), te,
                                [s_scr[q, h] for q in range(sb) for h in range(HG_H)], seg)
        for q in range(sb):
            for h in range(HG_H):
                s_scr[q, h] = states[q * HG_H + h]
            y_ref[q, rs, :] = y[:, q * W:(q + 1) * W]
        return carry

    def body_short(i, carry):
        s0 = pl.multiple_of(i * nseg, nseg)
        y, states = run(p_ref[pl.ds(s0, nseg)].reshape(chunk, 4 * W),
                        [s_scr[pl.ds(s0, nseg), h] for h in range(HG_H)])
        for h in range(HG_H):
            s_scr[pl.ds(s0, nseg), h] = states[h]
        y_ref[pl.ds(s0, nseg)] = y.reshape(nseg, seg, W)
        return carry

    if nseg == 1:
        lax.fori_loop(0, tt // chunk, body_long, 0)
    else:
        lax.fori_loop(0, sb * tt // chunk, body_short, 0)

    @pl.when(tstep == pl.num_programs(1) - 1)
    def _():
        for q in range(l):
            sout_ref[q] = prev_ref[q]

        def fin(i, carry):
            for h in range(HG_H):
                sout_ref[l, i, h] = s_scr[i, h].T
            return carry
        lax.fori_loop(0, sb, fin, 0)


def _rec_tile(bsz, t, rows, nlong):
    sb, tt = _tile(bsz, t, rows)
    return (min(nlong, bsz), tt) if t >= rows else (sb, tt)


def _hgrn_call(p, s_in, prev, hg_lower, gn, te, l, rows, nlong, chunk, seg):
    bsz, t, _ = p.shape
    sb, tt = _rec_tile(bsz, t, rows, nlong)
    zero_init = s_in is None
    in_spec, prev_spec, out_spec = _state_specs(l, sb, (HG_H, HG_D, HG_D))
    args = [p] + ([] if zero_init else [s_in]) + [hg_lower, gn, te] + ([prev] if l > 0 else [])
    specs = ([pl.BlockSpec((sb, tt, 4 * W), lambda i, j: (i, j, 0))] + ([] if zero_init else [in_spec])
             + [pl.BlockSpec(hg_lower.shape, lambda i, j: (0, 0)), pl.BlockSpec((1, W), lambda i, j: (0, 0)),
                pl.BlockSpec(te.shape, lambda i, j: (0, 0))]
             + ([prev_spec] if l > 0 else []))
    return pl.pallas_call(
        functools.partial(_hgrn_kernel, (l, chunk, seg, zero_init)),
        out_shape=(jax.ShapeDtypeStruct((bsz, t, W), F32),
                   jax.ShapeDtypeStruct((l + 1, bsz, HG_H, HG_D, HG_D), F32)),
        grid=(bsz // sb, t // tt),
        in_specs=specs,
        out_specs=(pl.BlockSpec((sb, tt, W), lambda i, j: (i, j, 0)), out_spec),
        scratch_shapes=[pltpu.VMEM((sb, HG_H, HG_D, HG_D), F32)],
        compiler_params=_cparams(("arbitrary", "arbitrary")),
        name="hgrn2",
    )(*args)


def _pick64(full, rowseg):
    nb = full.shape[1] // LANES
    half = lax.shift_right_logical(rowseg, 1)
    sel = full[:, 0:LANES]
    for j in range(1, nb):
        sel = jnp.where(half == j, full[:, j * LANES:(j + 1) * LANES], sel)
    sel = jnp.where(jnp.bitwise_and(rowseg, 1) == 1, pltpu.roll(sel, RW_D, axis=1), sel)
    return sel[:, 0:RW_D]


def _rwkv_recur(at, rt, bi, ki, bd, kd, v, states, pc, seg):
    c = at.shape[0]
    nseg = c // seg
    lg = int(math.log2(seg))
    rr = lax.broadcasted_iota(jnp.int32, (c, c), 0)
    cc = lax.broadcasted_iota(jnp.int32, (c, c), 1)
    if nseg == 1:
        strict = rr > cc
        incl = rr >= cc
    else:
        same = lax.shift_right_logical(rr, lg) == lax.shift_right_logical(cc, lg)
        strict = same & (rr > cc)
        incl = same & (rr >= cc)
        r2 = jnp.bitwise_and(lax.broadcasted_iota(jnp.int32, (2 * c, LANES), 0), c - 1)
        rowseg = lax.shift_right_logical(r2, lg)
        rb = jnp.bitwise_and(lax.broadcasted_iota(jnp.int32, (2 * c, nseg * RW_D), 0), c - 1)
        blockmask = lax.shift_right_logical(rb, lg) == _seg_id((2 * c, nseg * RW_D), 1, 6)
    heads = range(at.shape[1] // RW_D)
    sl = [slice(h * RW_D, (h + 1) * RW_D) for h in heads]
    ar = [jnp.concatenate([at[:, sl[h]], rt[:, sl[h]]], axis=0) for h in heads]
    bk = [jnp.concatenate([bi[:, sl[h]], ki[:, sl[h]]], axis=0) for h in heads]
    g = [_mm_nt(ar[h], bk[h]) for h in heads]
    if nseg == 1:
        a_s = [_mm_nt(ar[h], states[h]) for h in heads]
    else:
        a_s = [_pick64(_mm_nt(ar[h], states[h].reshape(nseg * RW_D, RW_D)), rowseg) for h in heads]
    vh = [v[:, sl[h]] for h in heads]
    lp = [jnp.where(strict, g[h][:c, :c], 0.0) for h in heads]
    x = [a_s[h][:c] + _mm(jnp.where(strict, g[h][:c, c:], 0.0), vh[h]) for h in heads]
    for j in range(lg):
        x = [x[h] + _mm(lp[h], x[h]) for h in heads]
        if j < lg - 1:
            lp = [_mm(lp[h], lp[h]) for h in heads]
    uv = [jnp.concatenate([x[h], vh[h]], axis=0) for h in heads]
    mrbk = [jnp.concatenate([jnp.where(incl, g[h][c:, :c], 0.0), jnp.where(incl, g[h][c:, c:], 0.0)], axis=1)
            for h in heads]
    ys = [a_s[h][c:] + _mm(mrbk[h], uv[h]) for h in heads]
    bkd = [jnp.concatenate([bd[:, sl[h]], kd[:, sl[h]]], axis=0) for h in heads]
    if nseg == 1:
        new_states = [states[h] * pc[:, sl[h]] + _mm_tn(uv[h], bkd[h]) for h in heads]
    else:
        new_states = []
        for h in heads:
            u2 = jnp.concatenate([uv[h], uv[h]], axis=1)
            uvexp = jnp.where(blockmask, jnp.concatenate([u2] * (nseg // 2), axis=1), 0.0)
            upd = _mm_tn(uvexp, bkd[h]).reshape(nseg, RW_D, RW_D)
            new_states.append(states[h] * pc[:, :, sl[h]] + upd)
    return ys, new_states


def _rwkv_kernel(cfg, *refs):
    l, chunk, seg, zero_init = cfg
    refs = list(refs)
    p_ref = refs.pop(0)
    s0_ref, sh0_ref = (None, None) if zero_init else (refs.pop(0), refs.pop(0))
    mu_ref, vec_ref, wup_ref, ones_ref, te_ref = [refs.pop(0) for _ in range(5)]
    sprev_ref, shprev_ref = (refs.pop(0), refs.pop(0)) if l > 0 else (None, None)
    y_ref, sout_ref, shout_ref = refs.pop(0), refs.pop(0), refs.pop(0)
    s_scr, prev_scr, at_s, rt_s, bi_s, ki_s, bd_s, kd_s, v_s, lpl_s, yr_s, bon_s, gate_s = refs
    sb, tt, _ = p_ref.shape
    rows = sb * tt
    nseg = chunk // seg
    tstep = pl.program_id(1)

    @pl.when(tstep == 0)
    def _():
        if zero_init:
            s_scr[...] = jnp.zeros_like(s_scr)
            prev_scr[...] = jnp.zeros_like(prev_scr)
        else:
            s_scr[...] = s0_ref[...]
            prev_scr[:, :, 0:RW_COLS] = sh0_ref[...]
            prev_scr[:, :, RW_COLS:RW_PAD] = jnp.zeros((sb, 1, RW_PAD - RW_COLS), F32)

    vec = vec_ref[...]
    w0, a0, k_k, k_a, r_k, ln_g, ln_b = [vec[i:i + 1, :] for i in range(7)]
    ones_bd = ones_ref[...]

    rw3 = p_ref[...]
    rw = rw3.reshape(rows, RW_PAD)
    prev_rows = jnp.broadcast_to(prev_scr[...], (sb, tt, RW_PAD)).reshape(rows, RW_PAD)
    prev = jnp.where(_local_row(rw.shape, tt) == 0, prev_rows, pltpu.roll(rw, 1, axis=0))
    prev_scr[...] = rw3[:, tt - 1:tt, :]
    rwm = rw + (prev - rw) * mu_ref[...]
    r, k, v = rwm[:, 0:W], rwm[:, W:2 * W], rwm[:, 2 * W:3 * W]
    lr = rwm[:, 3 * W:3 * W + 256]
    lane = lax.broadcasted_iota(jnp.int32, lr.shape, 1)
    act = jnp.where(lane < 64, jnp.tanh(lr), jnp.where(lane < 128, lr, _sigmoid(lr)))
    up = jnp.dot(_bf(act), wup_ref[...], preferred_element_type=F32)
    log_w = -RW_DECAY * _sigmoid(w0 + up[:, 0:W])
    a = _sigmoid(a0 + up[:, W:2 * W])
    gate_s[...] = up[:, 2 * W:3 * W]
    kk = k * k_k
    kk = kk / jnp.maximum(jnp.sqrt(_headsum(kk * kk, ones_bd)), 1e-12)
    k2 = k * (1.0 + (a - 1.0) * k_a)
    kka = kk * a
    bon_s[...] = _headsum(r * k2 * r_k, ones_bd) * v
    v_s[...] = v
    te = te_ref[...]
    scans = [_cumsum_last(log_w[i * chunk:(i + 1) * chunk], te) for i in range(rows // chunk)]
    logp = jnp.concatenate([s[0] for s in scans], axis=0) if len(scans) > 1 else scans[0][0]
    lpl = jnp.concatenate([s[1] for s in scans], axis=0) if len(scans) > 1 else scans[0][1]
    lpl_s[...] = lpl
    at_s[...] = -kk * jnp.exp(logp - log_w)
    rt_s[...] = r * jnp.exp(logp)
    einv = jnp.exp(-logp)
    bi_s[...] = kka * einv
    ki_s[...] = k2 * einv
    elast = jnp.exp(lpl - logp)
    bd_s[...] = kka * elast
    kd_s[...] = k2 * elast

    def run(rs, states, pc):
        return _rwkv_recur(at_s[rs, :], rt_s[rs, :], bi_s[rs, :], ki_s[rs, :], bd_s[rs, :], kd_s[rs, :],
                           v_s[rs, :], states, pc, seg)

    def body_long(ch, carry):
        rss = [pl.ds(pl.multiple_of(q * tt + ch * chunk, chunk), chunk) for q in range(sb)]
        wide = lambda ref: jnp.concatenate([ref[rs, :] for rs in rss], axis=1)
        pc = jnp.exp(jnp.concatenate([lpl_s[pl.ds(rs.start, 1), :] for rs in rss], axis=1))
        ys, states = _rwkv_recur(wide(at_s), wide(rt_s), wide(bi_s), wide(ki_s), wide(bd_s), wide(kd_s),
                                 wide(v_s), [s_scr[q, h] for q in range(sb) for h in range(RW_H)], pc, seg)
        for q in range(sb):
            for h in range(RW_H):
                s_scr[q, h] = states[q * RW_H + h]
            yr_s[rss[q], :] = jnp.concatenate(ys[q * RW_H:(q + 1) * RW_H], axis=1)
        return carry

    def body_short(ch, carry):
        rs = pl.ds(pl.multiple_of(ch * chunk, chunk), chunk)
        s0 = pl.multiple_of(ch * nseg, nseg)
        pc = jnp.exp(lpl_s[rs, :].reshape(nseg, seg, W)[:, 0:1, :])
        ys, states = run(rs, [s_scr[pl.ds(s0, nseg), h] for h in range(RW_H)], pc)
        for h in range(RW_H):
            s_scr[pl.ds(s0, nseg), h] = states[h]
        yr_s[rs, :] = jnp.concatenate(ys, axis=1)
        return carry

    if nseg == 1:
        lax.fori_loop(0, tt // chunk, body_long, 0)
    else:
        lax.fori_loop(0, rows // chunk, body_short, 0)

    y = yr_s[...]
    mean = _headsum(y, ones_bd) * (1.0 / RW_D)
    yc = y - mean
    var = _headsum(yc * yc, ones_bd) * (1.0 / RW_D)
    yn = yc * lax.rsqrt(var + RW_GN_EPS) * ln_g + ln_b
    y_ref[...] = ((yn + bon_s[...]) * gate_s[...]).reshape(sb, tt, W)

    @pl.when(tstep == pl.num_programs(1) - 1)
    def _():
        _emit_state(l, sout_ref, sprev_ref, s_scr[...])
        _emit_state(l, shout_ref, shprev_ref, prev_scr[:, :, 0:RW_COLS])


def _rwkv_call(p, s_in, sh_in, prev, mu, vec, wup, ones_bd, te, l, rows, nlong, chunk, seg):
    bsz, t, _ = p.shape
    sb, tt = _rec_tile(bsz, t, rows, nlong)
    zero_init = s_in is None
    s_specs = _state_specs(l, sb, (RW_H, RW_D, RW_D))
    sh_specs = _state_specs(l, sb, (1, RW_COLS))
    args = ([p] + ([] if zero_init else [s_in, sh_in]) + [mu, vec, wup, ones_bd, te]
            + (list(prev) if l > 0 else []))
    specs = ([pl.BlockSpec((sb, tt, RW_PAD), lambda i, j: (i, j, 1))]
             + ([] if zero_init else [s_specs[0], sh_specs[0]])
             + [pl.BlockSpec((1, RW_PAD), lambda i, j: (0, 0)),
                pl.BlockSpec((8, W), lambda i, j: (0, 0)),
                pl.BlockSpec((256, 3 * W), lambda i, j: (0, 0)),
                pl.BlockSpec((W, W), lambda i, j: (0, 0)),
                pl.BlockSpec(te.shape, lambda i, j: (0, 0))]
             + ([s_specs[1], sh_specs[1]] if l > 0 else []))
    tile_scr = [pltpu.VMEM((sb * tt, W), F32) for _ in range(11)]
    return pl.pallas_call(
        functools.partial(_rwkv_kernel, (l, chunk, seg, zero_init)),
        out_shape=(jax.ShapeDtypeStruct((bsz, t, W), F32),
                   jax.ShapeDtypeStruct((l + 1, bsz, RW_H, RW_D, RW_D), F32),
                   jax.ShapeDtypeStruct((l + 1, bsz, 1, RW_COLS), F32)),
        grid=(bsz // sb, t // tt),
        in_specs=specs,
        out_specs=(pl.BlockSpec((sb, tt, W), lambda i, j: (i, j, 0)), s_specs[2], sh_specs[2]),
        scratch_shapes=[pltpu.VMEM((sb, RW_H, RW_D, RW_D), F32), pltpu.VMEM((sb, 1, RW_PAD), F32)] + tile_scr,
        compiler_params=_cparams(("arbitrary", "arbitrary")),
        name="rwkv7",
    )(*args)


def _merge_kernel(yhg_ref, yrw_ref, ycf_ref, ylr_ref, x_ref, sc1_ref, sh1_ref, g1_ref, sc2_ref, sh2_ref,
                  nmix_ref, nmlp_ref, wg_ref, bg_ref, wb_ref, wo_ref, x1_ref, h2_ref):
    sb, tt, _ = x_ref.shape
    tm = sb * tt
    x = x_ref[...]
    h = _bf(_adaln(x, nmix_ref[...], sc1_ref[...], sh1_ref[...]).reshape(tm, D))
    z = jnp.zeros((tm, D), F32)
    for b, y_ref in enumerate((yhg_ref, yrw_ref, ycf_ref, ylr_ref)):
        bo = jnp.dot(_bf(y_ref[...].reshape(tm, W)), wb_ref[b], preferred_element_type=F32)
        logit = jnp.dot(h, wg_ref[:, b * D:(b + 1) * D], preferred_element_type=F32) + bg_ref[:, b * D:(b + 1) * D]
        z = z + _sigmoid(logit) * bo
    out = jnp.dot(_bf(z), wo_ref[...], preferred_element_type=F32).reshape(sb, tt, D)
    x1 = x + g1_ref[...] * out
    x1_ref[...] = x1
    h2_ref[...] = _bf(_adaln(x1, nmlp_ref[...], sc2_ref[...], sh2_ref[...]))


def _merge_call(ys, x, mods, nmix, nmlp, wg, bg, wb, wo, rows=512):
    bsz, t, _ = x.shape
    sb, tt = _tile(bsz, t, rows)
    tok = lambda width: pl.BlockSpec((sb, tt, width), lambda i, j: (i, j, 0))
    seq = pl.BlockSpec((sb, 1, D), lambda i, j: (i, 0, 0))
    const = lambda shape: pl.BlockSpec(shape, lambda i, j: (0,) * len(shape), pipeline_mode=pl.Buffered(1))
    return pl.pallas_call(
        _merge_kernel,
        out_shape=(jax.ShapeDtypeStruct((bsz, t, D), F32),
                   jax.ShapeDtypeStruct((bsz, t, D), BF16)),
        grid=(bsz // sb, t // tt),
        in_specs=[tok(W), tok(W), tok(W), tok(W), tok(D), seq, seq, seq, seq, seq,
                  const((1, D)), const((1, D)), const((D, 4 * D)), const((1, 4 * D)),
                  const((4, W, D)), const((D, D))],
        out_specs=(tok(D), tok(D)),
        compiler_params=_cparams(("arbitrary", "arbitrary")),
        name="merge",
    )(*ys, x, *mods, nmix, nmlp, wg, bg, wb, wo)


def _mlp_kernel(final, h2_ref, x1_ref, g2_ref, fg_ref, w1_ref, w2_ref, o_ref, acc):
    sb, tt, _ = x1_ref.shape
    k = pl.program_id(2)

    @pl.when(k == 0)
    def _():
        acc[...] = jnp.zeros_like(acc)

    hid = jnp.dot(h2_ref[...].reshape(sb * tt, D), w1_ref[...], preferred_element_type=F32)
    act = jnp.square(jnp.maximum(hid, 0.0))
    acc[...] += jnp.dot(_bf(act), w2_ref[...], preferred_element_type=F32)

    @pl.when(k == pl.num_programs(2) - 1)
    def _():
        x2 = x1_ref[...] + g2_ref[...] * acc[...].reshape(sb, tt, D)
        if final:
            ms = jnp.mean(x2 * x2, axis=-1, keepdims=True)
            x2 = x2 * lax.rsqrt(ms + EPS) * fg_ref[...]
        o_ref[...] = x2


def _mlp_call(h2, x1, g2, fg, w1, w2, final, rows=1024, th=1024):
    bsz, t, _ = x1.shape
    sb, tt = _tile(bsz, t, rows)
    tok = pl.BlockSpec((sb, tt, D), lambda i, j, k: (i, j, 0))
    return pl.pallas_call(
        functools.partial(_mlp_kernel, final),
        out_shape=jax.ShapeDtypeStruct((bsz, t, D), F32),
        grid=(bsz // sb, t // tt, HID // th),
        in_specs=[tok, tok,
                  pl.BlockSpec((sb, 1, D), lambda i, j, k: (i, 0, 0)),
                  pl.BlockSpec((1, D), lambda i, j, k: (0, 0)),
                  pl.BlockSpec((D, th), lambda i, j, k: (0, k)),
                  pl.BlockSpec((th, D), lambda i, j, k: (k, 0))],
        out_specs=tok,
        scratch_shapes=[pltpu.VMEM((sb * tt, D), F32)],
        compiler_params=_cparams(("arbitrary", "arbitrary", "arbitrary")),
        name="mlp",
    )(h2, x1, g2, fg, w1, w2)


def _block_diag(w):
    n, c, d = w.shape
    eye = jnp.eye(n, dtype=w.dtype)
    return (eye[:, None, :, None] * w[:, :, None, :]).reshape(n * c, n * d)


def _prep_layer(wt, l):
    w_in = wt["w_in"][l]
    zpad = jnp.zeros((D, RW_PAD - RW_COLS), F32)
    wcat = jnp.concatenate([w_in[:, 0:2048], w_in[:, 2048:2048 + RW_COLS], zpad, w_in[:, 2048 + RW_COLS:]], axis=1)
    wup = jnp.zeros((256, 3 * W), F32)
    wup = wup.at[0:64, 0:W].set(wt["rw_w_up"][l])
    wup = wup.at[64:128, W:2 * W].set(wt["rw_a_up"][l])
    wup = wup.at[128:256, 2 * W:3 * W].set(wt["rw_g_up"][l])
    zrow = jnp.zeros((W,), F32)
    rw_vec = jnp.stack([wt["rw_w0"][l], wt["rw_a0"][l], wt["rw_k_k"][l], wt["rw_k_a"][l], wt["rw_r_k"][l],
                        wt["rw_ln_g"][l], wt["rw_ln_b"][l], zrow])
    mu = jnp.concatenate([wt["rw_mu"][l], jnp.zeros((RW_PAD - RW_COLS,), F32)])[None, :]
    cf_vec = jnp.stack([wt["cf_dw_b"][l], wt["cf_ln_g"][l], wt["cf_ln_b"][l]] + [zrow] * 5)
    cf_dw = jnp.concatenate([wt["cf_dw"][l], jnp.zeros((1, W), F32)], axis=0)
    lru_vec = jnp.stack([wt["lru_conv_b"][l], wt["lru_ba"][l], wt["lru_bx"][l], wt["lru_lambda"][l]] + [zrow] * 4)
    lru_cw = jnp.concatenate([wt["lru_conv_w"][l], jnp.zeros((8 - LRU_K, W), F32)], axis=0)
    wax = jnp.concatenate([_block_diag(wt["lru_wa"][l]), _block_diag(wt["lru_wx"][l])], axis=1)
    return dict(
        wcat=_bf(wcat), wup=_bf(wup), rw_vec=rw_vec, mu=mu, cf_vec=cf_vec, cf_dw=cf_dw,
        lru_vec=lru_vec, lru_cw=lru_cw, wax=_bf(wax), wg=_bf(wt["w_gate"][l]), bg=wt["b_gate"][l][None, :],
        wb=_bf(wt["w_branch"][l]), wo=_bf(wt["w_out"][l]), w1=_bf(wt["w_mlp1"][l]), w2=_bf(wt["w_mlp2"][l]),
        hg_gn=wt["hg_norm_g"][l][None, :], nmix=wt["norm_mix_g"][l][None, :], nmlp=wt["norm_mlp_g"][l][None, :],
    )


def _trunk(x, mod, states, wt, layers, ones_bd, mix_rows, rec_rows, nlong, chunk, seg):
    nl = len(layers)
    te = _scan_mats(chunk, seg)
    if states is None:
        s_hg = s_rw = s_shift = s_cf = s_lh = s_lc = None
    else:
        s_hg, s_rw, s_shift, s_cf, s_lh, s_lc = states
        s_shift = s_shift[:, :, None, :]
        s_lh = s_lh[:, :, None, :]
    fg = wt["norm_final_g"][None, :]
    n_hg = n_rw = n_sh = n_cf = n_lh = n_lc = None
    for l, lw in enumerate(layers):
        sh1, sc1, g1, sh2, sc2, g2 = [mod[l, i][:, None, :] for i in range(6)]
        conv_in = None if states is None else (s_cf, s_lh, s_lc)
        p, y_cf, y_lr, n_cf, n_lh, n_lc = _inproj_call(
            x, sc1, sh1, lw["nmix"], lw["wcat"], conv_in, (n_cf, n_lh, n_lc), lw["cf_dw"], lw["cf_vec"],
            lw["lru_cw"], lw["lru_vec"], lw["wax"], l, mix_rows)
        y_hg, n_hg = _hgrn_call(p, s_hg, n_hg, wt["hg_lower"], lw["hg_gn"], te, l, rec_rows, nlong, chunk, seg)
        y_rw, n_rw, n_sh = _rwkv_call(p, s_rw, s_shift, (n_rw, n_sh), lw["mu"], lw["rw_vec"], lw["wup"], ones_bd,
                                      te, l, rec_rows, nlong, chunk, seg)
        x1, h2 = _merge_call((y_hg, y_rw, y_cf, y_lr), x, (sc1, sh1, g1, sc2, sh2), lw["nmix"], lw["nmlp"],
                             lw["wg"], lw["bg"], lw["wb"], lw["wo"])
        x = _mlp_call(h2, x1, g2, fg, lw["w1"], lw["w2"], final=(l == nl - 1))
    return x, [n_hg, n_rw, n_sh[:, :, 0, :], n_cf, n_lh[:, :, 0, :], n_lc]


def _run(x_prompt, x_sample, sample_states, c_prompt, c_sample, wt):
    nl = wt["w_in"].shape[0]
    layers = [_prep_layer(wt, l) for l in range(nl)]
    head = jnp.arange(W, dtype=jnp.int32) // RW_D
    ones_bd = _bf((head[:, None] == head[None, :]).astype(F32))
    tp = x_prompt.shape[1]
    ts = x_sample.shape[1]
    rows_p = min(256, tp)
    chunk_p = min(64, tp)
    bp = x_prompt.shape[0]
    mod = _mod_call(jnp.concatenate([c_prompt, c_sample], axis=0), wt["ada_w"], wt["ada_b"])
    y_p, st_p = _trunk(x_prompt, mod[:, :, :bp], None, wt, layers, ones_bd, rows_p, rows_p // 4, 8, chunk_p,
                       chunk_p)
    y_s, st_s = _trunk(x_sample, mod[:, :, bp:], sample_states, wt, layers, ones_bd, rows_p, 8 * ts, 1, 8 * ts, ts)
    return (y_p, y_s, *st_p, *st_s)


def kernel(x_prompt, x_sample, state_hgrn, state_rwkv, state_rwkv_shift, state_conv, state_lru_h, state_lru_conv, c_prompt, c_sample, ada_w, ada_b, norm_mix_g, norm_mlp_g, norm_final_g, w_in, hg_lower, hg_norm_g, rw_mu, rw_w0, rw_w_up, rw_a0, rw_a_up, rw_g_up, rw_k_k, rw_k_a, rw_r_k, rw_ln_g, rw_ln_b, cf_dw, cf_dw_b, cf_ln_g, cf_ln_b, lru_conv_w, lru_conv_b, lru_wa, lru_ba, lru_wx, lru_bx, lru_lambda, w_branch, w_gate, b_gate, w_out, w_mlp1, w_mlp2):
    wt = dict(ada_w=ada_w, ada_b=ada_b, norm_mix_g=norm_mix_g, norm_mlp_g=norm_mlp_g,
              norm_final_g=norm_final_g, w_in=w_in, hg_lower=hg_lower, hg_norm_g=hg_norm_g, rw_mu=rw_mu,
              rw_w0=rw_w0, rw_w_up=rw_w_up, rw_a0=rw_a0, rw_a_up=rw_a_up, rw_g_up=rw_g_up, rw_k_k=rw_k_k,
              rw_k_a=rw_k_a, rw_r_k=rw_r_k, rw_ln_g=rw_ln_g, rw_ln_b=rw_ln_b, cf_dw=cf_dw, cf_dw_b=cf_dw_b,
              cf_ln_g=cf_ln_g, cf_ln_b=cf_ln_b, lru_conv_w=lru_conv_w, lru_conv_b=lru_conv_b, lru_wa=lru_wa,
              lru_ba=lru_ba, lru_wx=lru_wx, lru_bx=lru_bx, lru_lambda=lru_lambda, w_branch=w_branch,
              w_gate=w_gate, b_gate=b_gate, w_out=w_out, w_mlp1=w_mlp1, w_mlp2=w_mlp2)
    sample_states = (state_hgrn, state_rwkv, state_rwkv_shift, state_conv, state_lru_h, state_lru_conv)
    return _run(x_prompt, x_sample, sample_states, c_prompt, c_sample, wt)
```

```python
import functools
import math

import jax
import jax.numpy as jnp
from jax import lax
from jax.experimental import pallas as pl
from jax.experimental.pallas import tpu as pltpu

D = 1024
W = 512
HG_H = 4
HG_D = 128
RW_H = 8
RW_D = 64
RW_COLS = 1792
RW_PAD = 2048
CF_K = 31
LRU_K = 4
HID = 4096
EPS = 1e-6
RW_GN_EPS = 64e-5
RW_DECAY = 0.606531
LRU_C = 8.0
LANES = 128

P_COLS = 6144
SEQ_UNROLL = 2
VMEM_LIMIT = 56 * 1024 * 1024

F32 = jnp.float32
BF16 = jnp.bfloat16


def _bf(x):
    return x.astype(BF16)


def _mm(a, b):
    return jnp.dot(_bf(a), _bf(b), preferred_element_type=F32)


def _mm_nt(a, b):
    return lax.dot_general(_bf(a), _bf(b), (((1,), (1,)), ((), ())), preferred_element_type=F32)


def _mm_tn(a, b):
    return lax.dot_general(_bf(a), _bf(b), (((0,), (0,)), ((), ())), preferred_element_type=F32)


def _sigmoid(x):
    return 1.0 / (1.0 + jnp.exp(-x))


def _silu(x):
    return x * _sigmoid(x)


def _local_row(shape, seg):
    return jnp.bitwise_and(lax.broadcasted_iota(jnp.int32, shape, 0), seg - 1)


def _shift_rows(x, d, fill, seg):
    return jnp.where(_local_row(x.shape, seg) >= d, pltpu.roll(x, d, axis=0), fill)


def _scan_mats(chunk, seg):
    r = jnp.arange(chunk, dtype=jnp.int32)
    same = (r[:, None] // seg) == (r[None, :] // seg)
    tri = same & (r[None, :] <= r[:, None])
    return _bf(jnp.concatenate([tri, same], axis=0).astype(F32))


def _cumsum_last(x, te):
    c = x.shape[0]
    p1 = _bf(x)
    r1 = x - p1.astype(F32)
    p2 = _bf(r1)
    p3 = _bf(r1 - p2.astype(F32))
    out = (jnp.dot(te, p1, preferred_element_type=F32) + jnp.dot(te, p2, preferred_element_type=F32)
           + jnp.dot(te, p3, preferred_element_type=F32))
    return out[:c], out[c:]


def _seg_id(shape, axis, log2_seg):
    return lax.shift_right_logical(lax.broadcasted_iota(jnp.int32, shape, axis), log2_seg)


def _headsum(x, ones_bd):
    hi = _bf(x)
    lo = _bf(x - hi.astype(F32))
    return (jnp.dot(hi, ones_bd, preferred_element_type=F32)
            + jnp.dot(lo, ones_bd, preferred_element_type=F32))


def _cparams(sem):
    return pltpu.CompilerParams(dimension_semantics=sem, vmem_limit_bytes=VMEM_LIMIT)


def _tile(bsz, t, rows):
    if t >= rows:
        return 1, rows
    return min(bsz, rows // t), t


def _state_specs(l, sb, tail):
    zeros = (0,) * len(tail)
    in_spec = pl.BlockSpec((None, sb) + tail, lambda i, j: (l, i) + zeros)
    prev_spec = pl.BlockSpec((l, sb) + tail, lambda i, j: (0, i) + zeros)
    out_spec = pl.BlockSpec((l + 1, sb) + tail, lambda i, j: (0, i) + zeros)
    return in_spec, prev_spec, out_spec


def _emit_state(l, out_ref, prev_ref, new):
    for q in range(l):
        out_ref[q] = prev_ref[q]
    out_ref[l] = new


def _mod_kernel(c_ref, w_ref, b_ref, o_ref):
    c = c_ref[...]
    o_ref[0, 0] = _mm(_silu(c), w_ref[0]) + b_ref[0, 0]


def _mod_call(c, ada_w, ada_b):
    nl = ada_w.shape[0]
    bsz = c.shape[0]
    return pl.pallas_call(
        _mod_kernel,
        out_shape=jax.ShapeDtypeStruct((nl, 6, bsz, D), F32),
        grid=(nl, 6),
        in_specs=[
            pl.BlockSpec((bsz, D), lambda l, j: (0, 0)),
            pl.BlockSpec((1, D, D), lambda l, j: (l, 0, j)),
            pl.BlockSpec((1, 1, 1, D), lambda l, j: (l, j, 0, 0)),
        ],
        out_specs=pl.BlockSpec((1, 1, bsz, D), lambda l, j: (l, j, 0, 0)),
        compiler_params=_cparams(("arbitrary", "arbitrary")),
        name="adaln_mod",
    )(c, ada_w, ada_b.reshape(nl, 6, 1, D))


def _adaln(x, g, sc, sh):
    ms = jnp.mean(x * x, axis=-1, keepdims=True)
    y = x * lax.rsqrt(ms + EPS) * g
    return y * (1.0 + sc) + sh


CF_PAD = 32
CF_RB = 64
SUBLANES = 8
LRU_PAD = 8
REC_COLS = 4096
REC_PIECE = 1024


def _gelu_tanh(x):
    return 0.5 * x * (1.0 + jnp.tanh(0.7978845608028654 * (x + 0.044715 * (x * x * x))))


def _conformer_prep(pcf, s, slot, ext, shifted):
    tt = pcf.shape[0]
    span = tt + CF_PAD - SUBLANES
    ext[s, CF_PAD:CF_PAD + tt, :] = pcf[:, 0:W] * _sigmoid(pcf[:, W:2 * W])
    for q in range(1, SUBLANES):
        shifted[slot, q, 0:span, :] = ext[s, q:q + span, :]


def _rows(start, size):
    if isinstance(start, int):
        return pl.ds(start, size)
    return pl.ds(pl.multiple_of(start, SUBLANES), size)


def _conformer_rows(r0, rb, s, slot, ext, shifted, y_ref, dw, bias, ln_g, ln_b):
    off = CF_PAD - (CF_K - 1)
    acc = jnp.zeros((rb, W), F32) + bias
    for j in range(CF_K):
        a8, q = divmod(off + j, SUBLANES)
        rows = _rows(r0 + a8 * SUBLANES, rb)
        tap = ext[s, rows, :] if q == 0 else shifted[slot, q, rows, :]
        acc = acc + dw[j:j + 1, :] * tap
    mean = jnp.mean(acc, axis=-1, keepdims=True)
    xc = acc - mean
    var = jnp.mean(xc * xc, axis=-1, keepdims=True)
    yn = xc * lax.rsqrt(var + 1e-5) * ln_g + ln_b
    y_ref[s, _rows(r0, rb), :] = _silu(yn)


def _group_prefix(a, b):
    d = 1
    while d < SUBLANES:
        a_s = _shift_rows(a, d, 1.0, SUBLANES)
        b_s = _shift_rows(b, d, 0.0, SUBLANES)
        b = a * b_s + b
        a = a * a_s
        d *= 2
    return a, b


def _linear_scan(a, b, h):
    n = a.shape[0]
    a, b = _group_prefix(a, b)
    out = []
    for g in range(n // SUBLANES):
        hs = a[g * SUBLANES:(g + 1) * SUBLANES] * h + b[g * SUBLANES:(g + 1) * SUBLANES]
        h = hs[SUBLANES - 1:SUBLANES, :]
        out.append(hs)
    return (jnp.concatenate(out, axis=0) if len(out) > 1 else out[0]), h


def _lru_conv(s, tt, ext, cw, cb):
    off = LRU_PAD - (LRU_K - 1)
    xc = jnp.zeros((tt, W), F32) + cb
    for j in range(LRU_K):
        xc = xc + cw[j:j + 1, :] * ext[s, off + j:off + j + tt, :]
    return xc


def _lru_rows(xc, gate, h, ba, bx, sp, wax):
    pre = jnp.dot(_bf(xc), wax, preferred_element_type=F32)
    rg = _sigmoid(pre[:, 0:W] + ba)
    ig = _sigmoid(pre[:, W:2 * W] + bx)
    a = jnp.exp(-LRU_C * rg * sp)
    hs, h = _linear_scan(a, jnp.sqrt(1.0 - a * a) * (ig * xc), h)
    return hs * _gelu_tanh(gate), h


def _inproj_kernel(cfg, *refs):
    l, zero_init = cfg
    refs = list(refs)
    x_ref, sc_ref, sh_ref, g_ref, w_ref = [refs.pop(0) for _ in range(5)]
    cf0_ref, lh0_ref, lc0_ref = (None,) * 3 if zero_init else [refs.pop(0) for _ in range(3)]
    dw_ref, cfv_ref, cw_ref, lrv_ref, wax_ref = [refs.pop(0) for _ in range(5)]
    cfp_ref, lhp_ref, lcp_ref = [refs.pop(0) for _ in range(3)] if l > 0 else (None,) * 3
    p_ref, ycf_ref, ylr_ref, cfo_ref, lho_ref, lco_ref = [refs.pop(0) for _ in range(6)]
    cf_ext, shifted, lr_ext, hcar, pc_scr, h_scr, xc_scr = refs
    sb, tt, _ = x_ref.shape
    tm = sb * tt
    tstep = pl.program_id(1)
    cf_off = CF_PAD - (CF_K - 1)
    lr_off = LRU_PAD - (LRU_K - 1)

    @pl.when(tstep == 0)
    def _():
        if zero_init:
            cf_ext[:, 0:CF_PAD, :] = jnp.zeros((sb, CF_PAD, W), F32)
            lr_ext[:, 0:LRU_PAD, :] = jnp.zeros((sb, LRU_PAD, W), F32)
            hcar[...] = jnp.zeros_like(hcar)
        else:
            cf_ext[:, cf_off:CF_PAD, :] = cf0_ref[...]
            lr_ext[:, lr_off:LRU_PAD, :] = lc0_ref[...]
            hcar[...] = lh0_ref[...]

    h_scr[...] = _bf(_adaln(x_ref[...], g_ref[...], sc_ref[...], sh_ref[...]).reshape(tm, D))
    pc_scr[...] = jnp.dot(h_scr[...], w_ref[:, REC_COLS:P_COLS], preferred_element_type=F32).reshape(sb, tt, 4 * W)

    def rec_piece(n):
        cols = slice(n * REC_PIECE, (n + 1) * REC_PIECE)
        p_ref[:, :, cols] = jnp.dot(h_scr[...], w_ref[:, cols], preferred_element_type=F32).reshape(sb, tt, REC_PIECE)

    cfv = cfv_ref[...]
    dw = dw_ref[...]
    lrv = lrv_ref[...]
    lam = lrv[3:4]
    sp = jnp.maximum(-lam, 0.0) + jnp.log1p(jnp.exp(-jnp.abs(lam)))
    cw = cw_ref[...]
    wax = wax_ref[...]
    nslot = shifted.shape[0]

    rb = min(CF_RB, tt)

    def prep(s, slot):
        _conformer_prep(pc_scr[s, :, 0:2 * W], s, slot, cf_ext, shifted)
        lr_ext[s, LRU_PAD:LRU_PAD + tt, :] = pc_scr[s, :, 2 * W:3 * W]
        xc_scr[s] = _lru_conv(s, tt, lr_ext, cw, lrv[0:1])

    def row_block(s, slot, r0, hstate):
        _conformer_rows(r0, rb, s, slot, cf_ext, shifted, ycf_ref, dw, cfv[0:1], cfv[1:2], cfv[2:3])
        rows = _rows(r0, rb)
        y, hstate = _lru_rows(xc_scr[s, rows, :], pc_scr[s, rows, 3 * W:4 * W], hstate, lrv[1:2], lrv[2:3], sp, wax)
        ylr_ref[s, rows, :] = y
        return hstate

    def finish(s, hstate):
        hcar[s] = hstate
        cf_tail = cf_ext[s, tt:tt + CF_PAD, :]
        cf_ext[s, 0:CF_PAD, :] = cf_tail
        lr_tail = lr_ext[s, tt:tt + LRU_PAD, :]
        lr_ext[s, 0:LRU_PAD, :] = lr_tail

    for n in range(REC_COLS // REC_PIECE):
        rec_piece(n)

    def all_short():
        pc = pc_scr[...]
        cf_ext[:, CF_PAD:CF_PAD + tt, :] = pc[:, :, 0:W] * _sigmoid(pc[:, :, W:2 * W])
        acc = jnp.zeros((sb, tt, W), F32) + cfv[0:1]
        for j in range(CF_K):
            acc = acc + dw[j:j + 1, :] * cf_ext[:, cf_off + j:cf_off + j + tt, :]
        mean = jnp.mean(acc, axis=-1, keepdims=True)
        xn = acc - mean
        var = jnp.mean(xn * xn, axis=-1, keepdims=True)
        ycf_ref[...] = _silu(xn * lax.rsqrt(var + 1e-5) * cfv[1:2] + cfv[2:3])
        lr_ext[:, LRU_PAD:LRU_PAD + tt, :] = pc[:, :, 2 * W:3 * W]
        xc = jnp.zeros((sb, tt, W), F32) + lrv[0:1]
        for j in range(LRU_K):
            xc = xc + cw[j:j + 1, :] * lr_ext[:, lr_off + j:lr_off + j + tt, :]
        xc = xc.reshape(tm, W)
        pre = jnp.dot(_bf(xc), wax, preferred_element_type=F32)
        rg = _sigmoid(pre[:, 0:W] + lrv[1:2])
        ig = _sigmoid(pre[:, W:2 * W] + lrv[2:3])
        a = jnp.exp(-LRU_C * rg * sp)
        a, b = _group_prefix(a, jnp.sqrt(1.0 - a * a) * (ig * xc))
        hs = a.reshape(sb, tt, W) * hcar[...] + b.reshape(sb, tt, W)
        hcar[...] = hs[:, tt - 1:tt, :]
        ylr_ref[...] = hs * _gelu_tanh(pc[:, :, 3 * W:4 * W])
        cf_tail = cf_ext[:, tt:tt + CF_PAD, :]
        cf_ext[:, 0:CF_PAD, :] = cf_tail
        lr_tail = lr_ext[:, tt:tt + LRU_PAD, :]
        lr_ext[:, 0:LRU_PAD, :] = lr_tail

    def body(i, carry):
        for slot in range(nslot):
            s = i * nslot + slot
            prep(s, slot)
            hstate = hcar[s]
            for r0 in range(0, tt, rb):
                hstate = row_block(s, slot, r0, hstate)
            finish(s, hstate)
        return carry

    if tt == SUBLANES:
        all_short()
    else:
        lax.fori_loop(0, sb // nslot, body, 0)

    @pl.when(tstep == pl.num_programs(1) - 1)
    def _():
        _emit_state(l, cfo_ref, cfp_ref, cf_ext[:, cf_off:CF_PAD, :])
        _emit_state(l, lho_ref, lhp_ref, hcar[...])
        _emit_state(l, lco_ref, lcp_ref, lr_ext[:, lr_off:LRU_PAD, :])


def _inproj_call(x, sc, sh, g, wcat, conv_in, conv_prev, dw, cfv, cw, lrv, wax, l, rows=256):
    bsz, t, _ = x.shape
    sb, tt = _tile(bsz, t, rows)
    zero_init = conv_in is None
    tails = ((CF_K - 1, W), (1, W), (LRU_K - 1, W))
    sspecs = [_state_specs(l, sb, tail) for tail in tails]
    tok = lambda width: pl.BlockSpec((sb, tt, width), lambda i, j: (i, j, 0))
    seq = pl.BlockSpec((sb, 1, D), lambda i, j: (i, 0, 0))
    const = lambda shape: pl.BlockSpec(shape, lambda i, j: (0,) * len(shape), pipeline_mode=pl.Buffered(1))
    args = ([x, sc, sh, g, wcat] + ([] if zero_init else list(conv_in)) + [dw, cfv, cw, lrv, wax]
            + (list(conv_prev) if l > 0 else []))
    specs = ([tok(D), seq, seq, const((1, D)), const((D, P_COLS))]
             + ([] if zero_init else [sp[0] for sp in sspecs])
             + [const((32, W)), const((8, W)), const((8, W)), const((8, W)), const((W, 2 * W))]
             + ([sp[1] for sp in sspecs] if l > 0 else []))
    nslot = min(sb, SEQ_UNROLL)
    return pl.pallas_call(
        functools.partial(_inproj_kernel, (l, zero_init)),
        out_shape=(jax.ShapeDtypeStruct((bsz, t, REC_COLS), F32),
                   jax.ShapeDtypeStruct((bsz, t, W), F32),
                   jax.ShapeDtypeStruct((bsz, t, W), F32))
        + tuple(jax.ShapeDtypeStruct((l + 1, bsz) + tail, F32) for tail in tails),
        grid=(bsz // sb, t // tt),
        in_specs=specs,
        out_specs=(tok(REC_COLS), tok(W), tok(W)) + tuple(sp[2] for sp in sspecs),
        scratch_shapes=[pltpu.VMEM((sb, CF_PAD + tt, W), F32),
                        pltpu.VMEM((nslot, SUBLANES, CF_PAD + tt, W), F32),
                        pltpu.VMEM((sb, LRU_PAD + tt, W), F32),
                        pltpu.VMEM((sb, 1, W), F32),
                        pltpu.VMEM((sb, tt, 4 * W), F32),
                        pltpu.VMEM((sb * tt, D), BF16),
                        pltpu.VMEM((sb, tt, W), F32)],
        compiler_params=_cparams(("arbitrary", "arbitrary")),
        name="inproj",
    )(*args)


HG_SUB = 16


def _hgrn_chunk(qr, fz, iv, og, lb, gn, te, states, seg):
    c, width = qr.shape
    nseg = c // seg
    heads = range(width // HG_D)
    q = _silu(qr)
    f = lb + (1.0 - lb) * _sigmoid(fz)
    logf = jnp.log(f)
    kf = (1.0 - lb) * _sigmoid(-fz)
    b, b_last = _cumsum_last(logf, te)
    qe = q * jnp.exp(b)
    kdec = kf * jnp.exp(b_last - b)
    e_last = jnp.exp(b_last)
    sl = [slice(h * HG_D, (h + 1) * HG_D) for h in heads]

    if nseg == 1:
        sub = min(HG_SUB, c)
        nsub = c // sub
        o_inter = [_mm_nt(qe[:, sl[h]], states[h]) for h in heads]
        new_states = [states[h] * e_last[0:1, sl[h]] + _mm_tn(iv[:, sl[h]], kdec[:, sl[h]]) for h in heads]
        pieces = [[] for _ in heads]
        for i in range(nsub):
            r0 = i * sub
            m = b[r0 - 1:r0, :] if i > 0 else jnp.zeros((1, width), F32)
            qs = q[r0:r0 + sub] * jnp.exp(b[r0:r0 + sub] - m)
            kd = kf[r0:r0 + sub] * jnp.exp(jnp.minimum(m - b[r0:r0 + sub], 80.0))
            if i > 0:
                kall = jnp.concatenate([kf[:r0] * jnp.exp(m - b[:r0]), kd], axis=0)
            else:
                kall = kd
            row = lax.broadcasted_iota(jnp.int32, (sub, r0 + sub), 0) + r0
            col = lax.broadcasted_iota(jnp.int32, (sub, r0 + sub), 1)
            causal = col <= row
            scs = [jnp.where(causal, _mm_nt(qs[:, sl[h]], kall[:, sl[h]]), 0.0) for h in heads]
            for h in heads:
                pieces[h].append(_mm(scs[h], iv[:r0 + sub, sl[h]]))
        o_intra = [jnp.concatenate(pieces[h], axis=0) if nsub > 1 else pieces[h][0] for h in heads]
    else:
        lg = int(math.log2(seg))
        kd = kf * jnp.exp(jnp.minimum(-b, 80.0))
        rr = lax.broadcasted_iota(jnp.int32, (c, c), 0)
        cc = lax.broadcasted_iota(jnp.int32, (c, c), 1)
        causal = (lax.shift_right_logical(rr, lg) == lax.shift_right_logical(cc, lg)) & (cc <= rr)
        rowseg = _seg_id((c, HG_D), 0, lg)
        blockmask = _seg_id((c, nseg * HG_D), 0, lg) == _seg_id((c, nseg * HG_D), 1, 7)
        e3 = e_last.reshape(nseg, seg, width)[:, 0:1, :]
        full = [_mm_nt(qe[:, sl[h]], states[h].reshape(nseg * HG_D, HG_D)) for h in heads]
        o_inter = []
        for h in heads:
            o = full[h][:, 0:HG_D]
            for s in range(1, nseg):
                o = jnp.where(rowseg == s, full[h][:, s * HG_D:(s + 1) * HG_D], o)
            o_inter.append(o)
        scs = [jnp.where(causal, _mm_nt(qe[:, sl[h]], kd[:, sl[h]]), 0.0) for h in heads]
        o_intra = [_mm(scs[h], iv[:, sl[h]]) for h in heads]
        new_states = []
        for h in heads:
            ivexp = jnp.where(blockmask, jnp.concatenate([iv[:, sl[h]]] * nseg, axis=1), 0.0)
            upd = _mm_tn(ivexp, kdec[:, sl[h]]).reshape(nseg, HG_D, HG_D)
            new_states.append(states[h] * e3[:, :, sl[h]] + upd)
    outs = []
    for h in heads:
        o = o_inter[h] + o_intra[h]
        outs.append(o * lax.rsqrt(jnp.mean(o * o, axis=-1, keepdims=True) + EPS))
    y = jnp.concatenate(outs, axis=1) * gn * _silu(og)
    return y, new_states


def _hgrn_kernel(cfg, *refs):
    l, chunk, seg, zero_init = cfg
    refs = list(refs)
    p_ref = refs.pop(0)
    s0_ref = None if zero_init else refs.pop(0)
    lower_ref, gn_ref, te_ref = refs.pop(0), refs.pop(0), refs.pop(0)
    prev_ref = refs.pop(0) if l > 0 else None
    y_ref, sout_ref, s_scr = refs
    sb, tt, _ = p_ref.shape
    nseg = chunk // seg
    tstep = pl.program_id(1)

    @pl.when(tstep == 0)
    def _():
        if zero_init:
            s_scr[...] = jnp.zeros_like(s_scr)
        else:
            def init(i, carry):
                for h in range(HG_H):
                    s_scr[i, h] = s0_ref[i, h].T
                return carry
            lax.fori_loop(0, sb, init, 0)

    low = lower_ref[...]
    e = jnp.exp(low - jnp.max(low, axis=0, keepdims=True))
    sm = e / jnp.sum(e, axis=0, keepdims=True)
    lb = jnp.sum(sm[:l + 1], axis=0, keepdims=True) - sm[0:1]
    gn = gn_ref[...]
    te = te_ref[...]

    def body_long(i, carry):
        rs = pl.ds(pl.multiple_of(i * chunk, chunk), chunk)
        cols = [jnp.concatenate([p_ref[q, rs, j * W:(j + 1) * W] for q in range(sb)], axis=1) for j in range(4)]
        wide = lambda v: jnp.concatenate([v] * sb, axis=1)
        y, states = _hgrn_chunk(*cols, wide(lb), wide(gn), te,
                                [s_scr[q, h] for q in range(sb) for h in range(HG_H)], seg)
        for q in range(sb):
            for h in range(HG_H):
                s_scr[q, h] = states[q * HG_H + h]
            y_ref[q, rs, :] = y[:, q * W:(q + 1) * W]
        return carry

    def all_short():
        nch = sb * tt // chunk
        grp = [slice(c * nseg, (c + 1) * nseg) for c in range(nch)]
        blks = [p_ref[g].reshape(chunk, 4 * W) for g in grp]
        cols = [jnp.concatenate([b[:, j * W:(j + 1) * W] for b in blks], axis=1) for j in range(4)]
        wide = lambda v: jnp.concatenate([v] * nch, axis=1)
        y, states = _hgrn_chunk(*cols, wide(lb), wide(gn), te,
                                [s_scr[g, h] for g in grp for h in range(HG_H)], seg)
        for c, g in enumerate(grp):
            for h in range(HG_H):
                s_scr[g, h] = states[c * HG_H + h]
            y_ref[g] = y[:, c * W:(c + 1) * W].reshape(nseg, seg, W)

    if nseg == 1:
        lax.fori_loop(0, tt // chunk, body_long, 0)
    else:
        all_short()

    @pl.when(tstep == pl.num_programs(1) - 1)
    def _():
        for q in range(l):
            sout_ref[q] = prev_ref[q]

        def fin(i, carry):
            for h in range(HG_H):
                sout_ref[l, i, h] = s_scr[i, h].T
            return carry
        lax.fori_loop(0, sb, fin, 0)


def _rec_tile(bsz, t, rows, nlong):
    sb, tt = _tile(bsz, t, rows)
    return (min(nlong, bsz), tt) if t >= rows else (sb, tt)


def _hgrn_call(p, s_in, prev, hg_lower, gn, te, l, rows, nlong, chunk, seg):
    bsz, t, _ = p.shape
    sb, tt = _rec_tile(bsz, t, rows, nlong)
    zero_init = s_in is None
    in_spec, prev_spec, out_spec = _state_specs(l, sb, (HG_H, HG_D, HG_D))
    args = [p] + ([] if zero_init else [s_in]) + [hg_lower, gn, te] + ([prev] if l > 0 else [])
    specs = ([pl.BlockSpec((sb, tt, 4 * W), lambda i, j: (i, j, 0))] + ([] if zero_init else [in_spec])
             + [pl.BlockSpec(hg_lower.shape, lambda i, j: (0, 0)), pl.BlockSpec((1, W), lambda i, j: (0, 0)),
                pl.BlockSpec(te.shape, lambda i, j: (0, 0))]
             + ([prev_spec] if l > 0 else []))
    return pl.pallas_call(
        functools.partial(_hgrn_kernel, (l, chunk, seg, zero_init)),
        out_shape=(jax.ShapeDtypeStruct((bsz, t, W), F32),
                   jax.ShapeDtypeStruct((l + 1, bsz, HG_H, HG_D, HG_D), F32)),
        grid=(bsz // sb, t // tt),
        in_specs=specs,
        out_specs=(pl.BlockSpec((sb, tt, W), lambda i, j: (i, j, 0)), out_spec),
        scratch_shapes=[pltpu.VMEM((sb, HG_H, HG_D, HG_D), F32)],
        compiler_params=_cparams(("arbitrary", "arbitrary")),
        name="hgrn2",
    )(*args)


def _pick64(full, rowseg):
    nb = full.shape[1] // LANES
    half = lax.shift_right_logical(rowseg, 1)
    sel = full[:, 0:LANES]
    for j in range(1, nb):
        sel = jnp.where(half == j, full[:, j * LANES:(j + 1) * LANES], sel)
    sel = jnp.where(jnp.bitwise_and(rowseg, 1) == 1, pltpu.roll(sel, RW_D, axis=1), sel)
    return sel[:, 0:RW_D]


def _rwkv_recur(at, rt, bi, ki, bd, kd, v, states, pc, seg):
    c = at.shape[0]
    nseg = c // seg
    lg = int(math.log2(seg))
    rr = lax.broadcasted_iota(jnp.int32, (c, c), 0)
    cc = lax.broadcasted_iota(jnp.int32, (c, c), 1)
    if nseg == 1:
        strict = rr > cc
        incl = rr >= cc
    else:
        same = lax.shift_right_logical(rr, lg) == lax.shift_right_logical(cc, lg)
        strict = same & (rr > cc)
        incl = same & (rr >= cc)
        r2 = jnp.bitwise_and(lax.broadcasted_iota(jnp.int32, (2 * c, LANES), 0), c - 1)
        rowseg = lax.shift_right_logical(r2, lg)
        rb = jnp.bitwise_and(lax.broadcasted_iota(jnp.int32, (2 * c, nseg * RW_D), 0), c - 1)
        blockmask = lax.shift_right_logical(rb, lg) == _seg_id((2 * c, nseg * RW_D), 1, 6)
    heads = range(at.shape[1] // RW_D)
    sl = [slice(h * RW_D, (h + 1) * RW_D) for h in heads]
    ar = [jnp.concatenate([at[:, sl[h]], rt[:, sl[h]]], axis=0) for h in heads]
    bk = [jnp.concatenate([bi[:, sl[h]], ki[:, sl[h]]], axis=0) for h in heads]
    g = [_mm_nt(ar[h], bk[h]) for h in heads]
    if nseg == 1:
        a_s = [_mm_nt(ar[h], states[h]) for h in heads]
    else:
        a_s = [_pick64(_mm_nt(ar[h], states[h].reshape(nseg * RW_D, RW_D)), rowseg) for h in heads]
    vh = [v[:, sl[h]] for h in heads]
    lp = [jnp.where(strict, g[h][:c, :c], 0.0) for h in heads]
    x = [a_s[h][:c] + _mm(jnp.where(strict, g[h][:c, c:], 0.0), vh[h]) for h in heads]
    for j in range(lg):
        x = [x[h] + _mm(lp[h], x[h]) for h in heads]
        if j < lg - 1:
            lp = [_mm(lp[h], lp[h]) for h in heads]
    uv = [jnp.concatenate([x[h], vh[h]], axis=0) for h in heads]
    mrbk = [jnp.concatenate([jnp.where(incl, g[h][c:, :c], 0.0), jnp.where(incl, g[h][c:, c:], 0.0)], axis=1)
            for h in heads]
    ys = [a_s[h][c:] + _mm(mrbk[h], uv[h]) for h in heads]
    bkd = [jnp.concatenate([bd[:, sl[h]], kd[:, sl[h]]], axis=0) for h in heads]
    if nseg == 1:
        new_states = [states[h] * pc[:, sl[h]] + _mm_tn(uv[h], bkd[h]) for h in heads]
    else:
        new_states = []
        for h in heads:
            u2 = jnp.concatenate([uv[h], uv[h]], axis=1)
            uvexp = jnp.where(blockmask, jnp.concatenate([u2] * (nseg // 2), axis=1), 0.0)
            upd = _mm_tn(uvexp, bkd[h]).reshape(nseg, RW_D, RW_D)
            new_states.append(states[h] * pc[:, :, sl[h]] + upd)
    return ys, new_states


def _rwkv_kernel(cfg, *refs):
    l, chunk, seg, zero_init = cfg
    refs = list(refs)
    p_ref = refs.pop(0)
    s0_ref, sh0_ref = (None, None) if zero_init else (refs.pop(0), refs.pop(0))
    mu_ref, vec_ref, wup_ref, ones_ref, te_ref = [refs.pop(0) for _ in range(5)]
    sprev_ref, shprev_ref = (refs.pop(0), refs.pop(0)) if l > 0 else (None, None)
    y_ref, sout_ref, shout_ref = refs.pop(0), refs.pop(0), refs.pop(0)
    s_scr, prev_scr, at_s, rt_s, bi_s, ki_s, bd_s, kd_s, v_s, lpl_s, yr_s, bon_s, gate_s = refs
    sb, tt, _ = p_ref.shape
    rows = sb * tt
    nseg = chunk // seg
    tstep = pl.program_id(1)

    @pl.when(tstep == 0)
    def _():
        if zero_init:
            s_scr[...] = jnp.zeros_like(s_scr)
            prev_scr[...] = jnp.zeros_like(prev_scr)
        else:
            s_scr[...] = s0_ref[...]
            prev_scr[:, :, 0:RW_COLS] = sh0_ref[...]
            prev_scr[:, :, RW_COLS:RW_PAD] = jnp.zeros((sb, 1, RW_PAD - RW_COLS), F32)

    vec = vec_ref[...]
    w0, a0, k_k, k_a, r_k, ln_g, ln_b = [vec[i:i + 1, :] for i in range(7)]
    ones_bd = ones_ref[...]

    rw3 = p_ref[...]
    rw = rw3.reshape(rows, RW_PAD)
    prev_rows = jnp.broadcast_to(prev_scr[...], (sb, tt, RW_PAD)).reshape(rows, RW_PAD)
    prev = jnp.where(_local_row(rw.shape, tt) == 0, prev_rows, pltpu.roll(rw, 1, axis=0))
    prev_scr[...] = rw3[:, tt - 1:tt, :]
    rwm = rw + (prev - rw) * mu_ref[...]
    r, k, v = rwm[:, 0:W], rwm[:, W:2 * W], rwm[:, 2 * W:3 * W]
    lr = rwm[:, 3 * W:3 * W + 256]
    lane = lax.broadcasted_iota(jnp.int32, lr.shape, 1)
    act = jnp.where(lane < 64, jnp.tanh(lr), jnp.where(lane < 128, lr, _sigmoid(lr)))
    up = jnp.dot(_bf(act), wup_ref[...], preferred_element_type=F32)
    log_w = -RW_DECAY * _sigmoid(w0 + up[:, 0:W])
    a = _sigmoid(a0 + up[:, W:2 * W])
    gate_s[...] = up[:, 2 * W:3 * W]
    kk = k * k_k
    kk = kk / jnp.maximum(jnp.sqrt(_headsum(kk * kk, ones_bd)), 1e-12)
    k2 = k * (1.0 + (a - 1.0) * k_a)
    kka = kk * a
    bon_s[...] = _headsum(r * k2 * r_k, ones_bd) * v
    v_s[...] = v
    te = te_ref[...]
    scans = [_cumsum_last(log_w[i * chunk:(i + 1) * chunk], te) for i in range(rows // chunk)]
    logp = jnp.concatenate([s[0] for s in scans], axis=0) if len(scans) > 1 else scans[0][0]
    lpl = jnp.concatenate([s[1] for s in scans], axis=0) if len(scans) > 1 else scans[0][1]
    lpl_s[...] = lpl
    at_s[...] = -kk * jnp.exp(logp - log_w)
    rt_s[...] = r * jnp.exp(logp)
    einv = jnp.exp(-logp)
    bi_s[...] = kka * einv
    ki_s[...] = k2 * einv
    elast = jnp.exp(lpl - logp)
    bd_s[...] = kka * elast
    kd_s[...] = k2 * elast

    def body_long(ch, carry):
        rss = [pl.ds(pl.multiple_of(q * tt + ch * chunk, chunk), chunk) for q in range(sb)]
        wide = lambda ref: jnp.concatenate([ref[rs, :] for rs in rss], axis=1)
        pc = jnp.exp(jnp.concatenate([lpl_s[pl.ds(rs.start, 1), :] for rs in rss], axis=1))
        ys, states = _rwkv_recur(wide(at_s), wide(rt_s), wide(bi_s), wide(ki_s), wide(bd_s), wide(kd_s),
                                 wide(v_s), [s_scr[q, h] for q in range(sb) for h in range(RW_H)], pc, seg)
        for q in range(sb):
            for h in range(RW_H):
                s_scr[q, h] = states[q * RW_H + h]
            yr_s[rss[q], :] = jnp.concatenate(ys[q * RW_H:(q + 1) * RW_H], axis=1)
        return carry

    def all_short():
        nch = rows // chunk
        rss = [slice(c * chunk, (c + 1) * chunk) for c in range(nch)]
        grp = [slice(c * nseg, (c + 1) * nseg) for c in range(nch)]
        wide = lambda ref: jnp.concatenate([ref[rs, :] for rs in rss], axis=1)
        pc = jnp.exp(jnp.concatenate([lpl_s[rs, :].reshape(nseg, seg, W)[:, 0:1, :] for rs in rss], axis=2))
        ys, states = _rwkv_recur(wide(at_s), wide(rt_s), wide(bi_s), wide(ki_s), wide(bd_s), wide(kd_s),
                                 wide(v_s), [s_scr[g, h] for g in grp for h in range(RW_H)], pc, seg)
        for c, g in enumerate(grp):
            for h in range(RW_H):
                s_scr[g, h] = states[c * RW_H + h]
            yr_s[rss[c], :] = jnp.concatenate(ys[c * RW_H:(c + 1) * RW_H], axis=1)

    if nseg == 1:
        lax.fori_loop(0, tt // chunk, body_long, 0)
    else:
        all_short()

    y = yr_s[...]
    mean = _headsum(y, ones_bd) * (1.0 / RW_D)
    yc = y - mean
    var = _headsum(yc * yc, ones_bd) * (1.0 / RW_D)
    yn = yc * lax.rsqrt(var + RW_GN_EPS) * ln_g + ln_b
    y_ref[...] = ((yn + bon_s[...]) * gate_s[...]).reshape(sb, tt, W)

    @pl.when(tstep == pl.num_programs(1) - 1)
    def _():
        _emit_state(l, sout_ref, sprev_ref, s_scr[...])
        _emit_state(l, shout_ref, shprev_ref, prev_scr[:, :, 0:RW_COLS])


def _rwkv_call(p, s_in, sh_in, prev, mu, vec, wup, ones_bd, te, l, rows, nlong, chunk, seg):
    bsz, t, _ = p.shape
    sb, tt = _rec_tile(bsz, t, rows, nlong)
    zero_init = s_in is None
    s_specs = _state_specs(l, sb, (RW_H, RW_D, RW_D))
    sh_specs = _state_specs(l, sb, (1, RW_COLS))
    args = ([p] + ([] if zero_init else [s_in, sh_in]) + [mu, vec, wup, ones_bd, te]
            + (list(prev) if l > 0 else []))
    specs = ([pl.BlockSpec((sb, tt, RW_PAD), lambda i, j: (i, j, 1))]
             + ([] if zero_init else [s_specs[0], sh_specs[0]])
             + [pl.BlockSpec((1, RW_PAD), lambda i, j: (0, 0)),
                pl.BlockSpec((8, W), lambda i, j: (0, 0)),
                pl.BlockSpec((256, 3 * W), lambda i, j: (0, 0)),
                pl.BlockSpec((W, W), lambda i, j: (0, 0)),
                pl.BlockSpec(te.shape, lambda i, j: (0, 0))]
             + ([s_specs[1], sh_specs[1]] if l > 0 else []))
    tile_scr = [pltpu.VMEM((sb * tt, W), F32) for _ in range(11)]
    return pl.pallas_call(
        functools.partial(_rwkv_kernel, (l, chunk, seg, zero_init)),
        out_shape=(jax.ShapeDtypeStruct((bsz, t, W), F32),
                   jax.ShapeDtypeStruct((l + 1, bsz, RW_H, RW_D, RW_D), F32),
                   jax.ShapeDtypeStruct((l + 1, bsz, 1, RW_COLS), F32)),
        grid=(bsz // sb, t // tt),
        in_specs=specs,
        out_specs=(pl.BlockSpec((sb, tt, W), lambda i, j: (i, j, 0)), s_specs[2], sh_specs[2]),
        scratch_shapes=[pltpu.VMEM((sb, RW_H, RW_D, RW_D), F32), pltpu.VMEM((sb, 1, RW_PAD), F32)] + tile_scr,
        compiler_params=_cparams(("arbitrary", "arbitrary")),
        name="rwkv7",
    )(*args)


def _merge_kernel(yhg_ref, yrw_ref, ycf_ref, ylr_ref, x_ref, sc1_ref, sh1_ref, g1_ref, sc2_ref, sh2_ref,
                  nmix_ref, nmlp_ref, wg_ref, bg_ref, wb_ref, wo_ref, x1_ref, h2_ref):
    sb, tt, _ = x_ref.shape
    tm = sb * tt
    x = x_ref[...]
    h = _bf(_adaln(x, nmix_ref[...], sc1_ref[...], sh1_ref[...]).reshape(tm, D))
    z = jnp.zeros((tm, D), F32)
    for b, y_ref in enumerate((yhg_ref, yrw_ref, ycf_ref, ylr_ref)):
        bo = jnp.dot(_bf(y_ref[...].reshape(tm, W)), wb_ref[b], preferred_element_type=F32)
        logit = jnp.dot(h, wg_ref[:, b * D:(b + 1) * D], preferred_element_type=F32) + bg_ref[:, b * D:(b + 1) * D]
        z = z + _sigmoid(logit) * bo
    out = jnp.dot(_bf(z), wo_ref[...], preferred_element_type=F32).reshape(sb, tt, D)
    x1 = x + g1_ref[...] * out
    x1_ref[...] = x1
    h2_ref[...] = _bf(_adaln(x1, nmlp_ref[...], sc2_ref[...], sh2_ref[...]))


def _merge_call(ys, x, mods, nmix, nmlp, wg, bg, wb, wo, rows=512):
    bsz, t, _ = x.shape
    sb, tt = _tile(bsz, t, rows)
    tok = lambda width: pl.BlockSpec((sb, tt, width), lambda i, j: (i, j, 0))
    seq = pl.BlockSpec((sb, 1, D), lambda i, j: (i, 0, 0))
    const = lambda shape: pl.BlockSpec(shape, lambda i, j: (0,) * len(shape), pipeline_mode=pl.Buffered(1))
    return pl.pallas_call(
        _merge_kernel,
        out_shape=(jax.ShapeDtypeStruct((bsz, t, D), F32),
                   jax.ShapeDtypeStruct((bsz, t, D), BF16)),
        grid=(bsz // sb, t // tt),
        in_specs=[tok(W), tok(W), tok(W), tok(W), tok(D), seq, seq, seq, seq, seq,
                  const((1, D)), const((1, D)), const((D, 4 * D)), const((1, 4 * D)),
                  const((4, W, D)), const((D, D))],
        out_specs=(tok(D), tok(D)),
        compiler_params=_cparams(("arbitrary", "arbitrary")),
        name="merge",
    )(*ys, x, *mods, nmix, nmlp, wg, bg, wb, wo)


def _mlp_kernel(final, h2_ref, x1_ref, g2_ref, fg_ref, w1_ref, w2_ref, o_ref, acc):
    sb, tt, _ = x1_ref.shape
    k = pl.program_id(2)

    @pl.when(k == 0)
    def _():
        acc[...] = jnp.zeros_like(acc)

    hid = jnp.dot(h2_ref[...].reshape(sb * tt, D), w1_ref[...], preferred_element_type=F32)
    act = jnp.square(jnp.maximum(hid, 0.0))
    acc[...] += jnp.dot(_bf(act), w2_ref[...], preferred_element_type=F32)

    @pl.when(k == pl.num_programs(2) - 1)
    def _():
        x2 = x1_ref[...] + g2_ref[...] * acc[...].reshape(sb, tt, D)
        if final:
            ms = jnp.mean(x2 * x2, axis=-1, keepdims=True)
            x2 = x2 * lax.rsqrt(ms + EPS) * fg_ref[...]
        o_ref[...] = x2


def _mlp_call(h2, x1, g2, fg, w1, w2, final, rows=1024, th=1024):
    bsz, t, _ = x1.shape
    sb, tt = _tile(bsz, t, rows)
    tok = pl.BlockSpec((sb, tt, D), lambda i, j, k: (i, j, 0))
    return pl.pallas_call(
        functools.partial(_mlp_kernel, final),
        out_shape=jax.ShapeDtypeStruct((bsz, t, D), F32),
        grid=(bsz // sb, t // tt, HID // th),
        in_specs=[tok, tok,
                  pl.BlockSpec((sb, 1, D), lambda i, j, k: (i, 0, 0)),
                  pl.BlockSpec((1, D), lambda i, j, k: (0, 0)),
                  pl.BlockSpec((D, th), lambda i, j, k: (0, k)),
                  pl.BlockSpec((th, D), lambda i, j, k: (k, 0))],
        out_specs=tok,
        scratch_shapes=[pltpu.VMEM((sb * tt, D), F32)],
        compiler_params=_cparams(("arbitrary", "arbitrary", "arbitrary")),
        name="mlp",
    )(h2, x1, g2, fg, w1, w2)


def _block_diag(w):
    n, c, d = w.shape
    eye = jnp.eye(n, dtype=w.dtype)
    return (eye[:, None, :, None] * w[:, :, None, :]).reshape(n * c, n * d)


def _prep_layer(wt, l):
    w_in = wt["w_in"][l]
    zpad = jnp.zeros((D, RW_PAD - RW_COLS), F32)
    wcat = jnp.concatenate([w_in[:, 0:2048], w_in[:, 2048:2048 + RW_COLS], zpad, w_in[:, 2048 + RW_COLS:]], axis=1)
    wup = jnp.zeros((256, 3 * W), F32)
    wup = wup.at[0:64, 0:W].set(wt["rw_w_up"][l])
    wup = wup.at[64:128, W:2 * W].set(wt["rw_a_up"][l])
    wup = wup.at[128:256, 2 * W:3 * W].set(wt["rw_g_up"][l])
    zrow = jnp.zeros((W,), F32)
    rw_vec = jnp.stack([wt["rw_w0"][l], wt["rw_a0"][l], wt["rw_k_k"][l], wt["rw_k_a"][l], wt["rw_r_k"][l],
                        wt["rw_ln_g"][l], wt["rw_ln_b"][l], zrow])
    mu = jnp.concatenate([wt["rw_mu"][l], jnp.zeros((RW_PAD - RW_COLS,), F32)])[None, :]
    cf_vec = jnp.stack([wt["cf_dw_b"][l], wt["cf_ln_g"][l], wt["cf_ln_b"][l]] + [zrow] * 5)
    cf_dw = jnp.concatenate([wt["cf_dw"][l], jnp.zeros((1, W), F32)], axis=0)
    lru_vec = jnp.stack([wt["lru_conv_b"][l], wt["lru_ba"][l], wt["lru_bx"][l], wt["lru_lambda"][l]] + [zrow] * 4)
    lru_cw = jnp.concatenate([wt["lru_conv_w"][l], jnp.zeros((8 - LRU_K, W), F32)], axis=0)
    wax = jnp.concatenate([_block_diag(wt["lru_wa"][l]), _block_diag(wt["lru_wx"][l])], axis=1)
    return dict(
        wcat=_bf(wcat), wup=_bf(wup), rw_vec=rw_vec, mu=mu, cf_vec=cf_vec, cf_dw=cf_dw,
        lru_vec=lru_vec, lru_cw=lru_cw, wax=_bf(wax), wg=_bf(wt["w_gate"][l]), bg=wt["b_gate"][l][None, :],
        wb=_bf(wt["w_branch"][l]), wo=_bf(wt["w_out"][l]), w1=_bf(wt["w_mlp1"][l]), w2=_bf(wt["w_mlp2"][l]),
        hg_gn=wt["hg_norm_g"][l][None, :], nmix=wt["norm_mix_g"][l][None, :], nmlp=wt["norm_mlp_g"][l][None, :],
    )


def _trunk(x, mod, states, wt, layers, ones_bd, mix_rows, rec_rows, nlong, chunk, seg):
    nl = len(layers)
    te = _scan_mats(chunk, seg)
    if states is None:
        s_hg = s_rw = s_shift = s_cf = s_lh = s_lc = None
    else:
        s_hg, s_rw, s_shift, s_cf, s_lh, s_lc = states
        s_shift = s_shift[:, :, None, :]
        s_lh = s_lh[:, :, None, :]
    fg = wt["norm_final_g"][None, :]
    n_hg = n_rw = n_sh = n_cf = n_lh = n_lc = None
    for l, lw in enumerate(layers):
        sh1, sc1, g1, sh2, sc2, g2 = [mod[l, i][:, None, :] for i in range(6)]
        conv_in = None if states is None else (s_cf, s_lh, s_lc)
        p, y_cf, y_lr, n_cf, n_lh, n_lc = _inproj_call(
            x, sc1, sh1, lw["nmix"], lw["wcat"], conv_in, (n_cf, n_lh, n_lc), lw["cf_dw"], lw["cf_vec"],
            lw["lru_cw"], lw["lru_vec"], lw["wax"], l, mix_rows)
        y_hg, n_hg = _hgrn_call(p, s_hg, n_hg, wt["hg_lower"], lw["hg_gn"], te, l, rec_rows, nlong, chunk, seg)
        y_rw, n_rw, n_sh = _rwkv_call(p, s_rw, s_shift, (n_rw, n_sh), lw["mu"], lw["rw_vec"], lw["wup"], ones_bd,
                                      te, l, rec_rows, nlong, chunk, seg)
        x1, h2 = _merge_call((y_hg, y_rw, y_cf, y_lr), x, (sc1, sh1, g1, sc2, sh2), lw["nmix"], lw["nmlp"],
                             lw["wg"], lw["bg"], lw["wb"], lw["wo"])
        x = _mlp_call(h2, x1, g2, fg, lw["w1"], lw["w2"], final=(l == nl - 1))
    return x, [n_hg, n_rw, n_sh[:, :, 0, :], n_cf, n_lh[:, :, 0, :], n_lc]


def _run(x_prompt, x_sample, sample_states, c_prompt, c_sample, wt):
    nl = wt["w_in"].shape[0]
    layers = [_prep_layer(wt, l) for l in range(nl)]
    head = jnp.arange(W, dtype=jnp.int32) // RW_D
    ones_bd = _bf((head[:, None] == head[None, :]).astype(F32))
    tp = x_prompt.shape[1]
    ts = x_sample.shape[1]
    rows_p = min(256, tp)
    chunk_p = min(64, tp)
    bp = x_prompt.shape[0]
    mod = _mod_call(jnp.concatenate([c_prompt, c_sample], axis=0), wt["ada_w"], wt["ada_b"])
    y_p, st_p = _trunk(x_prompt, mod[:, :, :bp], None, wt, layers, ones_bd, rows_p, rows_p // 4, 8, chunk_p,
                       chunk_p)
    y_s, st_s = _trunk(x_sample, mod[:, :, bp:], sample_states, wt, layers, ones_bd, rows_p, 16 * ts, 1, 8 * ts, ts)
    return (y_p, y_s, *st_p, *st_s)


def kernel(x_prompt, x_sample, state_hgrn, state_rwkv, state_rwkv_shift, state_conv, state_lru_h, state_lru_conv, c_prompt, c_sample, ada_w, ada_b, norm_mix_g, norm_mlp_g, norm_final_g, w_in, hg_lower, hg_norm_g, rw_mu, rw_w0, rw_w_up, rw_a0, rw_a_up, rw_g_up, rw_k_k, rw_k_a, rw_r_k, rw_ln_g, rw_ln_b, cf_dw, cf_dw_b, cf_ln_g, cf_ln_b, lru_conv_w, lru_conv_b, lru_wa, lru_ba, lru_wx, lru_bx, lru_lambda, w_branch, w_gate, b_gate, w_out, w_mlp1, w_mlp2):
    wt = dict(ada_w=ada_w, ada_b=ada_b, norm_mix_g=norm_mix_g, norm_mlp_g=norm_mlp_g,
              norm_final_g=norm_final_g, w_in=w_in, hg_lower=hg_lower, hg_norm_g=hg_norm_g, rw_mu=rw_mu,
              rw_w0=rw_w0, rw_w_up=rw_w_up, rw_a0=rw_a0, rw_a_up=rw_a_up, rw_g_up=rw_g_up, rw_k_k=rw_k_k,
              rw_k_a=rw_k_a, rw_r_k=rw_r_k, rw_ln_g=rw_ln_g, rw_ln_b=rw_ln_b, cf_dw=cf_dw, cf_dw_b=cf_dw_b,
              cf_ln_g=cf_ln_g, cf_ln_b=cf_ln_b, lru_conv_w=lru_conv_w, lru_conv_b=lru_conv_b, lru_wa=lru_wa,
              lru_ba=lru_ba, lru_wx=lru_wx, lru_bx=lru_bx, lru_lambda=lru_lambda, w_branch=w_branch,
              w_gate=w_gate, b_gate=b_gate, w_out=w_out, w_mlp1=w_mlp1, w_mlp2=w_mlp2)
    sample_states = (state_hgrn, state_rwkv, state_rwkv_shift, state_conv, state_lru_h, state_lru_conv)
    return _run(x_prompt, x_sample, sample_states, c_prompt, c_sample, wt)
```

```python
import functools
import math

import jax
import jax.numpy as jnp
from jax import lax
from jax.experimental import pallas as pl
from jax.experimental.pallas import tpu as pltpu

D = 1024
W = 512
HG_H = 4
HG_D = 128
RW_H = 8
RW_D = 64
RW_COLS = 1792
RW_PAD = 2048
CF_K = 31
LRU_K = 4
HID = 4096
EPS = 1e-6
RW_GN_EPS = 64e-5
RW_DECAY = 0.606531
LRU_C = 8.0
LANES = 128

P_COLS = 6144
SEQ_UNROLL = 2
VMEM_LIMIT = 56 * 1024 * 1024

F32 = jnp.float32
BF16 = jnp.bfloat16


def _bf(x):
    return x.astype(BF16)


def _mm(a, b):
    return jnp.dot(_bf(a), _bf(b), preferred_element_type=F32)


def _mm_nt(a, b):
    return lax.dot_general(_bf(a), _bf(b), (((1,), (1,)), ((), ())), preferred_element_type=F32)


def _mm_tn(a, b):
    return lax.dot_general(_bf(a), _bf(b), (((0,), (0,)), ((), ())), preferred_element_type=F32)


def _sigmoid(x):
    return 1.0 / (1.0 + jnp.exp(-x))


def _silu(x):
    return x * _sigmoid(x)


def _local_row(shape, seg):
    return jnp.bitwise_and(lax.broadcasted_iota(jnp.int32, shape, 0), seg - 1)


def _shift_rows(x, d, fill, seg):
    return jnp.where(_local_row(x.shape, seg) >= d, pltpu.roll(x, d, axis=0), fill)


def _scan_mats(chunk, seg):
    r = jnp.arange(chunk, dtype=jnp.int32)
    same = (r[:, None] // seg) == (r[None, :] // seg)
    tri = same & (r[None, :] <= r[:, None])
    return _bf(jnp.concatenate([tri, same], axis=0).astype(F32))


def _cumsum_last(x, te):
    c = x.shape[0]
    p1 = _bf(x)
    r1 = x - p1.astype(F32)
    p2 = _bf(r1)
    p3 = _bf(r1 - p2.astype(F32))
    out = (jnp.dot(te, p1, preferred_element_type=F32) + jnp.dot(te, p2, preferred_element_type=F32)
           + jnp.dot(te, p3, preferred_element_type=F32))
    return out[:c], out[c:]


def _seg_id(shape, axis, log2_seg):
    return lax.shift_right_logical(lax.broadcasted_iota(jnp.int32, shape, axis), log2_seg)


def _headsum(x, ones_bd):
    hi = _bf(x)
    lo = _bf(x - hi.astype(F32))
    return (jnp.dot(hi, ones_bd, preferred_element_type=F32)
            + jnp.dot(lo, ones_bd, preferred_element_type=F32))


def _cparams(sem):
    return pltpu.CompilerParams(dimension_semantics=sem, vmem_limit_bytes=VMEM_LIMIT)


def _tile(bsz, t, rows):
    if t >= rows:
        return 1, rows
    return min(bsz, rows // t), t


def _state_specs(l, sb, tail):
    zeros = (0,) * len(tail)
    in_spec = pl.BlockSpec((None, sb) + tail, lambda i, j: (l, i) + zeros)
    prev_spec = pl.BlockSpec((l, sb) + tail, lambda i, j: (0, i) + zeros)
    out_spec = pl.BlockSpec((l + 1, sb) + tail, lambda i, j: (0, i) + zeros)
    return in_spec, prev_spec, out_spec


def _emit_state(l, out_ref, prev_ref, new):
    for q in range(l):
        out_ref[q] = prev_ref[q]
    out_ref[l] = new


def _mod_kernel(c_ref, w_ref, b_ref, o_ref):
    c = c_ref[...]
    o_ref[0, 0] = _mm(_silu(c), w_ref[0]) + b_ref[0, 0]


def _mod_call(c, ada_w, ada_b):
    nl = ada_w.shape[0]
    bsz = c.shape[0]
    return pl.pallas_call(
        _mod_kernel,
        out_shape=jax.ShapeDtypeStruct((nl, 6, bsz, D), F32),
        grid=(nl, 6),
        in_specs=[
            pl.BlockSpec((bsz, D), lambda l, j: (0, 0)),
            pl.BlockSpec((1, D, D), lambda l, j: (l, 0, j)),
            pl.BlockSpec((1, 1, 1, D), lambda l, j: (l, j, 0, 0)),
        ],
        out_specs=pl.BlockSpec((1, 1, bsz, D), lambda l, j: (l, j, 0, 0)),
        compiler_params=_cparams(("arbitrary", "arbitrary")),
        name="adaln_mod",
    )(c, ada_w, ada_b.reshape(nl, 6, 1, D))


def _adaln(x, g, sc, sh):
    ms = jnp.mean(x * x, axis=-1, keepdims=True)
    y = x * lax.rsqrt(ms + EPS) * g
    return y * (1.0 + sc) + sh


CF_PAD = 32
CF_RB = 64
SUBLANES = 8
LRU_PAD = 8
REC_COLS = 4096
REC_PIECE = 1024


def _gelu_tanh(x):
    return 0.5 * x * (1.0 + jnp.tanh(0.7978845608028654 * (x + 0.044715 * (x * x * x))))


def _conformer_prep(pcf, s, slot, ext, shifted):
    tt = pcf.shape[0]
    span = tt + CF_PAD - SUBLANES
    ext[s, CF_PAD:CF_PAD + tt, :] = pcf[:, 0:W] * _sigmoid(pcf[:, W:2 * W])
    for q in range(1, SUBLANES):
        shifted[slot, q, 0:span, :] = ext[s, q:q + span, :]


def _rows(start, size):
    if isinstance(start, int):
        return pl.ds(start, size)
    return pl.ds(pl.multiple_of(start, SUBLANES), size)


def _conformer_rows(r0, rb, s, slot, ext, shifted, y_ref, dw, bias, ln_g, ln_b):
    off = CF_PAD - (CF_K - 1)
    acc = jnp.zeros((rb, W), F32) + bias
    for j in range(CF_K):
        a8, q = divmod(off + j, SUBLANES)
        rows = _rows(r0 + a8 * SUBLANES, rb)
        tap = ext[s, rows, :] if q == 0 else shifted[slot, q, rows, :]
        acc = acc + dw[j:j + 1, :] * tap
    mean = jnp.mean(acc, axis=-1, keepdims=True)
    xc = acc - mean
    var = jnp.mean(xc * xc, axis=-1, keepdims=True)
    yn = xc * lax.rsqrt(var + 1e-5) * ln_g + ln_b
    y_ref[s, _rows(r0, rb), :] = _silu(yn)


def _group_prefix(a, b):
    d = 1
    while d < SUBLANES:
        a_s = _shift_rows(a, d, 1.0, SUBLANES)
        b_s = _shift_rows(b, d, 0.0, SUBLANES)
        b = a * b_s + b
        a = a * a_s
        d *= 2
    return a, b


def _linear_scan(a, b, h):
    n = a.shape[0]
    a, b = _group_prefix(a, b)
    out = []
    for g in range(n // SUBLANES):
        hs = a[g * SUBLANES:(g + 1) * SUBLANES] * h + b[g * SUBLANES:(g + 1) * SUBLANES]
        h = hs[SUBLANES - 1:SUBLANES, :]
        out.append(hs)
    return (jnp.concatenate(out, axis=0) if len(out) > 1 else out[0]), h


def _lru_conv(s, tt, ext, cw, cb):
    off = LRU_PAD - (LRU_K - 1)
    xc = jnp.zeros((tt, W), F32) + cb
    for j in range(LRU_K):
        xc = xc + cw[j:j + 1, :] * ext[s, off + j:off + j + tt, :]
    return xc


def _lru_rows(xc, gate, h, ba, bx, sp, wax):
    pre = jnp.dot(_bf(xc), wax, preferred_element_type=F32)
    rg = _sigmoid(pre[:, 0:W] + ba)
    ig = _sigmoid(pre[:, W:2 * W] + bx)
    a = jnp.exp(-LRU_C * rg * sp)
    hs, h = _linear_scan(a, jnp.sqrt(1.0 - a * a) * (ig * xc), h)
    return hs * _gelu_tanh(gate), h


def _inproj_kernel(cfg, *refs):
    l, zero_init = cfg
    refs = list(refs)
    x_ref, sc_ref, sh_ref, g_ref, w_ref = [refs.pop(0) for _ in range(5)]
    cf0_ref, lh0_ref, lc0_ref = (None,) * 3 if zero_init else [refs.pop(0) for _ in range(3)]
    dw_ref, cfv_ref, cw_ref, lrv_ref, wax_ref = [refs.pop(0) for _ in range(5)]
    cfp_ref, lhp_ref, lcp_ref = [refs.pop(0) for _ in range(3)] if l > 0 else (None,) * 3
    p_ref, ycf_ref, ylr_ref, cfo_ref, lho_ref, lco_ref = [refs.pop(0) for _ in range(6)]
    cf_ext, shifted, lr_ext, hcar, pc_scr, h_scr, xc_scr = refs
    sb, tt, _ = x_ref.shape
    tm = sb * tt
    tstep = pl.program_id(1)
    cf_off = CF_PAD - (CF_K - 1)
    lr_off = LRU_PAD - (LRU_K - 1)

    @pl.when(tstep == 0)
    def _():
        if zero_init:
            cf_ext[:, 0:CF_PAD, :] = jnp.zeros((sb, CF_PAD, W), F32)
            lr_ext[:, 0:LRU_PAD, :] = jnp.zeros((sb, LRU_PAD, W), F32)
            hcar[...] = jnp.zeros_like(hcar)
        else:
            cf_ext[:, cf_off:CF_PAD, :] = cf0_ref[...]
            lr_ext[:, lr_off:LRU_PAD, :] = lc0_ref[...]
            hcar[...] = lh0_ref[...]

    h_scr[...] = _bf(_adaln(x_ref[...], g_ref[...], sc_ref[...], sh_ref[...]).reshape(tm, D))
    pc_scr[...] = jnp.dot(h_scr[...], w_ref[:, REC_COLS:P_COLS], preferred_element_type=F32).reshape(sb, tt, 4 * W)

    def rec_piece(n):
        cols = slice(n * REC_PIECE, (n + 1) * REC_PIECE)
        p_ref[:, :, cols] = jnp.dot(h_scr[...], w_ref[:, cols], preferred_element_type=F32).reshape(sb, tt, REC_PIECE)

    cfv = cfv_ref[...]
    dw = dw_ref[...]
    lrv = lrv_ref[...]
    lam = lrv[3:4]
    sp = jnp.maximum(-lam, 0.0) + jnp.log1p(jnp.exp(-jnp.abs(lam)))
    cw = cw_ref[...]
    wax = wax_ref[...]
    nslot = shifted.shape[0]

    rb = min(CF_RB, tt)

    def prep(s, slot):
        _conformer_prep(pc_scr[s, :, 0:2 * W], s, slot, cf_ext, shifted)
        lr_ext[s, LRU_PAD:LRU_PAD + tt, :] = pc_scr[s, :, 2 * W:3 * W]
        xc_scr[s] = _lru_conv(s, tt, lr_ext, cw, lrv[0:1])

    def row_block(s, slot, r0, hstate):
        _conformer_rows(r0, rb, s, slot, cf_ext, shifted, ycf_ref, dw, cfv[0:1], cfv[1:2], cfv[2:3])
        rows = _rows(r0, rb)
        y, hstate = _lru_rows(xc_scr[s, rows, :], pc_scr[s, rows, 3 * W:4 * W], hstate, lrv[1:2], lrv[2:3], sp, wax)
        ylr_ref[s, rows, :] = y
        return hstate

    def finish(s, hstate):
        hcar[s] = hstate
        cf_tail = cf_ext[s, tt:tt + CF_PAD, :]
        cf_ext[s, 0:CF_PAD, :] = cf_tail
        lr_tail = lr_ext[s, tt:tt + LRU_PAD, :]
        lr_ext[s, 0:LRU_PAD, :] = lr_tail

    for n in range(REC_COLS // REC_PIECE):
        rec_piece(n)

    def all_short():
        pc = pc_scr[...]
        cf_ext[:, CF_PAD:CF_PAD + tt, :] = pc[:, :, 0:W] * _sigmoid(pc[:, :, W:2 * W])
        acc = jnp.zeros((sb, tt, W), F32) + cfv[0:1]
        for j in range(CF_K):
            acc = acc + dw[j:j + 1, :] * cf_ext[:, cf_off + j:cf_off + j + tt, :]
        mean = jnp.mean(acc, axis=-1, keepdims=True)
        xn = acc - mean
        var = jnp.mean(xn * xn, axis=-1, keepdims=True)
        ycf_ref[...] = _silu(xn * lax.rsqrt(var + 1e-5) * cfv[1:2] + cfv[2:3])
        lr_ext[:, LRU_PAD:LRU_PAD + tt, :] = pc[:, :, 2 * W:3 * W]
        xc = jnp.zeros((sb, tt, W), F32) + lrv[0:1]
        for j in range(LRU_K):
            xc = xc + cw[j:j + 1, :] * lr_ext[:, lr_off + j:lr_off + j + tt, :]
        xc = xc.reshape(tm, W)
        pre = jnp.dot(_bf(xc), wax, preferred_element_type=F32)
        rg = _sigmoid(pre[:, 0:W] + lrv[1:2])
        ig = _sigmoid(pre[:, W:2 * W] + lrv[2:3])
        a = jnp.exp(-LRU_C * rg * sp)
        a, b = _group_prefix(a, jnp.sqrt(1.0 - a * a) * (ig * xc))
        hs = a.reshape(sb, tt, W) * hcar[...] + b.reshape(sb, tt, W)
        hcar[...] = hs[:, tt - 1:tt, :]
        ylr_ref[...] = hs * _gelu_tanh(pc[:, :, 3 * W:4 * W])
        cf_tail = cf_ext[:, tt:tt + CF_PAD, :]
        cf_ext[:, 0:CF_PAD, :] = cf_tail
        lr_tail = lr_ext[:, tt:tt + LRU_PAD, :]
        lr_ext[:, 0:LRU_PAD, :] = lr_tail

    def body(i, carry):
        for slot in range(nslot):
            s = i * nslot + slot
            prep(s, slot)
            hstate = hcar[s]
            for r0 in range(0, tt, rb):
                hstate = row_block(s, slot, r0, hstate)
            finish(s, hstate)
        return carry

    if tt == SUBLANES:
        all_short()
    else:
        lax.fori_loop(0, sb // nslot, body, 0)

    @pl.when(tstep == pl.num_programs(1) - 1)
    def _():
        _emit_state(l, cfo_ref, cfp_ref, cf_ext[:, cf_off:CF_PAD, :])
        _emit_state(l, lho_ref, lhp_ref, hcar[...])
        _emit_state(l, lco_ref, lcp_ref, lr_ext[:, lr_off:LRU_PAD, :])


def _inproj_call(x, sc, sh, g, wcat, conv_in, conv_prev, dw, cfv, cw, lrv, wax, l, rows=256):
    bsz, t, _ = x.shape
    sb, tt = _tile(bsz, t, rows)
    zero_init = conv_in is None
    tails = ((CF_K - 1, W), (1, W), (LRU_K - 1, W))
    sspecs = [_state_specs(l, sb, tail) for tail in tails]
    tok = lambda width: pl.BlockSpec((sb, tt, width), lambda i, j: (i, j, 0))
    seq = pl.BlockSpec((sb, 1, D), lambda i, j: (i, 0, 0))
    const = lambda shape: pl.BlockSpec(shape, lambda i, j: (0,) * len(shape), pipeline_mode=pl.Buffered(1))
    args = ([x, sc, sh, g, wcat] + ([] if zero_init else list(conv_in)) + [dw, cfv, cw, lrv, wax]
            + (list(conv_prev) if l > 0 else []))
    specs = ([tok(D), seq, seq, const((1, D)), const((D, P_COLS))]
             + ([] if zero_init else [sp[0] for sp in sspecs])
             + [const((32, W)), const((8, W)), const((8, W)), const((8, W)), const((W, 2 * W))]
             + ([sp[1] for sp in sspecs] if l > 0 else []))
    nslot = min(sb, SEQ_UNROLL)
    return pl.pallas_call(
        functools.partial(_inproj_kernel, (l, zero_init)),
        out_shape=(jax.ShapeDtypeStruct((bsz, t, REC_COLS), F32),
                   jax.ShapeDtypeStruct((bsz, t, W), F32),
                   jax.ShapeDtypeStruct((bsz, t, W), F32))
        + tuple(jax.ShapeDtypeStruct((l + 1, bsz) + tail, F32) for tail in tails),
        grid=(bsz // sb, t // tt),
        in_specs=specs,
        out_specs=(tok(REC_COLS), tok(W), tok(W)) + tuple(sp[2] for sp in sspecs),
        scratch_shapes=[pltpu.VMEM((sb, CF_PAD + tt, W), F32),
                        pltpu.VMEM((nslot, SUBLANES, CF_PAD + tt, W), F32),
                        pltpu.VMEM((sb, LRU_PAD + tt, W), F32),
                        pltpu.VMEM((sb, 1, W), F32),
                        pltpu.VMEM((sb, tt, 4 * W), F32),
                        pltpu.VMEM((sb * tt, D), BF16),
                        pltpu.VMEM((sb, tt, W), F32)],
        compiler_params=_cparams(("arbitrary", "arbitrary")),
        name="inproj",
    )(*args)


HG_SUB = 16


def _hgrn_chunk(qr, fz, iv, og, lb, gn, te, states, seg):
    c, width = qr.shape
    nseg = c // seg
    heads = range(width // HG_D)
    q = _silu(qr)
    f = lb + (1.0 - lb) * _sigmoid(fz)
    logf = jnp.log(f)
    kf = (1.0 - lb) * _sigmoid(-fz)
    b, b_last = _cumsum_last(logf, te)
    qe = q * jnp.exp(b)
    kdec = kf * jnp.exp(b_last - b)
    e_last = jnp.exp(b_last)
    sl = [slice(h * HG_D, (h + 1) * HG_D) for h in heads]

    if nseg == 1:
        sub = min(HG_SUB, c)
        nsub = c // sub
        o_inter = [_mm_nt(qe[:, sl[h]], states[h]) for h in heads]
        new_states = [states[h] * e_last[0:1, sl[h]] + _mm_tn(iv[:, sl[h]], kdec[:, sl[h]]) for h in heads]
        pieces = [[] for _ in heads]
        for i in range(nsub):
            r0 = i * sub
            m = b[r0 - 1:r0, :] if i > 0 else jnp.zeros((1, width), F32)
            qs = q[r0:r0 + sub] * jnp.exp(b[r0:r0 + sub] - m)
            kd = kf[r0:r0 + sub] * jnp.exp(jnp.minimum(m - b[r0:r0 + sub], 80.0))
            if i > 0:
                kall = jnp.concatenate([kf[:r0] * jnp.exp(m - b[:r0]), kd], axis=0)
            else:
                kall = kd
            row = lax.broadcasted_iota(jnp.int32, (sub, r0 + sub), 0) + r0
            col = lax.broadcasted_iota(jnp.int32, (sub, r0 + sub), 1)
            causal = col <= row
            scs = [jnp.where(causal, _mm_nt(qs[:, sl[h]], kall[:, sl[h]]), 0.0) for h in heads]
            for h in heads:
                pieces[h].append(_mm(scs[h], iv[:r0 + sub, sl[h]]))
        o_intra = [jnp.concatenate(pieces[h], axis=0) if nsub > 1 else pieces[h][0] for h in heads]
    else:
        lg = int(math.log2(seg))
        kd = kf * jnp.exp(jnp.minimum(-b, 80.0))
        rr = lax.broadcasted_iota(jnp.int32, (c, c), 0)
        cc = lax.broadcasted_iota(jnp.int32, (c, c), 1)
        causal = (lax.shift_right_logical(rr, lg) == lax.shift_right_logical(cc, lg)) & (cc <= rr)
        shape_x = (c, nseg * HG_D)
        blockmask = _seg_id(shape_x, 0, lg) == _seg_id(shape_x, 1, 7)
        first = blockmask & (_local_row(shape_x, seg) == 0)
        spread = lambda v, mask: jnp.where(mask, jnp.concatenate([v] * nseg, axis=1), 0.0)
        ones_cv = jnp.ones((3 * c, HG_D), BF16)
        sflat = [states[h].reshape(nseg * HG_D, HG_D) for h in heads]
        o_inter = [_mm(spread(qe[:, sl[h]], blockmask), sflat[h]) for h in heads]
        scs = [jnp.where(causal, _mm_nt(qe[:, sl[h]], kd[:, sl[h]]), 0.0) for h in heads]
        o_intra = [_mm(scs[h], iv[:, sl[h]]) for h in heads]
        new_states = []
        for h in heads:
            ex = spread(e_last[:, sl[h]], first)
            p1 = _bf(ex)
            r1 = ex - p1.astype(F32)
            p2 = _bf(r1)
            p3 = _bf(r1 - p2.astype(F32))
            dcol = lax.dot_general(jnp.concatenate([p1, p2, p3], axis=0), ones_cv, (((0,), (0,)), ((), ())),
                                   preferred_element_type=F32)
            upd = _mm_tn(spread(kdec[:, sl[h]], blockmask), iv[:, sl[h]])
            new_states.append((sflat[h] * dcol + upd).reshape(nseg, HG_D, HG_D))
    outs = []
    for h in heads:
        o = o_inter[h] + o_intra[h]
        outs.append(o * lax.rsqrt(jnp.mean(o * o, axis=-1, keepdims=True) + EPS))
    y = jnp.concatenate(outs, axis=1) * gn * _silu(og)
    return y, new_states


def _hgrn_kernel(cfg, *refs):
    l, chunk, seg, zero_init = cfg
    refs = list(refs)
    p_ref = refs.pop(0)
    s0_ref = None if zero_init else refs.pop(0)
    lower_ref, gn_ref, te_ref = refs.pop(0), refs.pop(0), refs.pop(0)
    prev_ref = refs.pop(0) if l > 0 else None
    y_ref, sout_ref, s_scr = refs
    sb, tt, _ = p_ref.shape
    nseg = chunk // seg
    tstep = pl.program_id(1)

    @pl.when(tstep == 0)
    def _():
        if zero_init:
            s_scr[...] = jnp.zeros_like(s_scr)
        elif nseg > 1:
            s_scr[...] = s0_ref[...]
        else:
            def init(i, carry):
                for h in range(HG_H):
                    s_scr[i, h] = s0_ref[i, h].T
                return carry
            lax.fori_loop(0, sb, init, 0)

    low = lower_ref[...]
    e = jnp.exp(low - jnp.max(low, axis=0, keepdims=True))
    sm = e / jnp.sum(e, axis=0, keepdims=True)
    lb = jnp.sum(sm[:l + 1], axis=0, keepdims=True) - sm[0:1]
    gn = gn_ref[...]
    te = te_ref[...]

    def body_long(i, carry):
        rs = pl.ds(pl.multiple_of(i * chunk, chunk), chunk)
        cols = [jnp.concatenate([p_ref[q, rs, j * W:(j + 1) * W] for q in range(sb)], axis=1) for j in range(4)]
        wide = lambda v: jnp.concatenate([v] * sb, axis=1)
        y, states = _hgrn_chunk(*cols, wide(lb), wide(gn), te,
                                [s_scr[q, h] for q in range(sb) for h in range(HG_H)], seg)
        for q in range(sb):
            for h in range(HG_H):
                s_scr[q, h] = states[q * HG_H + h]
            y_ref[q, rs, :] = y[:, q * W:(q + 1) * W]
        return carry

    def all_short():
        nch = sb * tt // chunk
        grp = [slice(c * nseg, (c + 1) * nseg) for c in range(nch)]
        blks = [p_ref[g].reshape(chunk, 4 * W) for g in grp]
        cols = [jnp.concatenate([b[:, j * W:(j + 1) * W] for b in blks], axis=1) for j in range(4)]
        wide = lambda v: jnp.concatenate([v] * nch, axis=1)
        y, states = _hgrn_chunk(*cols, wide(lb), wide(gn), te,
                                [s_scr[g, h] for g in grp for h in range(HG_H)], seg)
        for c, g in enumerate(grp):
            for h in range(HG_H):
                s_scr[g, h] = states[c * HG_H + h]
            y_ref[g] = y[:, c * W:(c + 1) * W].reshape(nseg, seg, W)

    if nseg == 1:
        lax.fori_loop(0, tt // chunk, body_long, 0)
    else:
        all_short()

    @pl.when(tstep == pl.num_programs(1) - 1)
    def _():
        for q in range(l):
            sout_ref[q] = prev_ref[q]
        if nseg > 1:
            sout_ref[l] = s_scr[...]
        else:
            def fin(i, carry):
                for h in range(HG_H):
                    sout_ref[l, i, h] = s_scr[i, h].T
                return carry
            lax.fori_loop(0, sb, fin, 0)


def _rec_tile(bsz, t, rows, nlong):
    sb, tt = _tile(bsz, t, rows)
    return (min(nlong, bsz), tt) if t >= rows else (sb, tt)


def _hgrn_call(p, s_in, prev, hg_lower, gn, te, l, rows, nlong, chunk, seg):
    bsz, t, _ = p.shape
    sb, tt = _rec_tile(bsz, t, rows, nlong)
    zero_init = s_in is None
    in_spec, prev_spec, out_spec = _state_specs(l, sb, (HG_H, HG_D, HG_D))
    args = [p] + ([] if zero_init else [s_in]) + [hg_lower, gn, te] + ([prev] if l > 0 else [])
    specs = ([pl.BlockSpec((sb, tt, 4 * W), lambda i, j: (i, j, 0))] + ([] if zero_init else [in_spec])
             + [pl.BlockSpec(hg_lower.shape, lambda i, j: (0, 0)), pl.BlockSpec((1, W), lambda i, j: (0, 0)),
                pl.BlockSpec(te.shape, lambda i, j: (0, 0))]
             + ([prev_spec] if l > 0 else []))
    return pl.pallas_call(
        functools.partial(_hgrn_kernel, (l, chunk, seg, zero_init)),
        out_shape=(jax.ShapeDtypeStruct((bsz, t, W), F32),
                   jax.ShapeDtypeStruct((l + 1, bsz, HG_H, HG_D, HG_D), F32)),
        grid=(bsz // sb, t // tt),
        in_specs=specs,
        out_specs=(pl.BlockSpec((sb, tt, W), lambda i, j: (i, j, 0)), out_spec),
        scratch_shapes=[pltpu.VMEM((sb, HG_H, HG_D, HG_D), F32)],
        compiler_params=_cparams(("arbitrary", "arbitrary")),
        name="hgrn2",
    )(*args)


def _pick64(full, rowseg):
    nb = full.shape[1] // LANES
    half = lax.shift_right_logical(rowseg, 1)
    sel = full[:, 0:LANES]
    for j in range(1, nb):
        sel = jnp.where(half == j, full[:, j * LANES:(j + 1) * LANES], sel)
    sel = jnp.where(jnp.bitwise_and(rowseg, 1) == 1, pltpu.roll(sel, RW_D, axis=1), sel)
    return sel[:, 0:RW_D]


def _rwkv_recur(at, rt, bi, ki, bd, kd, v, states, pc, seg):
    c = at.shape[0]
    nseg = c // seg
    lg = int(math.log2(seg))
    rr = lax.broadcasted_iota(jnp.int32, (c, c), 0)
    cc = lax.broadcasted_iota(jnp.int32, (c, c), 1)
    if nseg == 1:
        strict = rr > cc
        incl = rr >= cc
    else:
        same = lax.shift_right_logical(rr, lg) == lax.shift_right_logical(cc, lg)
        strict = same & (rr > cc)
        incl = same & (rr >= cc)
        r2 = jnp.bitwise_and(lax.broadcasted_iota(jnp.int32, (2 * c, LANES), 0), c - 1)
        rowseg = lax.shift_right_logical(r2, lg)
        rb = jnp.bitwise_and(lax.broadcasted_iota(jnp.int32, (2 * c, nseg * RW_D), 0), c - 1)
        blockmask = lax.shift_right_logical(rb, lg) == _seg_id((2 * c, nseg * RW_D), 1, 6)
    heads = range(at.shape[1] // RW_D)
    sl = [slice(h * RW_D, (h + 1) * RW_D) for h in heads]
    ar = [jnp.concatenate([at[:, sl[h]], rt[:, sl[h]]], axis=0) for h in heads]
    bk = [jnp.concatenate([bi[:, sl[h]], ki[:, sl[h]]], axis=0) for h in heads]
    g = [_mm_nt(ar[h], bk[h]) for h in heads]
    if nseg == 1:
        a_s = [_mm_nt(ar[h], states[h]) for h in heads]
    else:
        a_s = [_pick64(_mm_nt(ar[h], states[h].reshape(nseg * RW_D, RW_D)), rowseg) for h in heads]
    vh = [v[:, sl[h]] for h in heads]
    lp = [jnp.where(strict, g[h][:c, :c], 0.0) for h in heads]
    x = [a_s[h][:c] + _mm(jnp.where(strict, g[h][:c, c:], 0.0), vh[h]) for h in heads]
    for j in range(lg):
        x = [x[h] + _mm(lp[h], x[h]) for h in heads]
        if j < lg - 1:
            lp = [_mm(lp[h], lp[h]) for h in heads]
    uv = [jnp.concatenate([x[h], vh[h]], axis=0) for h in heads]
    mrbk = [jnp.concatenate([jnp.where(incl, g[h][c:, :c], 0.0), jnp.where(incl, g[h][c:, c:], 0.0)], axis=1)
            for h in heads]
    ys = [a_s[h][c:] + _mm(mrbk[h], uv[h]) for h in heads]
    bkd = [jnp.concatenate([bd[:, sl[h]], kd[:, sl[h]]], axis=0) for h in heads]
    if nseg == 1:
        new_states = [states[h] * pc[:, sl[h]] + _mm_tn(uv[h], bkd[h]) for h in heads]
    else:
        new_states = []
        for h in heads:
            u2 = jnp.concatenate([uv[h], uv[h]], axis=1)
            uvexp = jnp.where(blockmask, jnp.concatenate([u2] * (nseg // 2), axis=1), 0.0)
            upd = _mm_tn(uvexp, bkd[h]).reshape(nseg, RW_D, RW_D)
            new_states.append(states[h] * pc[:, :, sl[h]] + upd)
    return ys, new_states


def _rwkv_kernel(cfg, *refs):
    l, chunk, seg, zero_init = cfg
    refs = list(refs)
    p_ref = refs.pop(0)
    s0_ref, sh0_ref = (None, None) if zero_init else (refs.pop(0), refs.pop(0))
    mu_ref, vec_ref, wup_ref, ones_ref, te_ref = [refs.pop(0) for _ in range(5)]
    sprev_ref, shprev_ref = (refs.pop(0), refs.pop(0)) if l > 0 else (None, None)
    y_ref, sout_ref, shout_ref = refs.pop(0), refs.pop(0), refs.pop(0)
    s_scr, prev_scr, at_s, rt_s, bi_s, ki_s, bd_s, kd_s, v_s, lpl_s, yr_s, bon_s, gate_s = refs
    sb, tt, _ = p_ref.shape
    rows = sb * tt
    nseg = chunk // seg
    tstep = pl.program_id(1)

    @pl.when(tstep == 0)
    def _():
        if zero_init:
            s_scr[...] = jnp.zeros_like(s_scr)
            prev_scr[...] = jnp.zeros_like(prev_scr)
        else:
            s_scr[...] = s0_ref[...]
            prev_scr[:, :, 0:RW_COLS] = sh0_ref[...]
            prev_scr[:, :, RW_COLS:RW_PAD] = jnp.zeros((sb, 1, RW_PAD - RW_COLS), F32)

    vec = vec_ref[...]
    w0, a0, k_k, k_a, r_k, ln_g, ln_b = [vec[i:i + 1, :] for i in range(7)]
    ones_bd = ones_ref[...]

    rw3 = p_ref[...]
    rw = rw3.reshape(rows, RW_PAD)
    prev_rows = jnp.broadcast_to(prev_scr[...], (sb, tt, RW_PAD)).reshape(rows, RW_PAD)
    prev = jnp.where(_local_row(rw.shape, tt) == 0, prev_rows, pltpu.roll(rw, 1, axis=0))
    prev_scr[...] = rw3[:, tt - 1:tt, :]
    rwm = rw + (prev - rw) * mu_ref[...]
    r, k, v = rwm[:, 0:W], rwm[:, W:2 * W], rwm[:, 2 * W:3 * W]
    lr = rwm[:, 3 * W:3 * W + 256]
    lane = lax.broadcasted_iota(jnp.int32, lr.shape, 1)
    act = jnp.where(lane < 64, jnp.tanh(lr), jnp.where(lane < 128, lr, _sigmoid(lr)))
    up = jnp.dot(_bf(act), wup_ref[...], preferred_element_type=F32)
    log_w = -RW_DECAY * _sigmoid(w0 + up[:, 0:W])
    a = _sigmoid(a0 + up[:, W:2 * W])
    gate_s[...] = up[:, 2 * W:3 * W]
    kk = k * k_k
    kk = kk / jnp.maximum(jnp.sqrt(_headsum(kk * kk, ones_bd)), 1e-12)
    k2 = k * (1.0 + (a - 1.0) * k_a)
    kka = kk * a
    bon_s[...] = _headsum(r * k2 * r_k, ones_bd) * v
    v_s[...] = v
    te = te_ref[...]
    scans = [_cumsum_last(log_w[i * chunk:(i + 1) * chunk], te) for i in range(rows // chunk)]
    logp = jnp.concatenate([s[0] for s in scans], axis=0) if len(scans) > 1 else scans[0][0]
    lpl = jnp.concatenate([s[1] for s in scans], axis=0) if len(scans) > 1 else scans[0][1]
    lpl_s[...] = lpl
    at_s[...] = -kk * jnp.exp(logp - log_w)
    rt_s[...] = r * jnp.exp(logp)
    einv = jnp.exp(-logp)
    bi_s[...] = kka * einv
    ki_s[...] = k2 * einv
    elast = jnp.exp(lpl - logp)
    bd_s[...] = kka * elast
    kd_s[...] = k2 * elast

    def body_long(ch, carry):
        rss = [pl.ds(pl.multiple_of(q * tt + ch * chunk, chunk), chunk) for q in range(sb)]
        wide = lambda ref: jnp.concatenate([ref[rs, :] for rs in rss], axis=1)
        pc = jnp.exp(jnp.concatenate([lpl_s[pl.ds(rs.start, 1), :] for rs in rss], axis=1))
        ys, states = _rwkv_recur(wide(at_s), wide(rt_s), wide(bi_s), wide(ki_s), wide(bd_s), wide(kd_s),
                                 wide(v_s), [s_scr[q, h] for q in range(sb) for h in range(RW_H)], pc, seg)
        for q in range(sb):
            for h in range(RW_H):
                s_scr[q, h] = states[q * RW_H + h]
            yr_s[rss[q], :] = jnp.concatenate(ys[q * RW_H:(q + 1) * RW_H], axis=1)
        return carry

    def all_short():
        nch = rows // chunk
        rss = [slice(c * chunk, (c + 1) * chunk) for c in range(nch)]
        grp = [slice(c * nseg, (c + 1) * nseg) for c in range(nch)]
        wide = lambda ref: jnp.concatenate([ref[rs, :] for rs in rss], axis=1)
        pc = jnp.exp(jnp.concatenate([lpl_s[rs, :].reshape(nseg, seg, W)[:, 0:1, :] for rs in rss], axis=2))
        ys, states = _rwkv_recur(wide(at_s), wide(rt_s), wide(bi_s), wide(ki_s), wide(bd_s), wide(kd_s),
                                 wide(v_s), [s_scr[g, h] for g in grp for h in range(RW_H)], pc, seg)
        for c, g in enumerate(grp):
            for h in range(RW_H):
                s_scr[g, h] = states[c * RW_H + h]
            yr_s[rss[c], :] = jnp.concatenate(ys[c * RW_H:(c + 1) * RW_H], axis=1)

    if nseg == 1:
        lax.fori_loop(0, tt // chunk, body_long, 0)
    else:
        all_short()

    y = yr_s[...]
    mean = _headsum(y, ones_bd) * (1.0 / RW_D)
    yc = y - mean
    var = _headsum(yc * yc, ones_bd) * (1.0 / RW_D)
    yn = yc * lax.rsqrt(var + RW_GN_EPS) * ln_g + ln_b
    y_ref[...] = ((yn + bon_s[...]) * gate_s[...]).reshape(sb, tt, W)

    @pl.when(tstep == pl.num_programs(1) - 1)
    def _():
        _emit_state(l, sout_ref, sprev_ref, s_scr[...])
        _emit_state(l, shout_ref, shprev_ref, prev_scr[:, :, 0:RW_COLS])


def _rwkv_call(p, s_in, sh_in, prev, mu, vec, wup, ones_bd, te, l, rows, nlong, chunk, seg):
    bsz, t, _ = p.shape
    sb, tt = _rec_tile(bsz, t, rows, nlong)
    zero_init = s_in is None
    s_specs = _state_specs(l, sb, (RW_H, RW_D, RW_D))
    sh_specs = _state_specs(l, sb, (1, RW_COLS))
    args = ([p] + ([] if zero_init else [s_in, sh_in]) + [mu, vec, wup, ones_bd, te]
            + (list(prev) if l > 0 else []))
    specs = ([pl.BlockSpec((sb, tt, RW_PAD), lambda i, j: (i, j, 1))]
             + ([] if zero_init else [s_specs[0], sh_specs[0]])
             + [pl.BlockSpec((1, RW_PAD), lambda i, j: (0, 0)),
                pl.BlockSpec((8, W), lambda i, j: (0, 0)),
                pl.BlockSpec((256, 3 * W), lambda i, j: (0, 0)),
                pl.BlockSpec((W, W), lambda i, j: (0, 0)),
                pl.BlockSpec(te.shape, lambda i, j: (0, 0))]
             + ([s_specs[1], sh_specs[1]] if l > 0 else []))
    tile_scr = [pltpu.VMEM((sb * tt, W), F32) for _ in range(11)]
    return pl.pallas_call(
        functools.partial(_rwkv_kernel, (l, chunk, seg, zero_init)),
        out_shape=(jax.ShapeDtypeStruct((bsz, t, W), F32),
                   jax.ShapeDtypeStruct((l + 1, bsz, RW_H, RW_D, RW_D), F32),
                   jax.ShapeDtypeStruct((l + 1, bsz, 1, RW_COLS), F32)),
        grid=(bsz // sb, t // tt),
        in_specs=specs,
        out_specs=(pl.BlockSpec((sb, tt, W), lambda i, j: (i, j, 0)), s_specs[2], sh_specs[2]),
        scratch_shapes=[pltpu.VMEM((sb, RW_H, RW_D, RW_D), F32), pltpu.VMEM((sb, 1, RW_PAD), F32)] + tile_scr,
        compiler_params=_cparams(("arbitrary", "arbitrary")),
        name="rwkv7",
    )(*args)


def _merge_kernel(yhg_ref, yrw_ref, ycf_ref, ylr_ref, x_ref, sc1_ref, sh1_ref, g1_ref, sc2_ref, sh2_ref,
                  nmix_ref, nmlp_ref, wg_ref, bg_ref, wb_ref, wo_ref, x1_ref, h2_ref):
    sb, tt, _ = x_ref.shape
    tm = sb * tt
    x = x_ref[...]
    h = _bf(_adaln(x, nmix_ref[...], sc1_ref[...], sh1_ref[...]).reshape(tm, D))
    z = jnp.zeros((tm, D), F32)
    for b, y_ref in enumerate((yhg_ref, yrw_ref, ycf_ref, ylr_ref)):
        bo = jnp.dot(_bf(y_ref[...].reshape(tm, W)), wb_ref[b], preferred_element_type=F32)
        logit = jnp.dot(h, wg_ref[:, b * D:(b + 1) * D], preferred_element_type=F32) + bg_ref[:, b * D:(b + 1) * D]
        z = z + _sigmoid(logit) * bo
    out = jnp.dot(_bf(z), wo_ref[...], preferred_element_type=F32).reshape(sb, tt, D)
    x1 = x + g1_ref[...] * out
    x1_ref[...] = x1
    h2_ref[...] = _bf(_adaln(x1, nmlp_ref[...], sc2_ref[...], sh2_ref[...]))


def _merge_call(ys, x, mods, nmix, nmlp, wg, bg, wb, wo, rows=512):
    bsz, t, _ = x.shape
    sb, tt = _tile(bsz, t, rows)
    tok = lambda width: pl.BlockSpec((sb, tt, width), lambda i, j: (i, j, 0))
    seq = pl.BlockSpec((sb, 1, D), lambda i, j: (i, 0, 0))
    const = lambda shape: pl.BlockSpec(shape, lambda i, j: (0,) * len(shape), pipeline_mode=pl.Buffered(1))
    return pl.pallas_call(
        _merge_kernel,
        out_shape=(jax.ShapeDtypeStruct((bsz, t, D), F32),
                   jax.ShapeDtypeStruct((bsz, t, D), BF16)),
        grid=(bsz // sb, t // tt),
        in_specs=[tok(W), tok(W), tok(W), tok(W), tok(D), seq, seq, seq, seq, seq,
                  const((1, D)), const((1, D)), const((D, 4 * D)), const((1, 4 * D)),
                  const((4, W, D)), const((D, D))],
        out_specs=(tok(D), tok(D)),
        compiler_params=_cparams(("arbitrary", "arbitrary")),
        name="merge",
    )(*ys, x, *mods, nmix, nmlp, wg, bg, wb, wo)


def _mlp_kernel(final, h2_ref, x1_ref, g2_ref, fg_ref, w1_ref, w2_ref, o_ref, acc):
    sb, tt, _ = x1_ref.shape
    k = pl.program_id(2)

    @pl.when(k == 0)
    def _():
        acc[...] = jnp.zeros_like(acc)

    hid = jnp.dot(h2_ref[...].reshape(sb * tt, D), w1_ref[...], preferred_element_type=F32)
    act = jnp.square(jnp.maximum(hid, 0.0))
    acc[...] += jnp.dot(_bf(act), w2_ref[...], preferred_element_type=F32)

    @pl.when(k == pl.num_programs(2) - 1)
    def _():
        x2 = x1_ref[...] + g2_ref[...] * acc[...].reshape(sb, tt, D)
        if final:
            ms = jnp.mean(x2 * x2, axis=-1, keepdims=True)
            x2 = x2 * lax.rsqrt(ms + EPS) * fg_ref[...]
        o_ref[...] = x2


def _mlp_call(h2, x1, g2, fg, w1, w2, final, rows=1024, th=2048):
    bsz, t, _ = x1.shape
    sb, tt = _tile(bsz, t, rows)
    tok = pl.BlockSpec((sb, tt, D), lambda i, j, k: (i, j, 0))
    return pl.pallas_call(
        functools.partial(_mlp_kernel, final),
        out_shape=jax.ShapeDtypeStruct((bsz, t, D), F32),
        grid=(bsz // sb, t // tt, HID // th),
        in_specs=[tok, tok,
                  pl.BlockSpec((sb, 1, D), lambda i, j, k: (i, 0, 0)),
                  pl.BlockSpec((1, D), lambda i, j, k: (0, 0)),
                  pl.BlockSpec((D, th), lambda i, j, k: (0, k)),
                  pl.BlockSpec((th, D), lambda i, j, k: (k, 0))],
        out_specs=tok,
        scratch_shapes=[pltpu.VMEM((sb * tt, D), F32)],
        compiler_params=_cparams(("arbitrary", "arbitrary", "arbitrary")),
        name="mlp",
    )(h2, x1, g2, fg, w1, w2)


def _block_diag(w):
    n, c, d = w.shape
    eye = jnp.eye(n, dtype=w.dtype)
    return (eye[:, None, :, None] * w[:, :, None, :]).reshape(n * c, n * d)


def _prep_layer(wt, l):
    w_in = wt["w_in"][l]
    zpad = jnp.zeros((D, RW_PAD - RW_COLS), F32)
    wcat = jnp.concatenate([w_in[:, 0:2048], w_in[:, 2048:2048 + RW_COLS], zpad, w_in[:, 2048 + RW_COLS:]], axis=1)
    wup = jnp.zeros((256, 3 * W), F32)
    wup = wup.at[0:64, 0:W].set(wt["rw_w_up"][l])
    wup = wup.at[64:128, W:2 * W].set(wt["rw_a_up"][l])
    wup = wup.at[128:256, 2 * W:3 * W].set(wt["rw_g_up"][l])
    zrow = jnp.zeros((W,), F32)
    rw_vec = jnp.stack([wt["rw_w0"][l], wt["rw_a0"][l], wt["rw_k_k"][l], wt["rw_k_a"][l], wt["rw_r_k"][l],
                        wt["rw_ln_g"][l], wt["rw_ln_b"][l], zrow])
    mu = jnp.concatenate([wt["rw_mu"][l], jnp.zeros((RW_PAD - RW_COLS,), F32)])[None, :]
    cf_vec = jnp.stack([wt["cf_dw_b"][l], wt["cf_ln_g"][l], wt["cf_ln_b"][l]] + [zrow] * 5)
    cf_dw = jnp.concatenate([wt["cf_dw"][l], jnp.zeros((1, W), F32)], axis=0)
    lru_vec = jnp.stack([wt["lru_conv_b"][l], wt["lru_ba"][l], wt["lru_bx"][l], wt["lru_lambda"][l]] + [zrow] * 4)
    lru_cw = jnp.concatenate([wt["lru_conv_w"][l], jnp.zeros((8 - LRU_K, W), F32)], axis=0)
    wax = jnp.concatenate([_block_diag(wt["lru_wa"][l]), _block_diag(wt["lru_wx"][l])], axis=1)
    return dict(
        wcat=_bf(wcat), wup=_bf(wup), rw_vec=rw_vec, mu=mu, cf_vec=cf_vec, cf_dw=cf_dw,
        lru_vec=lru_vec, lru_cw=lru_cw, wax=_bf(wax), wg=_bf(wt["w_gate"][l]), bg=wt["b_gate"][l][None, :],
        wb=_bf(wt["w_branch"][l]), wo=_bf(wt["w_out"][l]), w1=_bf(wt["w_mlp1"][l]), w2=_bf(wt["w_mlp2"][l]),
        hg_gn=wt["hg_norm_g"][l][None, :], nmix=wt["norm_mix_g"][l][None, :], nmlp=wt["norm_mlp_g"][l][None, :],
    )


def _trunk(x, mod, states, wt, layers, ones_bd, mix_rows, rec_rows, nlong, chunk, seg):
    nl = len(layers)
    te = _scan_mats(chunk, seg)
    if states is None:
        s_hg = s_rw = s_shift = s_cf = s_lh = s_lc = None
    else:
        s_hg, s_rw, s_shift, s_cf, s_lh, s_lc = states
        s_shift = s_shift[:, :, None, :]
        s_lh = s_lh[:, :, None, :]
    fg = wt["norm_final_g"][None, :]
    n_hg = n_rw = n_sh = n_cf = n_lh = n_lc = None
    for l, lw in enumerate(layers):
        sh1, sc1, g1, sh2, sc2, g2 = [mod[l, i][:, None, :] for i in range(6)]
        conv_in = None if states is None else (s_cf, s_lh, s_lc)
        p, y_cf, y_lr, n_cf, n_lh, n_lc = _inproj_call(
            x, sc1, sh1, lw["nmix"], lw["wcat"], conv_in, (n_cf, n_lh, n_lc), lw["cf_dw"], lw["cf_vec"],
            lw["lru_cw"], lw["lru_vec"], lw["wax"], l, mix_rows)
        y_hg, n_hg = _hgrn_call(p, s_hg, n_hg, wt["hg_lower"], lw["hg_gn"], te, l, rec_rows, nlong, chunk, seg)
        y_rw, n_rw, n_sh = _rwkv_call(p, s_rw, s_shift, (n_rw, n_sh), lw["mu"], lw["rw_vec"], lw["wup"], ones_bd,
                                      te, l, rec_rows, nlong, chunk, seg)
        x1, h2 = _merge_call((y_hg, y_rw, y_cf, y_lr), x, (sc1, sh1, g1, sc2, sh2), lw["nmix"], lw["nmlp"],
                             lw["wg"], lw["bg"], lw["wb"], lw["wo"])
        x = _mlp_call(h2, x1, g2, fg, lw["w1"], lw["w2"], final=(l == nl - 1))
    return x, [n_hg, n_rw, n_sh[:, :, 0, :], n_cf, n_lh[:, :, 0, :], n_lc]


def _run(x_prompt, x_sample, sample_states, c_prompt, c_sample, wt):
    nl = wt["w_in"].shape[0]
    layers = [_prep_layer(wt, l) for l in range(nl)]
    head = jnp.arange(W, dtype=jnp.int32) // RW_D
    ones_bd = _bf((head[:, None] == head[None, :]).astype(F32))
    tp = x_prompt.shape[1]
    ts = x_sample.shape[1]
    rows_p = min(256, tp)
    chunk_p = min(64, tp)
    bp = x_prompt.shape[0]
    mod = _mod_call(jnp.concatenate([c_prompt, c_sample], axis=0), wt["ada_w"], wt["ada_b"])
    y_p, st_p = _trunk(x_prompt, mod[:, :, :bp], None, wt, layers, ones_bd, rows_p, rows_p // 4, 8, chunk_p,
                       chunk_p)
    y_s, st_s = _trunk(x_sample, mod[:, :, bp:], sample_states, wt, layers, ones_bd, rows_p, 16 * ts, 1, 8 * ts, ts)
    return (y_p, y_s, *st_p, *st_s)


def kernel(x_prompt, x_sample, state_hgrn, state_rwkv, state_rwkv_shift, state_conv, state_lru_h, state_lru_conv, c_prompt, c_sample, ada_w, ada_b, norm_mix_g, norm_mlp_g, norm_final_g, w_in, hg_lower, hg_norm_g, rw_mu, rw_w0, rw_w_up, rw_a0, rw_a_up, rw_g_up, rw_k_k, rw_k_a, rw_r_k, rw_ln_g, rw_ln_b, cf_dw, cf_dw_b, cf_ln_g, cf_ln_b, lru_conv_w, lru_conv_b, lru_wa, lru_ba, lru_wx, lru_bx, lru_lambda, w_branch, w_gate, b_gate, w_out, w_mlp1, w_mlp2):
    wt = dict(ada_w=ada_w, ada_b=ada_b, norm_mix_g=norm_mix_g, norm_mlp_g=norm_mlp_g,
              norm_final_g=norm_final_g, w_in=w_in, hg_lower=hg_lower, hg_norm_g=hg_norm_g, rw_mu=rw_mu,
              rw_w0=rw_w0, rw_w_up=rw_w_up, rw_a0=rw_a0, rw_a_up=rw_a_up, rw_g_up=rw_g_up, rw_k_k=rw_k_k,
              rw_k_a=rw_k_a, rw_r_k=rw_r_k, rw_ln_g=rw_ln_g, rw_ln_b=rw_ln_b, cf_dw=cf_dw, cf_dw_b=cf_dw_b,
              cf_ln_g=cf_ln_g, cf_ln_b=cf_ln_b, lru_conv_w=lru_conv_w, lru_conv_b=lru_conv_b, lru_wa=lru_wa,
              lru_ba=lru_ba, lru_wx=lru_wx, lru_bx=lru_bx, lru_lambda=lru_lambda, w_branch=w_branch,
              w_gate=w_gate, b_gate=b_gate, w_out=w_out, w_mlp1=w_mlp1, w_mlp2=w_mlp2)
    sample_states = (state_hgrn, state_rwkv, state_rwkv_shift, state_conv, state_lru_h, state_lru_conv)
    return _run(x_prompt, x_sample, sample_states, c_prompt, c_sample, wt)
```

```python
import functools
import math

import jax
import jax.numpy as jnp
from jax import lax
from jax.experimental import pallas as pl
from jax.experimental.pallas import tpu as pltpu

D = 1024
W = 512
HG_H = 4
HG_D = 128
RW_H = 8
RW_D = 64
RW_COLS = 1792
RW_PAD = 2048
CF_K = 31
LRU_K = 4
HID = 4096
EPS = 1e-6
RW_GN_EPS = 64e-5
RW_DECAY = 0.606531
LRU_C = 8.0
LANES = 128

P_COLS = 6144
SEQ_UNROLL = 2
VMEM_LIMIT = 56 * 1024 * 1024

F32 = jnp.float32
BF16 = jnp.bfloat16


def _bf(x):
    return x.astype(BF16)


def _mm(a, b):
    return jnp.dot(_bf(a), _bf(b), preferred_element_type=F32)


def _mm_nt(a, b):
    return lax.dot_general(_bf(a), _bf(b), (((1,), (1,)), ((), ())), preferred_element_type=F32)


def _mm_tn(a, b):
    return lax.dot_general(_bf(a), _bf(b), (((0,), (0,)), ((), ())), preferred_element_type=F32)


def _sigmoid(x):
    return 1.0 / (1.0 + jnp.exp(-x))


def _silu(x):
    return x * _sigmoid(x)


def _local_row(shape, seg):
    return jnp.bitwise_and(lax.broadcasted_iota(jnp.int32, shape, 0), seg - 1)


def _shift_rows(x, d, fill, seg):
    return jnp.where(_local_row(x.shape, seg) >= d, pltpu.roll(x, d, axis=0), fill)


def _scan_mats(chunk, seg):
    r = jnp.arange(chunk, dtype=jnp.int32)
    same = (r[:, None] // seg) == (r[None, :] // seg)
    tri = same & (r[None, :] <= r[:, None])
    return _bf(jnp.concatenate([tri, same], axis=0).astype(F32))


def _cumsum_last(x, te):
    c = x.shape[0]
    p1 = _bf(x)
    r1 = x - p1.astype(F32)
    p2 = _bf(r1)
    p3 = _bf(r1 - p2.astype(F32))
    out = (jnp.dot(te, p1, preferred_element_type=F32) + jnp.dot(te, p2, preferred_element_type=F32)
           + jnp.dot(te, p3, preferred_element_type=F32))
    return out[:c], out[c:]


def _seg_id(shape, axis, log2_seg):
    return lax.shift_right_logical(lax.broadcasted_iota(jnp.int32, shape, axis), log2_seg)


def _headsum(x, ones_bd):
    hi = _bf(x)
    lo = _bf(x - hi.astype(F32))
    return (jnp.dot(hi, ones_bd, preferred_element_type=F32)
            + jnp.dot(lo, ones_bd, preferred_element_type=F32))


def _cparams(sem):
    return pltpu.CompilerParams(dimension_semantics=sem, vmem_limit_bytes=VMEM_LIMIT)


def _tile(bsz, t, rows):
    if t >= rows:
        return 1, rows
    return min(bsz, rows // t), t


def _state_specs(l, sb, tail):
    zeros = (0,) * len(tail)
    in_spec = pl.BlockSpec((None, sb) + tail, lambda i, j: (l, i) + zeros)
    prev_spec = pl.BlockSpec((l, sb) + tail, lambda i, j: (0, i) + zeros)
    out_spec = pl.BlockSpec((l + 1, sb) + tail, lambda i, j: (0, i) + zeros)
    return in_spec, prev_spec, out_spec


def _emit_state(l, out_ref, prev_ref, new):
    for q in range(l):
        out_ref[q] = prev_ref[q]
    out_ref[l] = new


def _mod_kernel(c_ref, w_ref, b_ref, o_ref):
    c = c_ref[...]
    o_ref[0, 0] = _mm(_silu(c), w_ref[0]) + b_ref[0, 0]


def _mod_call(c, ada_w, ada_b):
    nl = ada_w.shape[0]
    bsz = c.shape[0]
    return pl.pallas_call(
        _mod_kernel,
        out_shape=jax.ShapeDtypeStruct((nl, 6, bsz, D), F32),
        grid=(nl, 6),
        in_specs=[
            pl.BlockSpec((bsz, D), lambda l, j: (0, 0)),
            pl.BlockSpec((1, D, D), lambda l, j: (l, 0, j)),
            pl.BlockSpec((1, 1, 1, D), lambda l, j: (l, j, 0, 0)),
        ],
        out_specs=pl.BlockSpec((1, 1, bsz, D), lambda l, j: (l, j, 0, 0)),
        compiler_params=_cparams(("arbitrary", "arbitrary")),
        name="adaln_mod",
    )(c, ada_w, ada_b.reshape(nl, 6, 1, D))


def _adaln(x, g, sc, sh):
    ms = jnp.mean(x * x, axis=-1, keepdims=True)
    y = x * lax.rsqrt(ms + EPS) * g
    return y * (1.0 + sc) + sh


CF_PAD = 32
CF_RB = 64
SUBLANES = 8
LRU_PAD = 8
REC_COLS = 4096
REC_PIECE = 1024


def _gelu_tanh(x):
    return 0.5 * x * (1.0 + jnp.tanh(0.7978845608028654 * (x + 0.044715 * (x * x * x))))


def _conformer_prep(pcf, s, slot, ext, shifted):
    tt = pcf.shape[0]
    span = tt + CF_PAD - SUBLANES
    ext[s, CF_PAD:CF_PAD + tt, :] = pcf[:, 0:W] * _sigmoid(pcf[:, W:2 * W])
    for q in range(1, SUBLANES):
        shifted[slot, q, 0:span, :] = ext[s, q:q + span, :]


def _rows(start, size):
    if isinstance(start, int):
        return pl.ds(start, size)
    return pl.ds(pl.multiple_of(start, SUBLANES), size)


def _conformer_rows(r0, rb, s, slot, ext, shifted, y_ref, dw, bias, ln_g, ln_b):
    off = CF_PAD - (CF_K - 1)
    acc = jnp.zeros((rb, W), F32) + bias
    for j in range(CF_K):
        a8, q = divmod(off + j, SUBLANES)
        rows = _rows(r0 + a8 * SUBLANES, rb)
        tap = ext[s, rows, :] if q == 0 else shifted[slot, q, rows, :]
        acc = acc + dw[j:j + 1, :] * tap
    mean = jnp.mean(acc, axis=-1, keepdims=True)
    xc = acc - mean
    var = jnp.mean(xc * xc, axis=-1, keepdims=True)
    yn = xc * lax.rsqrt(var + 1e-5) * ln_g + ln_b
    y_ref[s, _rows(r0, rb), :] = _silu(yn)


def _group_prefix(a, b):
    d = 1
    while d < SUBLANES:
        a_s = _shift_rows(a, d, 1.0, SUBLANES)
        b_s = _shift_rows(b, d, 0.0, SUBLANES)
        b = a * b_s + b
        a = a * a_s
        d *= 2
    return a, b


def _linear_scan(a, b, h):
    n = a.shape[0]
    a, b = _group_prefix(a, b)
    out = []
    for g in range(n // SUBLANES):
        hs = a[g * SUBLANES:(g + 1) * SUBLANES] * h + b[g * SUBLANES:(g + 1) * SUBLANES]
        h = hs[SUBLANES - 1:SUBLANES, :]
        out.append(hs)
    return (jnp.concatenate(out, axis=0) if len(out) > 1 else out[0]), h


def _lru_conv(s, tt, ext, cw, cb):
    off = LRU_PAD - (LRU_K - 1)
    xc = jnp.zeros((tt, W), F32) + cb
    for j in range(LRU_K):
        xc = xc + cw[j:j + 1, :] * ext[s, off + j:off + j + tt, :]
    return xc


def _lru_rows(xc, gate, h, ba, bx, sp, wax):
    pre = jnp.dot(_bf(xc), wax, preferred_element_type=F32)
    rg = _sigmoid(pre[:, 0:W] + ba)
    ig = _sigmoid(pre[:, W:2 * W] + bx)
    a = jnp.exp(-LRU_C * rg * sp)
    hs, h = _linear_scan(a, jnp.sqrt(1.0 - a * a) * (ig * xc), h)
    return hs * _gelu_tanh(gate), h


def _inproj_kernel(cfg, *refs):
    l, zero_init = cfg
    refs = list(refs)
    x_ref, sc_ref, sh_ref, g_ref, w_ref = [refs.pop(0) for _ in range(5)]
    cf0_ref, lh0_ref, lc0_ref = (None,) * 3 if zero_init else [refs.pop(0) for _ in range(3)]
    dw_ref, cfv_ref, cw_ref, lrv_ref, wax_ref = [refs.pop(0) for _ in range(5)]
    cfp_ref, lhp_ref, lcp_ref = [refs.pop(0) for _ in range(3)] if l > 0 else (None,) * 3
    p_ref, ycf_ref, ylr_ref, cfo_ref, lho_ref, lco_ref = [refs.pop(0) for _ in range(6)]
    cf_ext, shifted, lr_ext, hcar, pc_scr, h_scr, xc_scr = refs
    sb, tt, _ = x_ref.shape
    tm = sb * tt
    tstep = pl.program_id(1)
    cf_off = CF_PAD - (CF_K - 1)
    lr_off = LRU_PAD - (LRU_K - 1)

    @pl.when(tstep == 0)
    def _():
        if zero_init:
            cf_ext[:, 0:CF_PAD, :] = jnp.zeros((sb, CF_PAD, W), F32)
            lr_ext[:, 0:LRU_PAD, :] = jnp.zeros((sb, LRU_PAD, W), F32)
            hcar[...] = jnp.zeros_like(hcar)
        else:
            cf_ext[:, cf_off:CF_PAD, :] = cf0_ref[...]
            lr_ext[:, lr_off:LRU_PAD, :] = lc0_ref[...]
            hcar[...] = lh0_ref[...]

    h_scr[...] = _bf(_adaln(x_ref[...], g_ref[...], sc_ref[...], sh_ref[...]).reshape(tm, D))
    pc_scr[...] = jnp.dot(h_scr[...], w_ref[:, REC_COLS:P_COLS], preferred_element_type=F32).reshape(sb, tt, 4 * W)

    def rec_piece(n):
        cols = slice(n * REC_PIECE, (n + 1) * REC_PIECE)
        p_ref[:, :, cols] = jnp.dot(h_scr[...], w_ref[:, cols], preferred_element_type=F32).reshape(sb, tt, REC_PIECE)

    cfv = cfv_ref[...]
    dw = dw_ref[...]
    lrv = lrv_ref[...]
    lam = lrv[3:4]
    sp = jnp.maximum(-lam, 0.0) + jnp.log1p(jnp.exp(-jnp.abs(lam)))
    cw = cw_ref[...]
    wax = wax_ref[...]
    nslot = shifted.shape[0]

    rb = min(CF_RB, tt)

    def prep(s, slot):
        _conformer_prep(pc_scr[s, :, 0:2 * W], s, slot, cf_ext, shifted)
        lr_ext[s, LRU_PAD:LRU_PAD + tt, :] = pc_scr[s, :, 2 * W:3 * W]
        xc_scr[s] = _lru_conv(s, tt, lr_ext, cw, lrv[0:1])

    def row_block(s, slot, r0, hstate):
        _conformer_rows(r0, rb, s, slot, cf_ext, shifted, ycf_ref, dw, cfv[0:1], cfv[1:2], cfv[2:3])
        rows = _rows(r0, rb)
        y, hstate = _lru_rows(xc_scr[s, rows, :], pc_scr[s, rows, 3 * W:4 * W], hstate, lrv[1:2], lrv[2:3], sp, wax)
        ylr_ref[s, rows, :] = y
        return hstate

    def finish(s, hstate):
        hcar[s] = hstate
        cf_tail = cf_ext[s, tt:tt + CF_PAD, :]
        cf_ext[s, 0:CF_PAD, :] = cf_tail
        lr_tail = lr_ext[s, tt:tt + LRU_PAD, :]
        lr_ext[s, 0:LRU_PAD, :] = lr_tail

    for n in range(REC_COLS // REC_PIECE):
        rec_piece(n)

    def all_short():
        pc = pc_scr[...]
        cf_ext[:, CF_PAD:CF_PAD + tt, :] = pc[:, :, 0:W] * _sigmoid(pc[:, :, W:2 * W])
        acc = jnp.zeros((sb, tt, W), F32) + cfv[0:1]
        for j in range(CF_K):
            acc = acc + dw[j:j + 1, :] * cf_ext[:, cf_off + j:cf_off + j + tt, :]
        mean = jnp.mean(acc, axis=-1, keepdims=True)
        xn = acc - mean
        var = jnp.mean(xn * xn, axis=-1, keepdims=True)
        ycf_ref[...] = _silu(xn * lax.rsqrt(var + 1e-5) * cfv[1:2] + cfv[2:3])
        lr_ext[:, LRU_PAD:LRU_PAD + tt, :] = pc[:, :, 2 * W:3 * W]
        xc = jnp.zeros((sb, tt, W), F32) + lrv[0:1]
        for j in range(LRU_K):
            xc = xc + cw[j:j + 1, :] * lr_ext[:, lr_off + j:lr_off + j + tt, :]
        xc = xc.reshape(tm, W)
        pre = jnp.dot(_bf(xc), wax, preferred_element_type=F32)
        rg = _sigmoid(pre[:, 0:W] + lrv[1:2])
        ig = _sigmoid(pre[:, W:2 * W] + lrv[2:3])
        a = jnp.exp(-LRU_C * rg * sp)
        a, b = _group_prefix(a, jnp.sqrt(1.0 - a * a) * (ig * xc))
        hs = a.reshape(sb, tt, W) * hcar[...] + b.reshape(sb, tt, W)
        hcar[...] = hs[:, tt - 1:tt, :]
        ylr_ref[...] = hs * _gelu_tanh(pc[:, :, 3 * W:4 * W])
        cf_tail = cf_ext[:, tt:tt + CF_PAD, :]
        cf_ext[:, 0:CF_PAD, :] = cf_tail
        lr_tail = lr_ext[:, tt:tt + LRU_PAD, :]
        lr_ext[:, 0:LRU_PAD, :] = lr_tail

    def body(i, carry):
        for slot in range(nslot):
            s = i * nslot + slot
            prep(s, slot)
            hstate = hcar[s]
            for r0 in range(0, tt, rb):
                hstate = row_block(s, slot, r0, hstate)
            finish(s, hstate)
        return carry

    if tt == SUBLANES:
        all_short()
    else:
        lax.fori_loop(0, sb // nslot, body, 0)

    @pl.when(tstep == pl.num_programs(1) - 1)
    def _():
        _emit_state(l, cfo_ref, cfp_ref, cf_ext[:, cf_off:CF_PAD, :])
        _emit_state(l, lho_ref, lhp_ref, hcar[...])
        _emit_state(l, lco_ref, lcp_ref, lr_ext[:, lr_off:LRU_PAD, :])


def _inproj_call(x, sc, sh, g, wcat, conv_in, conv_prev, dw, cfv, cw, lrv, wax, l, rows=256):
    bsz, t, _ = x.shape
    sb, tt = _tile(bsz, t, rows)
    zero_init = conv_in is None
    tails = ((CF_K - 1, W), (1, W), (LRU_K - 1, W))
    sspecs = [_state_specs(l, sb, tail) for tail in tails]
    tok = lambda width: pl.BlockSpec((sb, tt, width), lambda i, j: (i, j, 0))
    seq = pl.BlockSpec((sb, 1, D), lambda i, j: (i, 0, 0))
    const = lambda shape: pl.BlockSpec(shape, lambda i, j: (0,) * len(shape), pipeline_mode=pl.Buffered(1))
    args = ([x, sc, sh, g, wcat] + ([] if zero_init else list(conv_in)) + [dw, cfv, cw, lrv, wax]
            + (list(conv_prev) if l > 0 else []))
    specs = ([tok(D), seq, seq, const((1, D)), const((D, P_COLS))]
             + ([] if zero_init else [sp[0] for sp in sspecs])
             + [const((32, W)), const((8, W)), const((8, W)), const((8, W)), const((W, 2 * W))]
             + ([sp[1] for sp in sspecs] if l > 0 else []))
    nslot = min(sb, SEQ_UNROLL)
    return pl.pallas_call(
        functools.partial(_inproj_kernel, (l, zero_init)),
        out_shape=(jax.ShapeDtypeStruct((bsz, t, REC_COLS), F32),
                   jax.ShapeDtypeStruct((bsz, t, W), F32),
                   jax.ShapeDtypeStruct((bsz, t, W), F32))
        + tuple(jax.ShapeDtypeStruct((l + 1, bsz) + tail, F32) for tail in tails),
        grid=(bsz // sb, t // tt),
        in_specs=specs,
        out_specs=(tok(REC_COLS), tok(W), tok(W)) + tuple(sp[2] for sp in sspecs),
        scratch_shapes=[pltpu.VMEM((sb, CF_PAD + tt, W), F32),
                        pltpu.VMEM((nslot, SUBLANES, CF_PAD + tt, W), F32),
                        pltpu.VMEM((sb, LRU_PAD + tt, W), F32),
                        pltpu.VMEM((sb, 1, W), F32),
                        pltpu.VMEM((sb, tt, 4 * W), F32),
                        pltpu.VMEM((sb * tt, D), BF16),
                        pltpu.VMEM((sb, tt, W), F32)],
        compiler_params=_cparams(("arbitrary", "arbitrary")),
        name="inproj",
    )(*args)


HG_SUB = 16


def _hgrn_chunk(qr, fz, iv, og, lb, gn, te, states, seg):
    c, width = qr.shape
    nseg = c // seg
    heads = range(width // HG_D)
    q = _silu(qr)
    f = lb + (1.0 - lb) * _sigmoid(fz)
    logf = jnp.log(f)
    kf = (1.0 - lb) * _sigmoid(-fz)
    b, b_last = _cumsum_last(logf, te)
    qe = q * jnp.exp(b)
    kdec = kf * jnp.exp(b_last - b)
    e_last = jnp.exp(b_last)
    sl = [slice(h * HG_D, (h + 1) * HG_D) for h in heads]

    if nseg == 1:
        sub = min(HG_SUB, c)
        nsub = c // sub
        o_inter = [_mm_nt(qe[:, sl[h]], states[h]) for h in heads]
        new_states = [states[h] * e_last[0:1, sl[h]] + _mm_tn(iv[:, sl[h]], kdec[:, sl[h]]) for h in heads]
        pieces = [[] for _ in heads]
        for i in range(nsub):
            r0 = i * sub
            m = b[r0 - 1:r0, :] if i > 0 else jnp.zeros((1, width), F32)
            qs = q[r0:r0 + sub] * jnp.exp(b[r0:r0 + sub] - m)
            kd = kf[r0:r0 + sub] * jnp.exp(jnp.minimum(m - b[r0:r0 + sub], 80.0))
            if i > 0:
                kall = jnp.concatenate([kf[:r0] * jnp.exp(m - b[:r0]), kd], axis=0)
            else:
                kall = kd
            row = lax.broadcasted_iota(jnp.int32, (sub, r0 + sub), 0) + r0
            col = lax.broadcasted_iota(jnp.int32, (sub, r0 + sub), 1)
            causal = col <= row
            scs = [jnp.where(causal, _mm_nt(qs[:, sl[h]], kall[:, sl[h]]), 0.0) for h in heads]
            for h in heads:
                pieces[h].append(_mm(scs[h], iv[:r0 + sub, sl[h]]))
        o_intra = [jnp.concatenate(pieces[h], axis=0) if nsub > 1 else pieces[h][0] for h in heads]
    else:
        lg = int(math.log2(seg))
        kd = kf * jnp.exp(jnp.minimum(-b, 80.0))
        rr = lax.broadcasted_iota(jnp.int32, (c, c), 0)
        cc = lax.broadcasted_iota(jnp.int32, (c, c), 1)
        causal = (lax.shift_right_logical(rr, lg) == lax.shift_right_logical(cc, lg)) & (cc <= rr)
        shape_x = (c, nseg * HG_D)
        blockmask = _seg_id(shape_x, 0, lg) == _seg_id(shape_x, 1, 7)
        first = blockmask & (_local_row(shape_x, seg) == 0)
        spread = lambda v, mask: jnp.where(mask, jnp.concatenate([v] * nseg, axis=1), 0.0)
        ones_cv = jnp.ones((3 * c, HG_D), BF16)
        sflat = [states[h].reshape(nseg * HG_D, HG_D) for h in heads]
        o_inter = [_mm(spread(qe[:, sl[h]], blockmask), sflat[h]) for h in heads]
        scs = [jnp.where(causal, _mm_nt(qe[:, sl[h]], kd[:, sl[h]]), 0.0) for h in heads]
        o_intra = [_mm(scs[h], iv[:, sl[h]]) for h in heads]
        new_states = []
        for h in heads:
            ex = spread(e_last[:, sl[h]], first)
            p1 = _bf(ex)
            r1 = ex - p1.astype(F32)
            p2 = _bf(r1)
            p3 = _bf(r1 - p2.astype(F32))
            dcol = lax.dot_general(jnp.concatenate([p1, p2, p3], axis=0), ones_cv, (((0,), (0,)), ((), ())),
                                   preferred_element_type=F32)
            upd = _mm_tn(spread(kdec[:, sl[h]], blockmask), iv[:, sl[h]])
            new_states.append((sflat[h] * dcol + upd).reshape(nseg, HG_D, HG_D))
    outs = []
    for h in heads:
        o = o_inter[h] + o_intra[h]
        outs.append(o * lax.rsqrt(jnp.mean(o * o, axis=-1, keepdims=True) + EPS))
    y = jnp.concatenate(outs, axis=1) * gn * _silu(og)
    return y, new_states


def _hgrn_kernel(cfg, *refs):
    l, chunk, seg, zero_init = cfg
    refs = list(refs)
    p_ref = refs.pop(0)
    s0_ref = None if zero_init else refs.pop(0)
    lower_ref, gn_ref, te_ref = refs.pop(0), refs.pop(0), refs.pop(0)
    prev_ref = refs.pop(0) if l > 0 else None
    y_ref, sout_ref, s_scr = refs
    sb, tt, _ = p_ref.shape
    nseg = chunk // seg
    tstep = pl.program_id(1)

    @pl.when(tstep == 0)
    def _():
        if zero_init:
            s_scr[...] = jnp.zeros_like(s_scr)
        elif nseg > 1:
            s_scr[...] = s0_ref[...]
        else:
            def init(i, carry):
                for h in range(HG_H):
                    s_scr[i, h] = s0_ref[i, h].T
                return carry
            lax.fori_loop(0, sb, init, 0)

    low = lower_ref[...]
    e = jnp.exp(low - jnp.max(low, axis=0, keepdims=True))
    sm = e / jnp.sum(e, axis=0, keepdims=True)
    lb = jnp.sum(sm[:l + 1], axis=0, keepdims=True) - sm[0:1]
    gn = gn_ref[...]
    te = te_ref[...]

    def body_long(i, carry):
        rs = pl.ds(pl.multiple_of(i * chunk, chunk), chunk)
        cols = [jnp.concatenate([p_ref[q, rs, j * W:(j + 1) * W] for q in range(sb)], axis=1) for j in range(4)]
        wide = lambda v: jnp.concatenate([v] * sb, axis=1)
        y, states = _hgrn_chunk(*cols, wide(lb), wide(gn), te,
                                [s_scr[q, h] for q in range(sb) for h in range(HG_H)], seg)
        for q in range(sb):
            for h in range(HG_H):
                s_scr[q, h] = states[q * HG_H + h]
            y_ref[q, rs, :] = y[:, q * W:(q + 1) * W]
        return carry

    def all_short():
        nch = sb * tt // chunk
        grp = [slice(c * nseg, (c + 1) * nseg) for c in range(nch)]
        blks = [p_ref[g].reshape(chunk, 4 * W) for g in grp]
        cols = [jnp.concatenate([b[:, j * W:(j + 1) * W] for b in blks], axis=1) for j in range(4)]
        wide = lambda v: jnp.concatenate([v] * nch, axis=1)
        y, states = _hgrn_chunk(*cols, wide(lb), wide(gn), te,
                                [s_scr[g, h] for g in grp for h in range(HG_H)], seg)
        for c, g in enumerate(grp):
            for h in range(HG_H):
                s_scr[g, h] = states[c * HG_H + h]
            y_ref[g] = y[:, c * W:(c + 1) * W].reshape(nseg, seg, W)

    if nseg == 1:
        lax.fori_loop(0, tt // chunk, body_long, 0)
    else:
        all_short()

    @pl.when(tstep == pl.num_programs(1) - 1)
    def _():
        for q in range(l):
            sout_ref[q] = prev_ref[q]
        if nseg > 1:
            sout_ref[l] = s_scr[...]
        else:
            def fin(i, carry):
                for h in range(HG_H):
                    sout_ref[l, i, h] = s_scr[i, h].T
                return carry
            lax.fori_loop(0, sb, fin, 0)


def _rec_tile(bsz, t, rows, nlong):
    sb, tt = _tile(bsz, t, rows)
    return (min(nlong, bsz), tt) if t >= rows else (sb, tt)


def _hgrn_call(p, s_in, prev, hg_lower, gn, te, l, rows, nlong, chunk, seg):
    bsz, t, _ = p.shape
    sb, tt = _rec_tile(bsz, t, rows, nlong)
    zero_init = s_in is None
    in_spec, prev_spec, out_spec = _state_specs(l, sb, (HG_H, HG_D, HG_D))
    args = [p] + ([] if zero_init else [s_in]) + [hg_lower, gn, te] + ([prev] if l > 0 else [])
    specs = ([pl.BlockSpec((sb, tt, 4 * W), lambda i, j: (i, j, 0))] + ([] if zero_init else [in_spec])
             + [pl.BlockSpec(hg_lower.shape, lambda i, j: (0, 0)), pl.BlockSpec((1, W), lambda i, j: (0, 0)),
                pl.BlockSpec(te.shape, lambda i, j: (0, 0))]
             + ([prev_spec] if l > 0 else []))
    return pl.pallas_call(
        functools.partial(_hgrn_kernel, (l, chunk, seg, zero_init)),
        out_shape=(jax.ShapeDtypeStruct((bsz, t, W), F32),
                   jax.ShapeDtypeStruct((l + 1, bsz, HG_H, HG_D, HG_D), F32)),
        grid=(bsz // sb, t // tt),
        in_specs=specs,
        out_specs=(pl.BlockSpec((sb, tt, W), lambda i, j: (i, j, 0)), out_spec),
        scratch_shapes=[pltpu.VMEM((sb, HG_H, HG_D, HG_D), F32)],
        compiler_params=_cparams(("arbitrary", "arbitrary")),
        name="hgrn2",
    )(*args)


def _pick64(full, rowseg):
    nb = full.shape[1] // LANES
    half = lax.shift_right_logical(rowseg, 1)
    sel = full[:, 0:LANES]
    for j in range(1, nb):
        sel = jnp.where(half == j, full[:, j * LANES:(j + 1) * LANES], sel)
    sel = jnp.where(jnp.bitwise_and(rowseg, 1) == 1, pltpu.roll(sel, RW_D, axis=1), sel)
    return sel[:, 0:RW_D]


def _rwkv_recur(at, rt, bi, ki, bd, kd, v, states, pc, seg):
    c = at.shape[0]
    nseg = c // seg
    lg = int(math.log2(seg))
    rr = lax.broadcasted_iota(jnp.int32, (c, c), 0)
    cc = lax.broadcasted_iota(jnp.int32, (c, c), 1)
    if nseg == 1:
        strict = rr > cc
        incl = rr >= cc
    else:
        same = lax.shift_right_logical(rr, lg) == lax.shift_right_logical(cc, lg)
        strict = same & (rr > cc)
        incl = same & (rr >= cc)
        r2 = jnp.bitwise_and(lax.broadcasted_iota(jnp.int32, (2 * c, LANES), 0), c - 1)
        rowseg = lax.shift_right_logical(r2, lg)
        rb = jnp.bitwise_and(lax.broadcasted_iota(jnp.int32, (2 * c, nseg * RW_D), 0), c - 1)
        blockmask = lax.shift_right_logical(rb, lg) == _seg_id((2 * c, nseg * RW_D), 1, 6)
    heads = range(at.shape[1] // RW_D)
    sl = [slice(h * RW_D, (h + 1) * RW_D) for h in heads]
    ar = [jnp.concatenate([at[:, sl[h]], rt[:, sl[h]]], axis=0) for h in heads]
    bk = [jnp.concatenate([bi[:, sl[h]], ki[:, sl[h]]], axis=0) for h in heads]
    g = [_mm_nt(ar[h], bk[h]) for h in heads]
    if nseg == 1:
        a_s = [_mm_nt(ar[h], states[h]) for h in heads]
    else:
        a_s = [_pick64(_mm_nt(ar[h], states[h].reshape(nseg * RW_D, RW_D)), rowseg) for h in heads]
    vh = [v[:, sl[h]] for h in heads]
    lp = [jnp.where(strict, g[h][:c, :c], 0.0) for h in heads]
    x = [a_s[h][:c] + _mm(jnp.where(strict, g[h][:c, c:], 0.0), vh[h]) for h in heads]
    for j in range(lg):
        x = [x[h] + _mm(lp[h], x[h]) for h in heads]
        if j < lg - 1:
            lp = [_mm(lp[h], lp[h]) for h in heads]
    uv = [jnp.concatenate([x[h], vh[h]], axis=0) for h in heads]
    mrbk = [jnp.concatenate([jnp.where(incl, g[h][c:, :c], 0.0), jnp.where(incl, g[h][c:, c:], 0.0)], axis=1)
            for h in heads]
    ys = [a_s[h][c:] + _mm(mrbk[h], uv[h]) for h in heads]
    bkd = [jnp.concatenate([bd[:, sl[h]], kd[:, sl[h]]], axis=0) for h in heads]
    if nseg == 1:
        new_states = [states[h] * pc[:, sl[h]] + _mm_tn(uv[h], bkd[h]) for h in heads]
    else:
        new_states = []
        for h in heads:
            u2 = jnp.concatenate([uv[h], uv[h]], axis=1)
            uvexp = jnp.where(blockmask, jnp.concatenate([u2] * (nseg // 2), axis=1), 0.0)
            upd = _mm_tn(uvexp, bkd[h]).reshape(nseg, RW_D, RW_D)
            new_states.append(states[h] * pc[:, :, sl[h]] + upd)
    return ys, new_states


def _rwkv_kernel(cfg, *refs):
    l, chunk, seg, zero_init = cfg
    refs = list(refs)
    p_ref = refs.pop(0)
    s0_ref, sh0_ref = (None, None) if zero_init else (refs.pop(0), refs.pop(0))
    mu_ref, vec_ref, wup_ref, ones_ref, te_ref = [refs.pop(0) for _ in range(5)]
    sprev_ref, shprev_ref = (refs.pop(0), refs.pop(0)) if l > 0 else (None, None)
    y_ref, sout_ref, shout_ref = refs.pop(0), refs.pop(0), refs.pop(0)
    s_scr, prev_scr, at_s, rt_s, bi_s, ki_s, bd_s, kd_s, v_s, lpl_s, yr_s, bon_s, gate_s = refs
    sb, tt, _ = p_ref.shape
    rows = sb * tt
    nseg = chunk // seg
    tstep = pl.program_id(1)

    @pl.when(tstep == 0)
    def _():
        if zero_init:
            s_scr[...] = jnp.zeros_like(s_scr)
            prev_scr[...] = jnp.zeros_like(prev_scr)
        else:
            s_scr[...] = s0_ref[...]
            prev_scr[:, :, 0:RW_COLS] = sh0_ref[...]
            prev_scr[:, :, RW_COLS:RW_PAD] = jnp.zeros((sb, 1, RW_PAD - RW_COLS), F32)

    vec = vec_ref[...]
    w0, a0, k_k, k_a, r_k, ln_g, ln_b = [vec[i:i + 1, :] for i in range(7)]
    ones_bd = ones_ref[...]

    rw3 = p_ref[...]
    rw = rw3.reshape(rows, RW_PAD)
    prev_rows = jnp.broadcast_to(prev_scr[...], (sb, tt, RW_PAD)).reshape(rows, RW_PAD)
    prev = jnp.where(_local_row(rw.shape, tt) == 0, prev_rows, pltpu.roll(rw, 1, axis=0))
    prev_scr[...] = rw3[:, tt - 1:tt, :]
    rwm = rw + (prev - rw) * mu_ref[...]
    r, k, v = rwm[:, 0:W], rwm[:, W:2 * W], rwm[:, 2 * W:3 * W]
    lr = rwm[:, 3 * W:3 * W + 256]
    lane = lax.broadcasted_iota(jnp.int32, lr.shape, 1)
    act = jnp.where(lane < 64, jnp.tanh(lr), jnp.where(lane < 128, lr, _sigmoid(lr)))
    up = jnp.dot(_bf(act), wup_ref[...], preferred_element_type=F32)
    log_w = -RW_DECAY * _sigmoid(w0 + up[:, 0:W])
    a = _sigmoid(a0 + up[:, W:2 * W])
    gate_s[...] = up[:, 2 * W:3 * W]
    kk = k * k_k
    kk = kk / jnp.maximum(jnp.sqrt(_headsum(kk * kk, ones_bd)), 1e-12)
    k2 = k * (1.0 + (a - 1.0) * k_a)
    kka = kk * a
    bon_s[...] = _headsum(r * k2 * r_k, ones_bd) * v
    v_s[...] = v
    te = te_ref[...]
    scans = [_cumsum_last(log_w[i * chunk:(i + 1) * chunk], te) for i in range(rows // chunk)]
    logp = jnp.concatenate([s[0] for s in scans], axis=0) if len(scans) > 1 else scans[0][0]
    lpl = jnp.concatenate([s[1] for s in scans], axis=0) if len(scans) > 1 else scans[0][1]
    lpl_s[...] = lpl
    at_s[...] = -kk * jnp.exp(logp - log_w)
    rt_s[...] = r * jnp.exp(logp)
    einv = jnp.exp(-logp)
    bi_s[...] = kka * einv
    ki_s[...] = k2 * einv
    elast = jnp.exp(lpl - logp)
    bd_s[...] = kka * elast
    kd_s[...] = k2 * elast

    def body_long(ch, carry):
        rss = [pl.ds(pl.multiple_of(q * tt + ch * chunk, chunk), chunk) for q in range(sb)]
        wide = lambda ref: jnp.concatenate([ref[rs, :] for rs in rss], axis=1)
        pc = jnp.exp(jnp.concatenate([lpl_s[pl.ds(rs.start, 1), :] for rs in rss], axis=1))
        ys, states = _rwkv_recur(wide(at_s), wide(rt_s), wide(bi_s), wide(ki_s), wide(bd_s), wide(kd_s),
                                 wide(v_s), [s_scr[q, h] for q in range(sb) for h in range(RW_H)], pc, seg)
        for q in range(sb):
            for h in range(RW_H):
                s_scr[q, h] = states[q * RW_H + h]
            yr_s[rss[q], :] = jnp.concatenate(ys[q * RW_H:(q + 1) * RW_H], axis=1)
        return carry

    def all_short():
        nch = rows // chunk
        rss = [slice(c * chunk, (c + 1) * chunk) for c in range(nch)]
        grp = [slice(c * nseg, (c + 1) * nseg) for c in range(nch)]
        wide = lambda ref: jnp.concatenate([ref[rs, :] for rs in rss], axis=1)
        pc = jnp.exp(jnp.concatenate([lpl_s[rs, :].reshape(nseg, seg, W)[:, 0:1, :] for rs in rss], axis=2))
        ys, states = _rwkv_recur(wide(at_s), wide(rt_s), wide(bi_s), wide(ki_s), wide(bd_s), wide(kd_s),
                                 wide(v_s), [s_scr[g, h] for g in grp for h in range(RW_H)], pc, seg)
        for c, g in enumerate(grp):
            for h in range(RW_H):
                s_scr[g, h] = states[c * RW_H + h]
            yr_s[rss[c], :] = jnp.concatenate(ys[c * RW_H:(c + 1) * RW_H], axis=1)

    if nseg == 1:
        lax.fori_loop(0, tt // chunk, body_long, 0)
    else:
        all_short()

    y = yr_s[...]
    mean = _headsum(y, ones_bd) * (1.0 / RW_D)
    yc = y - mean
    var = _headsum(yc * yc, ones_bd) * (1.0 / RW_D)
    yn = yc * lax.rsqrt(var + RW_GN_EPS) * ln_g + ln_b
    y_ref[...] = ((yn + bon_s[...]) * gate_s[...]).reshape(sb, tt, W)

    @pl.when(tstep == pl.num_programs(1) - 1)
    def _():
        _emit_state(l, sout_ref, sprev_ref, s_scr[...])
        _emit_state(l, shout_ref, shprev_ref, prev_scr[:, :, 0:RW_COLS])


def _rwkv_call(p, s_in, sh_in, prev, mu, vec, wup, ones_bd, te, l, rows, nlong, chunk, seg):
    bsz, t, _ = p.shape
    sb, tt = _rec_tile(bsz, t, rows, nlong)
    zero_init = s_in is None
    s_specs = _state_specs(l, sb, (RW_H, RW_D, RW_D))
    sh_specs = _state_specs(l, sb, (1, RW_COLS))
    args = ([p] + ([] if zero_init else [s_in, sh_in]) + [mu, vec, wup, ones_bd, te]
            + (list(prev) if l > 0 else []))
    specs = ([pl.BlockSpec((sb, tt, RW_PAD), lambda i, j: (i, j, 1))]
             + ([] if zero_init else [s_specs[0], sh_specs[0]])
             + [pl.BlockSpec((1, RW_PAD), lambda i, j: (0, 0)),
                pl.BlockSpec((8, W), lambda i, j: (0, 0)),
                pl.BlockSpec((256, 3 * W), lambda i, j: (0, 0)),
                pl.BlockSpec((W, W), lambda i, j: (0, 0)),
                pl.BlockSpec(te.shape, lambda i, j: (0, 0))]
             + ([s_specs[1], sh_specs[1]] if l > 0 else []))
    tile_scr = [pltpu.VMEM((sb * tt, W), F32) for _ in range(11)]
    return pl.pallas_call(
        functools.partial(_rwkv_kernel, (l, chunk, seg, zero_init)),
        out_shape=(jax.ShapeDtypeStruct((bsz, t, W), F32),
                   jax.ShapeDtypeStruct((l + 1, bsz, RW_H, RW_D, RW_D), F32),
                   jax.ShapeDtypeStruct((l + 1, bsz, 1, RW_COLS), F32)),
        grid=(bsz // sb, t // tt),
        in_specs=specs,
        out_specs=(pl.BlockSpec((sb, tt, W), lambda i, j: (i, j, 0)), s_specs[2], sh_specs[2]),
        scratch_shapes=[pltpu.VMEM((sb, RW_H, RW_D, RW_D), F32), pltpu.VMEM((sb, 1, RW_PAD), F32)] + tile_scr,
        compiler_params=_cparams(("arbitrary", "arbitrary")),
        name="rwkv7",
    )(*args)


def _merge_kernel(yhg_ref, yrw_ref, ycf_ref, ylr_ref, x_ref, sc1_ref, sh1_ref, g1_ref, sc2_ref, sh2_ref,
                  nmix_ref, nmlp_ref, wg_ref, bg_ref, wb_ref, wo_ref, x1_ref, h2_ref):
    sb, tt, _ = x_ref.shape
    tm = sb * tt
    x = x_ref[...]
    h = _bf(_adaln(x, nmix_ref[...], sc1_ref[...], sh1_ref[...]).reshape(tm, D))
    z = jnp.zeros((tm, D), F32)
    for b, y_ref in enumerate((yhg_ref, yrw_ref, ycf_ref, ylr_ref)):
        bo = jnp.dot(_bf(y_ref[...].reshape(tm, W)), wb_ref[b], preferred_element_type=F32)
        logit = jnp.dot(h, wg_ref[:, b * D:(b + 1) * D], preferred_element_type=F32) + bg_ref[:, b * D:(b + 1) * D]
        z = z + _sigmoid(logit) * bo
    out = jnp.dot(_bf(z), wo_ref[...], preferred_element_type=F32).reshape(sb, tt, D)
    x1 = x + g1_ref[...] * out
    x1_ref[...] = x1
    h2_ref[...] = _bf(_adaln(x1, nmlp_ref[...], sc2_ref[...], sh2_ref[...]))


def _merge_call(ys, x, mods, nmix, nmlp, wg, bg, wb, wo, rows=512):
    bsz, t, _ = x.shape
    sb, tt = _tile(bsz, t, rows)
    tok = lambda width: pl.BlockSpec((sb, tt, width), lambda i, j: (i, j, 0))
    seq = pl.BlockSpec((sb, 1, D), lambda i, j: (i, 0, 0))
    const = lambda shape: pl.BlockSpec(shape, lambda i, j: (0,) * len(shape), pipeline_mode=pl.Buffered(1))
    return pl.pallas_call(
        _merge_kernel,
        out_shape=(jax.ShapeDtypeStruct((bsz, t, D), F32),
                   jax.ShapeDtypeStruct((bsz, t, D), BF16)),
        grid=(bsz // sb, t // tt),
        in_specs=[tok(W), tok(W), tok(W), tok(W), tok(D), seq, seq, seq, seq, seq,
                  const((1, D)), const((1, D)), const((D, 4 * D)), const((1, 4 * D)),
                  const((4, W, D)), const((D, D))],
        out_specs=(tok(D), tok(D)),
        compiler_params=_cparams(("arbitrary", "arbitrary")),
        name="merge",
    )(*ys, x, *mods, nmix, nmlp, wg, bg, wb, wo)


def _mlp_kernel(final, h2_ref, x1_ref, g2_ref, fg_ref, w1_ref, w2_ref, o_ref, acc):
    sb, tt, _ = x1_ref.shape
    k = pl.program_id(2)

    @pl.when(k == 0)
    def _():
        acc[...] = jnp.zeros_like(acc)

    hid = jnp.dot(h2_ref[...].reshape(sb * tt, D), w1_ref[...], preferred_element_type=F32)
    act = jnp.square(jnp.maximum(hid, 0.0))
    acc[...] += jnp.dot(_bf(act), w2_ref[...], preferred_element_type=F32)

    @pl.when(k == pl.num_programs(2) - 1)
    def _():
        x2 = x1_ref[...] + g2_ref[...] * acc[...].reshape(sb, tt, D)
        if final:
            ms = jnp.mean(x2 * x2, axis=-1, keepdims=True)
            x2 = x2 * lax.rsqrt(ms + EPS) * fg_ref[...]
        o_ref[...] = x2


def _mlp_call(h2, x1, g2, fg, w1, w2, final, rows=1024, th=2048):
    bsz, t, _ = x1.shape
    sb, tt = _tile(bsz, t, rows)
    tok = pl.BlockSpec((sb, tt, D), lambda i, j, k: (i, j, 0))
    return pl.pallas_call(
        functools.partial(_mlp_kernel, final),
        out_shape=jax.ShapeDtypeStruct((bsz, t, D), F32),
        grid=(bsz // sb, t // tt, HID // th),
        in_specs=[tok, tok,
                  pl.BlockSpec((sb, 1, D), lambda i, j, k: (i, 0, 0)),
                  pl.BlockSpec((1, D), lambda i, j, k: (0, 0)),
                  pl.BlockSpec((D, th), lambda i, j, k: (0, k)),
                  pl.BlockSpec((th, D), lambda i, j, k: (k, 0))],
        out_specs=tok,
        scratch_shapes=[pltpu.VMEM((sb * tt, D), F32)],
        compiler_params=_cparams(("arbitrary", "arbitrary", "arbitrary")),
        name="mlp",
    )(h2, x1, g2, fg, w1, w2)


def _block_diag(w):
    n, c, d = w.shape
    eye = jnp.eye(n, dtype=w.dtype)
    return (eye[:, None, :, None] * w[:, :, None, :]).reshape(n * c, n * d)


def _prep_layer(wt, l):
    w_in = wt["w_in"][l]
    zpad = jnp.zeros((D, RW_PAD - RW_COLS), F32)
    wcat = jnp.concatenate([w_in[:, 0:2048], w_in[:, 2048:2048 + RW_COLS], zpad, w_in[:, 2048 + RW_COLS:]], axis=1)
    wup = jnp.zeros((256, 3 * W), F32)
    wup = wup.at[0:64, 0:W].set(wt["rw_w_up"][l])
    wup = wup.at[64:128, W:2 * W].set(wt["rw_a_up"][l])
    wup = wup.at[128:256, 2 * W:3 * W].set(wt["rw_g_up"][l])
    zrow = jnp.zeros((W,), F32)
    rw_vec = jnp.stack([wt["rw_w0"][l], wt["rw_a0"][l], wt["rw_k_k"][l], wt["rw_k_a"][l], wt["rw_r_k"][l],
                        wt["rw_ln_g"][l], wt["rw_ln_b"][l], zrow])
    mu = jnp.concatenate([wt["rw_mu"][l], jnp.zeros((RW_PAD - RW_COLS,), F32)])[None, :]
    cf_vec = jnp.stack([wt["cf_dw_b"][l], wt["cf_ln_g"][l], wt["cf_ln_b"][l]] + [zrow] * 5)
    cf_dw = jnp.concatenate([wt["cf_dw"][l], jnp.zeros((1, W), F32)], axis=0)
    lru_vec = jnp.stack([wt["lru_conv_b"][l], wt["lru_ba"][l], wt["lru_bx"][l], wt["lru_lambda"][l]] + [zrow] * 4)
    lru_cw = jnp.concatenate([wt["lru_conv_w"][l], jnp.zeros((8 - LRU_K, W), F32)], axis=0)
    wax = jnp.concatenate([_block_diag(wt["lru_wa"][l]), _block_diag(wt["lru_wx"][l])], axis=1)
    return dict(
        wcat=_bf(wcat), wup=_bf(wup), rw_vec=rw_vec, mu=mu, cf_vec=cf_vec, cf_dw=cf_dw,
        lru_vec=lru_vec, lru_cw=lru_cw, wax=_bf(wax), wg=_bf(wt["w_gate"][l]), bg=wt["b_gate"][l][None, :],
        wb=_bf(wt["w_branch"][l]), wo=_bf(wt["w_out"][l]), w1=_bf(wt["w_mlp1"][l]), w2=_bf(wt["w_mlp2"][l]),
        hg_gn=wt["hg_norm_g"][l][None, :], nmix=wt["norm_mix_g"][l][None, :], nmlp=wt["norm_mlp_g"][l][None, :],
    )


def _trunk(x, mod, states, wt, layers, ones_bd, mix_rows, rec_rows, nlong, chunk, seg):
    nl = len(layers)
    te = _scan_mats(chunk, seg)
    if states is None:
        s_hg = s_rw = s_shift = s_cf = s_lh = s_lc = None
    else:
        s_hg, s_rw, s_shift, s_cf, s_lh, s_lc = states
        s_shift = s_shift[:, :, None, :]
        s_lh = s_lh[:, :, None, :]
    fg = wt["norm_final_g"][None, :]
    n_hg = n_rw = n_sh = n_cf = n_lh = n_lc = None
    for l, lw in enumerate(layers):
        sh1, sc1, g1, sh2, sc2, g2 = [mod[l, i][:, None, :] for i in range(6)]
        conv_in = None if states is None else (s_cf, s_lh, s_lc)
        p, y_cf, y_lr, n_cf, n_lh, n_lc = _inproj_call(
            x, sc1, sh1, lw["nmix"], lw["wcat"], conv_in, (n_cf, n_lh, n_lc), lw["cf_dw"], lw["cf_vec"],
            lw["lru_cw"], lw["lru_vec"], lw["wax"], l, mix_rows)
        y_hg, n_hg = _hgrn_call(p, s_hg, n_hg, wt["hg_lower"], lw["hg_gn"], te, l, rec_rows, nlong, chunk, seg)
        y_rw, n_rw, n_sh = _rwkv_call(p, s_rw, s_shift, (n_rw, n_sh), lw["mu"], lw["rw_vec"], lw["wup"], ones_bd,
                                      te, l, rec_rows, nlong, chunk, seg)
        x1, h2 = _merge_call((y_hg, y_rw, y_cf, y_lr), x, (sc1, sh1, g1, sc2, sh2), lw["nmix"], lw["nmlp"],
                             lw["wg"], lw["bg"], lw["wb"], lw["wo"])
        x = _mlp_call(h2, x1, g2, fg, lw["w1"], lw["w2"], final=(l == nl - 1))
    return x, [n_hg, n_rw, n_sh[:, :, 0, :], n_cf, n_lh[:, :, 0, :], n_lc]


def _run(x_prompt, x_sample, sample_states, c_prompt, c_sample, wt):
    nl = wt["w_in"].shape[0]
    layers = [_prep_layer(wt, l) for l in range(nl)]
    head = jnp.arange(W, dtype=jnp.int32) // RW_D
    ones_bd = _bf((head[:, None] == head[None, :]).astype(F32))
    tp = x_prompt.shape[1]
    ts = x_sample.shape[1]
    rows_p = min(256, tp)
    chunk_p = min(64, tp)
    bp = x_prompt.shape[0]
    mod = _mod_call(jnp.concatenate([c_prompt, c_sample], axis=0), wt["ada_w"], wt["ada_b"])
    y_p, st_p = _trunk(x_prompt, mod[:, :, :bp], None, wt, layers, ones_bd, 2 * rows_p, rows_p // 4, 8, chunk_p,
                       chunk_p)
    y_s, st_s = _trunk(x_sample, mod[:, :, bp:], sample_states, wt, layers, ones_bd, rows_p, 16 * ts, 1, 8 * ts, ts)
    return (y_p, y_s, *st_p, *st_s)


def kernel(x_prompt, x_sample, state_hgrn, state_rwkv, state_rwkv_shift, state_conv, state_lru_h, state_lru_conv, c_prompt, c_sample, ada_w, ada_b, norm_mix_g, norm_mlp_g, norm_final_g, w_in, hg_lower, hg_norm_g, rw_mu, rw_w0, rw_w_up, rw_a0, rw_a_up, rw_g_up, rw_k_k, rw_k_a, rw_r_k, rw_ln_g, rw_ln_b, cf_dw, cf_dw_b, cf_ln_g, cf_ln_b, lru_conv_w, lru_conv_b, lru_wa, lru_ba, lru_wx, lru_bx, lru_lambda, w_branch, w_gate, b_gate, w_out, w_mlp1, w_mlp2):
    wt = dict(ada_w=ada_w, ada_b=ada_b, norm_mix_g=norm_mix_g, norm_mlp_g=norm_mlp_g,
              norm_final_g=norm_final_g, w_in=w_in, hg_lower=hg_lower, hg_norm_g=hg_norm_g, rw_mu=rw_mu,
              rw_w0=rw_w0, rw_w_up=rw_w_up, rw_a0=rw_a0, rw_a_up=rw_a_up, rw_g_up=rw_g_up, rw_k_k=rw_k_k,
              rw_k_a=rw_k_a, rw_r_k=rw_r_k, rw_ln_g=rw_ln_g, rw_ln_b=rw_ln_b, cf_dw=cf_dw, cf_dw_b=cf_dw_b,
              cf_ln_g=cf_ln_g, cf_ln_b=cf_ln_b, lru_conv_w=lru_conv_w, lru_conv_b=lru_conv_b, lru_wa=lru_wa,
              lru_ba=lru_ba, lru_wx=lru_wx, lru_bx=lru_bx, lru_lambda=lru_lambda, w_branch=w_branch,
              w_gate=w_gate, b_gate=b_gate, w_out=w_out, w_mlp1=w_mlp1, w_mlp2=w_mlp2)
    sample_states = (state_hgrn, state_rwkv, state_rwkv_shift, state_conv, state_lru_h, state_lru_conv)
    return _run(x_prompt, x_sample, sample_states, c_prompt, c_sample, wt)
```

```python
import functools
import math

import jax
import jax.numpy as jnp
from jax import lax
from jax.experimental import pallas as pl
from jax.experimental.pallas import tpu as pltpu

D = 1024
W = 512
HG_H = 4
HG_D = 128
RW_H = 8
RW_D = 64
RW_COLS = 1792
RW_PAD = 2048
CF_K = 31
LRU_K = 4
HID = 4096
EPS = 1e-6
RW_GN_EPS = 64e-5
RW_DECAY = 0.606531
LRU_C = 8.0
LANES = 128

P_COLS = 6144
SEQ_UNROLL = 2
VMEM_LIMIT = 56 * 1024 * 1024

F32 = jnp.float32
BF16 = jnp.bfloat16


def _bf(x):
    return x.astype(BF16)


def _mm(a, b):
    return jnp.dot(_bf(a), _bf(b), preferred_element_type=F32)


def _mm_nt(a, b):
    return lax.dot_general(_bf(a), _bf(b), (((1,), (1,)), ((), ())), preferred_element_type=F32)


def _mm_tn(a, b):
    return lax.dot_general(_bf(a), _bf(b), (((0,), (0,)), ((), ())), preferred_element_type=F32)


def _sigmoid(x):
    return 1.0 / (1.0 + jnp.exp(-x))


def _silu(x):
    return x * _sigmoid(x)


def _local_row(shape, seg):
    return jnp.bitwise_and(lax.broadcasted_iota(jnp.int32, shape, 0), seg - 1)


def _shift_rows(x, d, fill, seg):
    return jnp.where(_local_row(x.shape, seg) >= d, pltpu.roll(x, d, axis=0), fill)


def _scan_mats(chunk, seg):
    r = jnp.arange(chunk, dtype=jnp.int32)
    same = (r[:, None] // seg) == (r[None, :] // seg)
    tri = same & (r[None, :] <= r[:, None])
    return _bf(jnp.concatenate([tri, same], axis=0).astype(F32))


def _cumsum_last(x, te):
    c = x.shape[0]
    p1 = _bf(x)
    r1 = x - p1.astype(F32)
    p2 = _bf(r1)
    p3 = _bf(r1 - p2.astype(F32))
    out = (jnp.dot(te, p1, preferred_element_type=F32) + jnp.dot(te, p2, preferred_element_type=F32)
           + jnp.dot(te, p3, preferred_element_type=F32))
    return out[:c], out[c:]


def _seg_id(shape, axis, log2_seg):
    return lax.shift_right_logical(lax.broadcasted_iota(jnp.int32, shape, axis), log2_seg)


def _headsum(x, ones_bd):
    hi = _bf(x)
    lo = _bf(x - hi.astype(F32))
    return (jnp.dot(hi, ones_bd, preferred_element_type=F32)
            + jnp.dot(lo, ones_bd, preferred_element_type=F32))


def _cparams(sem):
    return pltpu.CompilerParams(dimension_semantics=sem, vmem_limit_bytes=VMEM_LIMIT)


def _tile(bsz, t, rows):
    if t >= rows:
        return 1, rows
    return min(bsz, rows // t), t


def _state_specs(l, sb, tail):
    zeros = (0,) * len(tail)
    in_spec = pl.BlockSpec((None, sb) + tail, lambda i, j: (l, i) + zeros)
    prev_spec = pl.BlockSpec((l, sb) + tail, lambda i, j: (0, i) + zeros)
    out_spec = pl.BlockSpec((l + 1, sb) + tail, lambda i, j: (0, i) + zeros)
    return in_spec, prev_spec, out_spec


def _emit_state(l, out_ref, prev_ref, new):
    for q in range(l):
        out_ref[q] = prev_ref[q]
    out_ref[l] = new


def _mod_kernel(c_ref, w_ref, b_ref, o_ref):
    c = c_ref[...]
    o_ref[0, 0] = _mm(_silu(c), w_ref[0]) + b_ref[0, 0]


def _mod_call(c, ada_w, ada_b):
    nl = ada_w.shape[0]
    bsz = c.shape[0]
    return pl.pallas_call(
        _mod_kernel,
        out_shape=jax.ShapeDtypeStruct((nl, 6, bsz, D), F32),
        grid=(nl, 6),
        in_specs=[
            pl.BlockSpec((bsz, D), lambda l, j: (0, 0)),
            pl.BlockSpec((1, D, D), lambda l, j: (l, 0, j)),
            pl.BlockSpec((1, 1, 1, D), lambda l, j: (l, j, 0, 0)),
        ],
        out_specs=pl.BlockSpec((1, 1, bsz, D), lambda l, j: (l, j, 0, 0)),
        compiler_params=_cparams(("arbitrary", "arbitrary")),
        name="adaln_mod",
    )(c, ada_w, ada_b.reshape(nl, 6, 1, D))


def _adaln(x, g, sc, sh):
    ms = jnp.mean(x * x, axis=-1, keepdims=True)
    y = x * lax.rsqrt(ms + EPS) * g
    return y * (1.0 + sc) + sh


CF_PAD = 32
CF_RB = 64
SUBLANES = 8
LRU_PAD = 8
REC_COLS = 4096
REC_PIECE = 1024


def _gelu_tanh(x):
    return 0.5 * x * (1.0 + jnp.tanh(0.7978845608028654 * (x + 0.044715 * (x * x * x))))


def _conformer_prep(pcf, s, slot, ext, shifted):
    tt = pcf.shape[0]
    span = tt + CF_PAD - SUBLANES
    ext[s, CF_PAD:CF_PAD + tt, :] = pcf[:, 0:W] * _sigmoid(pcf[:, W:2 * W])
    for q in range(1, SUBLANES):
        shifted[slot, q, 0:span, :] = ext[s, q:q + span, :]


def _rows(start, size):
    if isinstance(start, int):
        return pl.ds(start, size)
    return pl.ds(pl.multiple_of(start, SUBLANES), size)


def _conformer_rows(r0, rb, s, slot, ext, shifted, y_ref, dw, bias, ln_g, ln_b):
    off = CF_PAD - (CF_K - 1)
    acc = jnp.zeros((rb, W), F32) + bias
    for j in range(CF_K):
        a8, q = divmod(off + j, SUBLANES)
        rows = _rows(r0 + a8 * SUBLANES, rb)
        tap = ext[s, rows, :] if q == 0 else shifted[slot, q, rows, :]
        acc = acc + dw[j:j + 1, :] * tap
    mean = jnp.mean(acc, axis=-1, keepdims=True)
    xc = acc - mean
    var = jnp.mean(xc * xc, axis=-1, keepdims=True)
    yn = xc * lax.rsqrt(var + 1e-5) * ln_g + ln_b
    y_ref[s, _rows(r0, rb), :] = _silu(yn)


def _group_prefix(a, b):
    d = 1
    while d < SUBLANES:
        a_s = _shift_rows(a, d, 1.0, SUBLANES)
        b_s = _shift_rows(b, d, 0.0, SUBLANES)
        b = a * b_s + b
        a = a * a_s
        d *= 2
    return a, b


def _linear_scan(a, b, h):
    n = a.shape[0]
    a, b = _group_prefix(a, b)
    out = []
    for g in range(n // SUBLANES):
        hs = a[g * SUBLANES:(g + 1) * SUBLANES] * h + b[g * SUBLANES:(g + 1) * SUBLANES]
        h = hs[SUBLANES - 1:SUBLANES, :]
        out.append(hs)
    return (jnp.concatenate(out, axis=0) if len(out) > 1 else out[0]), h


def _lru_conv(s, tt, ext, cw, cb):
    off = LRU_PAD - (LRU_K - 1)
    xc = jnp.zeros((tt, W), F32) + cb
    for j in range(LRU_K):
        xc = xc + cw[j:j + 1, :] * ext[s, off + j:off + j + tt, :]
    return xc


def _lru_rows(xc, gate, h, ba, bx, sp, wax):
    pre = jnp.dot(_bf(xc), wax, preferred_element_type=F32)
    rg = _sigmoid(pre[:, 0:W] + ba)
    ig = _sigmoid(pre[:, W:2 * W] + bx)
    a = jnp.exp(-LRU_C * rg * sp)
    hs, h = _linear_scan(a, jnp.sqrt(1.0 - a * a) * (ig * xc), h)
    return hs * _gelu_tanh(gate), h


def _inproj_kernel(cfg, *refs):
    l, zero_init = cfg
    refs = list(refs)
    x_ref, sc_ref, sh_ref, g_ref, w_ref = [refs.pop(0) for _ in range(5)]
    cf0_ref, lh0_ref, lc0_ref = (None,) * 3 if zero_init else [refs.pop(0) for _ in range(3)]
    dw_ref, cfv_ref, cw_ref, lrv_ref, wax_ref = [refs.pop(0) for _ in range(5)]
    cfp_ref, lhp_ref, lcp_ref = [refs.pop(0) for _ in range(3)] if l > 0 else (None,) * 3
    p_ref, ycf_ref, ylr_ref, cfo_ref, lho_ref, lco_ref = [refs.pop(0) for _ in range(6)]
    cf_ext, shifted, lr_ext, hcar, pc_scr, h_scr, xc_scr = refs
    sb, tt, _ = x_ref.shape
    tm = sb * tt
    tstep = pl.program_id(1)
    cf_off = CF_PAD - (CF_K - 1)
    lr_off = LRU_PAD - (LRU_K - 1)

    @pl.when(tstep == 0)
    def _():
        if zero_init:
            cf_ext[:, 0:CF_PAD, :] = jnp.zeros((sb, CF_PAD, W), F32)
            lr_ext[:, 0:LRU_PAD, :] = jnp.zeros((sb, LRU_PAD, W), F32)
            hcar[...] = jnp.zeros_like(hcar)
        else:
            cf_ext[:, cf_off:CF_PAD, :] = cf0_ref[...]
            lr_ext[:, lr_off:LRU_PAD, :] = lc0_ref[...]
            hcar[...] = lh0_ref[...]

    h_scr[...] = _bf(_adaln(x_ref[...], g_ref[...], sc_ref[...], sh_ref[...]).reshape(tm, D))
    pc_scr[...] = jnp.dot(h_scr[...], w_ref[:, REC_COLS:P_COLS], preferred_element_type=F32).reshape(sb, tt, 4 * W)

    def rec_piece(n):
        cols = slice(n * REC_PIECE, (n + 1) * REC_PIECE)
        p_ref[:, :, cols] = jnp.dot(h_scr[...], w_ref[:, cols], preferred_element_type=F32).reshape(sb, tt, REC_PIECE)

    cfv = cfv_ref[...]
    dw = dw_ref[...]
    lrv = lrv_ref[...]
    lam = lrv[3:4]
    sp = jnp.maximum(-lam, 0.0) + jnp.log1p(jnp.exp(-jnp.abs(lam)))
    cw = cw_ref[...]
    wax = wax_ref[...]
    nslot = shifted.shape[0]

    rb = min(CF_RB, tt)

    def prep(s, slot):
        _conformer_prep(pc_scr[s, :, 0:2 * W], s, slot, cf_ext, shifted)
        lr_ext[s, LRU_PAD:LRU_PAD + tt, :] = pc_scr[s, :, 2 * W:3 * W]
        xc_scr[s] = _lru_conv(s, tt, lr_ext, cw, lrv[0:1])

    def row_block(s, slot, r0, hstate):
        _conformer_rows(r0, rb, s, slot, cf_ext, shifted, ycf_ref, dw, cfv[0:1], cfv[1:2], cfv[2:3])
        rows = _rows(r0, rb)
        y, hstate = _lru_rows(xc_scr[s, rows, :], pc_scr[s, rows, 3 * W:4 * W], hstate, lrv[1:2], lrv[2:3], sp, wax)
        ylr_ref[s, rows, :] = y
        return hstate

    def finish(s, hstate):
        hcar[s] = hstate
        cf_tail = cf_ext[s, tt:tt + CF_PAD, :]
        cf_ext[s, 0:CF_PAD, :] = cf_tail
        lr_tail = lr_ext[s, tt:tt + LRU_PAD, :]
        lr_ext[s, 0:LRU_PAD, :] = lr_tail

    for n in range(REC_COLS // REC_PIECE):
        rec_piece(n)

    def all_short():
        pc = pc_scr[...]
        cf_ext[:, CF_PAD:CF_PAD + tt, :] = pc[:, :, 0:W] * _sigmoid(pc[:, :, W:2 * W])
        acc = jnp.zeros((sb, tt, W), F32) + cfv[0:1]
        for j in range(CF_K):
            acc = acc + dw[j:j + 1, :] * cf_ext[:, cf_off + j:cf_off + j + tt, :]
        mean = jnp.mean(acc, axis=-1, keepdims=True)
        xn = acc - mean
        var = jnp.mean(xn * xn, axis=-1, keepdims=True)
        ycf_ref[...] = _silu(xn * lax.rsqrt(var + 1e-5) * cfv[1:2] + cfv[2:3])
        lr_ext[:, LRU_PAD:LRU_PAD + tt, :] = pc[:, :, 2 * W:3 * W]
        xc = jnp.zeros((sb, tt, W), F32) + lrv[0:1]
        for j in range(LRU_K):
            xc = xc + cw[j:j + 1, :] * lr_ext[:, lr_off + j:lr_off + j + tt, :]
        xc = xc.reshape(tm, W)
        pre = jnp.dot(_bf(xc), wax, preferred_element_type=F32)
        rg = _sigmoid(pre[:, 0:W] + lrv[1:2])
        ig = _sigmoid(pre[:, W:2 * W] + lrv[2:3])
        a = jnp.exp(-LRU_C * rg * sp)
        a, b = _group_prefix(a, jnp.sqrt(1.0 - a * a) * (ig * xc))
        hs = a.reshape(sb, tt, W) * hcar[...] + b.reshape(sb, tt, W)
        hcar[...] = hs[:, tt - 1:tt, :]
        ylr_ref[...] = hs * _gelu_tanh(pc[:, :, 3 * W:4 * W])
        cf_tail = cf_ext[:, tt:tt + CF_PAD, :]
        cf_ext[:, 0:CF_PAD, :] = cf_tail
        lr_tail = lr_ext[:, tt:tt + LRU_PAD, :]
        lr_ext[:, 0:LRU_PAD, :] = lr_tail

    def body(i, carry):
        for slot in range(nslot):
            s = i * nslot + slot
            prep(s, slot)
            hstate = hcar[s]
            for r0 in range(0, tt, rb):
                hstate = row_block(s, slot, r0, hstate)
            finish(s, hstate)
        return carry

    if tt == SUBLANES:
        all_short()
    else:
        lax.fori_loop(0, sb // nslot, body, 0)

    @pl.when(tstep == pl.num_programs(1) - 1)
    def _():
        _emit_state(l, cfo_ref, cfp_ref, cf_ext[:, cf_off:CF_PAD, :])
        _emit_state(l, lho_ref, lhp_ref, hcar[...])
        _emit_state(l, lco_ref, lcp_ref, lr_ext[:, lr_off:LRU_PAD, :])


def _inproj_call(x, sc, sh, g, wcat, conv_in, conv_prev, dw, cfv, cw, lrv, wax, l, rows=256):
    bsz, t, _ = x.shape
    sb, tt = _tile(bsz, t, rows)
    zero_init = conv_in is None
    tails = ((CF_K - 1, W), (1, W), (LRU_K - 1, W))
    sspecs = [_state_specs(l, sb, tail) for tail in tails]
    tok = lambda width: pl.BlockSpec((sb, tt, width), lambda i, j: (i, j, 0))
    seq = pl.BlockSpec((sb, 1, D), lambda i, j: (i, 0, 0))
    const = lambda shape: pl.BlockSpec(shape, lambda i, j: (0,) * len(shape), pipeline_mode=pl.Buffered(1))
    args = ([x, sc, sh, g, wcat] + ([] if zero_init else list(conv_in)) + [dw, cfv, cw, lrv, wax]
            + (list(conv_prev) if l > 0 else []))
    specs = ([tok(D), seq, seq, const((1, D)), const((D, P_COLS))]
             + ([] if zero_init else [sp[0] for sp in sspecs])
             + [const((32, W)), const((8, W)), const((8, W)), const((8, W)), const((W, 2 * W))]
             + ([sp[1] for sp in sspecs] if l > 0 else []))
    nslot = min(sb, SEQ_UNROLL)
    return pl.pallas_call(
        functools.partial(_inproj_kernel, (l, zero_init)),
        out_shape=(jax.ShapeDtypeStruct((bsz, t, REC_COLS), F32),
                   jax.ShapeDtypeStruct((bsz, t, W), F32),
                   jax.ShapeDtypeStruct((bsz, t, W), F32))
        + tuple(jax.ShapeDtypeStruct((l + 1, bsz) + tail, F32) for tail in tails),
        grid=(bsz // sb, t // tt),
        in_specs=specs,
        out_specs=(tok(REC_COLS), tok(W), tok(W)) + tuple(sp[2] for sp in sspecs),
        scratch_shapes=[pltpu.VMEM((sb, CF_PAD + tt, W), F32),
                        pltpu.VMEM((nslot, SUBLANES, CF_PAD + tt, W), F32),
                        pltpu.VMEM((sb, LRU_PAD + tt, W), F32),
                        pltpu.VMEM((sb, 1, W), F32),
                        pltpu.VMEM((sb, tt, 4 * W), F32),
                        pltpu.VMEM((sb * tt, D), BF16),
                        pltpu.VMEM((sb, tt, W), F32)],
        compiler_params=_cparams(("arbitrary", "arbitrary")),
        name="inproj",
    )(*args)


HG_SUB = 16


def _hgrn_chunk(qr, fz, iv, og, lb, gn, te, states, seg):
    c, width = qr.shape
    nseg = c // seg
    heads = range(width // HG_D)
    q = _silu(qr)
    f = lb + (1.0 - lb) * _sigmoid(fz)
    logf = jnp.log(f)
    kf = (1.0 - lb) * _sigmoid(-fz)
    b, b_last = _cumsum_last(logf, te)
    qe = q * jnp.exp(b)
    kdec = kf * jnp.exp(b_last - b)
    e_last = jnp.exp(b_last)
    sl = [slice(h * HG_D, (h + 1) * HG_D) for h in heads]

    if nseg == 1:
        sub = min(HG_SUB, c)
        nsub = c // sub
        o_inter = [_mm_nt(qe[:, sl[h]], states[h]) for h in heads]
        new_states = [states[h] * e_last[0:1, sl[h]] + _mm_tn(iv[:, sl[h]], kdec[:, sl[h]]) for h in heads]
        pieces = [[] for _ in heads]
        for i in range(nsub):
            r0 = i * sub
            m = b[r0 - 1:r0, :] if i > 0 else jnp.zeros((1, width), F32)
            qs = q[r0:r0 + sub] * jnp.exp(b[r0:r0 + sub] - m)
            kd = kf[r0:r0 + sub] * jnp.exp(jnp.minimum(m - b[r0:r0 + sub], 80.0))
            if i > 0:
                kall = jnp.concatenate([kf[:r0] * jnp.exp(m - b[:r0]), kd], axis=0)
            else:
                kall = kd
            row = lax.broadcasted_iota(jnp.int32, (sub, r0 + sub), 0) + r0
            col = lax.broadcasted_iota(jnp.int32, (sub, r0 + sub), 1)
            causal = col <= row
            scs = [jnp.where(causal, _mm_nt(qs[:, sl[h]], kall[:, sl[h]]), 0.0) for h in heads]
            for h in heads:
                pieces[h].append(_mm(scs[h], iv[:r0 + sub, sl[h]]))
        o_intra = [jnp.concatenate(pieces[h], axis=0) if nsub > 1 else pieces[h][0] for h in heads]
    else:
        lg = int(math.log2(seg))
        kd = kf * jnp.exp(jnp.minimum(-b, 80.0))
        rr = lax.broadcasted_iota(jnp.int32, (c, c), 0)
        cc = lax.broadcasted_iota(jnp.int32, (c, c), 1)
        causal = (lax.shift_right_logical(rr, lg) == lax.shift_right_logical(cc, lg)) & (cc <= rr)
        shape_x = (c, nseg * HG_D)
        blockmask = _seg_id(shape_x, 0, lg) == _seg_id(shape_x, 1, 7)
        first = blockmask & (_local_row(shape_x, seg) == 0)
        spread = lambda v, mask: jnp.where(mask, jnp.concatenate([v] * nseg, axis=1), 0.0)
        ones_cv = jnp.ones((3 * c, HG_D), BF16)
        sflat = [states[h].reshape(nseg * HG_D, HG_D) for h in heads]
        o_inter = [_mm(spread(qe[:, sl[h]], blockmask), sflat[h]) for h in heads]
        scs = [jnp.where(causal, _mm_nt(qe[:, sl[h]], kd[:, sl[h]]), 0.0) for h in heads]
        o_intra = [_mm(scs[h], iv[:, sl[h]]) for h in heads]
        new_states = []
        for h in heads:
            ex = spread(e_last[:, sl[h]], first)
            p1 = _bf(ex)
            r1 = ex - p1.astype(F32)
            p2 = _bf(r1)
            p3 = _bf(r1 - p2.astype(F32))
            dcol = lax.dot_general(jnp.concatenate([p1, p2, p3], axis=0), ones_cv, (((0,), (0,)), ((), ())),
                                   preferred_element_type=F32)
            upd = _mm_tn(spread(kdec[:, sl[h]], blockmask), iv[:, sl[h]])
            new_states.append((sflat[h] * dcol + upd).reshape(nseg, HG_D, HG_D))
    outs = []
    for h in heads:
        o = o_inter[h] + o_intra[h]
        outs.append(o * lax.rsqrt(jnp.mean(o * o, axis=-1, keepdims=True) + EPS))
    y = jnp.concatenate(outs, axis=1) * gn * _silu(og)
    return y, new_states


def _hgrn_kernel(cfg, *refs):
    l, chunk, seg, zero_init = cfg
    refs = list(refs)
    p_ref = refs.pop(0)
    s0_ref = None if zero_init else refs.pop(0)
    lower_ref, gn_ref, te_ref = refs.pop(0), refs.pop(0), refs.pop(0)
    prev_ref = refs.pop(0) if l > 0 else None
    y_ref, sout_ref, s_scr = refs
    sb, tt, _ = p_ref.shape
    nseg = chunk // seg
    tstep = pl.program_id(1)

    @pl.when(tstep == 0)
    def _():
        if zero_init:
            s_scr[...] = jnp.zeros_like(s_scr)
        elif nseg > 1:
            s_scr[...] = s0_ref[...]
        else:
            def init(i, carry):
                for h in range(HG_H):
                    s_scr[i, h] = s0_ref[i, h].T
                return carry
            lax.fori_loop(0, sb, init, 0)

    low = lower_ref[...]
    e = jnp.exp(low - jnp.max(low, axis=0, keepdims=True))
    sm = e / jnp.sum(e, axis=0, keepdims=True)
    lb = jnp.sum(sm[:l + 1], axis=0, keepdims=True) - sm[0:1]
    gn = gn_ref[...]
    te = te_ref[...]

    def body_long(i, carry):
        rs = pl.ds(pl.multiple_of(i * chunk, chunk), chunk)
        cols = [jnp.concatenate([p_ref[q, rs, j * W:(j + 1) * W] for q in range(sb)], axis=1) for j in range(4)]
        wide = lambda v: jnp.concatenate([v] * sb, axis=1)
        y, states = _hgrn_chunk(*cols, wide(lb), wide(gn), te,
                                [s_scr[q, h] for q in range(sb) for h in range(HG_H)], seg)
        for q in range(sb):
            for h in range(HG_H):
                s_scr[q, h] = states[q * HG_H + h]
            y_ref[q, rs, :] = y[:, q * W:(q + 1) * W]
        return carry

    def all_short():
        nch = sb * tt // chunk
        grp = [slice(c * nseg, (c + 1) * nseg) for c in range(nch)]
        blks = [p_ref[g].reshape(chunk, 4 * W) for g in grp]
        cols = [jnp.concatenate([b[:, j * W:(j + 1) * W] for b in blks], axis=1) for j in range(4)]
        wide = lambda v: jnp.concatenate([v] * nch, axis=1)
        y, states = _hgrn_chunk(*cols, wide(lb), wide(gn), te,
                                [s_scr[g, h] for g in grp for h in range(HG_H)], seg)
        for c, g in enumerate(grp):
            for h in range(HG_H):
                s_scr[g, h] = states[c * HG_H + h]
            y_ref[g] = y[:, c * W:(c + 1) * W].reshape(nseg, seg, W)

    if nseg == 1:
        lax.fori_loop(0, tt // chunk, body_long, 0)
    else:
        all_short()

    @pl.when(tstep == pl.num_programs(1) - 1)
    def _():
        for q in range(l):
            sout_ref[q] = prev_ref[q]
        if nseg > 1:
            sout_ref[l] = s_scr[...]
        else:
            def fin(i, carry):
                for h in range(HG_H):
                    sout_ref[l, i, h] = s_scr[i, h].T
                return carry
            lax.fori_loop(0, sb, fin, 0)


def _rec_tile(bsz, t, rows, nlong):
    sb, tt = _tile(bsz, t, rows)
    return (min(nlong, bsz), tt) if t >= rows else (sb, tt)


def _hgrn_call(p, s_in, prev, hg_lower, gn, te, l, rows, nlong, chunk, seg):
    bsz, t, _ = p.shape
    sb, tt = _rec_tile(bsz, t, rows, nlong)
    zero_init = s_in is None
    in_spec, prev_spec, out_spec = _state_specs(l, sb, (HG_H, HG_D, HG_D))
    args = [p] + ([] if zero_init else [s_in]) + [hg_lower, gn, te] + ([prev] if l > 0 else [])
    specs = ([pl.BlockSpec((sb, tt, 4 * W), lambda i, j: (i, j, 0))] + ([] if zero_init else [in_spec])
             + [pl.BlockSpec(hg_lower.shape, lambda i, j: (0, 0)), pl.BlockSpec((1, W), lambda i, j: (0, 0)),
                pl.BlockSpec(te.shape, lambda i, j: (0, 0))]
             + ([prev_spec] if l > 0 else []))
    return pl.pallas_call(
        functools.partial(_hgrn_kernel, (l, chunk, seg, zero_init)),
        out_shape=(jax.ShapeDtypeStruct((bsz, t, W), F32),
                   jax.ShapeDtypeStruct((l + 1, bsz, HG_H, HG_D, HG_D), F32)),
        grid=(bsz // sb, t // tt),
        in_specs=specs,
        out_specs=(pl.BlockSpec((sb, tt, W), lambda i, j: (i, j, 0)), out_spec),
        scratch_shapes=[pltpu.VMEM((sb, HG_H, HG_D, HG_D), F32)],
        compiler_params=_cparams(("arbitrary", "arbitrary")),
        name="hgrn2",
    )(*args)


def _pick64(full, rowseg):
    nb = full.shape[1] // LANES
    half = lax.shift_right_logical(rowseg, 1)
    sel = full[:, 0:LANES]
    for j in range(1, nb):
        sel = jnp.where(half == j, full[:, j * LANES:(j + 1) * LANES], sel)
    sel = jnp.where(jnp.bitwise_and(rowseg, 1) == 1, pltpu.roll(sel, RW_D, axis=1), sel)
    return sel[:, 0:RW_D]


def _rwkv_recur(at, rt, bi, ki, bd, kd, v, states, pc, seg):
    c = at.shape[0]
    nseg = c // seg
    lg = int(math.log2(seg))
    rr = lax.broadcasted_iota(jnp.int32, (c, c), 0)
    cc = lax.broadcasted_iota(jnp.int32, (c, c), 1)
    if nseg == 1:
        strict = rr > cc
        incl = rr >= cc
    else:
        same = lax.shift_right_logical(rr, lg) == lax.shift_right_logical(cc, lg)
        strict = same & (rr > cc)
        incl = same & (rr >= cc)
        r2 = jnp.bitwise_and(lax.broadcasted_iota(jnp.int32, (2 * c, LANES), 0), c - 1)
        rowseg = lax.shift_right_logical(r2, lg)
        rb = jnp.bitwise_and(lax.broadcasted_iota(jnp.int32, (2 * c, nseg * RW_D), 0), c - 1)
        blockmask = lax.shift_right_logical(rb, lg) == _seg_id((2 * c, nseg * RW_D), 1, 6)
    heads = range(at.shape[1] // RW_D)
    sl = [slice(h * RW_D, (h + 1) * RW_D) for h in heads]
    ar = [jnp.concatenate([at[:, sl[h]], rt[:, sl[h]]], axis=0) for h in heads]
    bk = [jnp.concatenate([bi[:, sl[h]], ki[:, sl[h]]], axis=0) for h in heads]
    g = [_mm_nt(ar[h], bk[h]) for h in heads]
    if nseg == 1:
        a_s = [_mm_nt(ar[h], states[h]) for h in heads]
    else:
        a_s = [_pick64(_mm_nt(ar[h], states[h].reshape(nseg * RW_D, RW_D)), rowseg) for h in heads]
    vh = [v[:, sl[h]] for h in heads]
    lp = [jnp.where(strict, g[h][:c, :c], 0.0) for h in heads]
    x = [a_s[h][:c] + _mm(jnp.where(strict, g[h][:c, c:], 0.0), vh[h]) for h in heads]
    for j in range(lg):
        x = [x[h] + _mm(lp[h], x[h]) for h in heads]
        if j < lg - 1:
            lp = [_mm(lp[h], lp[h]) for h in heads]
    uv = [jnp.concatenate([x[h], vh[h]], axis=0) for h in heads]
    mrbk = [jnp.concatenate([jnp.where(incl, g[h][c:, :c], 0.0), jnp.where(incl, g[h][c:, c:], 0.0)], axis=1)
            for h in heads]
    ys = [a_s[h][c:] + _mm(mrbk[h], uv[h]) for h in heads]
    bkd = [jnp.concatenate([bd[:, sl[h]], kd[:, sl[h]]], axis=0) for h in heads]
    if nseg == 1:
        new_states = [states[h] * pc[:, sl[h]] + _mm_tn(uv[h], bkd[h]) for h in heads]
    else:
        new_states = []
        for h in heads:
            u2 = jnp.concatenate([uv[h], uv[h]], axis=1)
            uvexp = jnp.where(blockmask, jnp.concatenate([u2] * (nseg // 2), axis=1), 0.0)
            upd = _mm_tn(uvexp, bkd[h]).reshape(nseg, RW_D, RW_D)
            new_states.append(states[h] * pc[:, :, sl[h]] + upd)
    return ys, new_states


def _rwkv_kernel(cfg, *refs):
    l, chunk, seg, zero_init = cfg
    refs = list(refs)
    p_ref = refs.pop(0)
    s0_ref, sh0_ref = (None, None) if zero_init else (refs.pop(0), refs.pop(0))
    mu_ref, vec_ref, wup_ref, ones_ref, te_ref = [refs.pop(0) for _ in range(5)]
    sprev_ref, shprev_ref = (refs.pop(0), refs.pop(0)) if l > 0 else (None, None)
    y_ref, sout_ref, shout_ref = refs.pop(0), refs.pop(0), refs.pop(0)
    s_scr, prev_scr, at_s, rt_s, bi_s, ki_s, bd_s, kd_s, v_s, lpl_s, yr_s, bon_s, gate_s = refs
    sb, tt, _ = p_ref.shape
    rows = sb * tt
    nseg = chunk // seg
    tstep = pl.program_id(1)

    @pl.when(tstep == 0)
    def _():
        if zero_init:
            s_scr[...] = jnp.zeros_like(s_scr)
            prev_scr[...] = jnp.zeros_like(prev_scr)
        else:
            s_scr[...] = s0_ref[...]
            prev_scr[:, :, 0:RW_COLS] = sh0_ref[...]
            prev_scr[:, :, RW_COLS:RW_PAD] = jnp.zeros((sb, 1, RW_PAD - RW_COLS), F32)

    vec = vec_ref[...]
    w0, a0, k_k, k_a, r_k, ln_g, ln_b = [vec[i:i + 1, :] for i in range(7)]
    ones_bd = ones_ref[...]

    rw3 = p_ref[...]
    rw = rw3.reshape(rows, RW_PAD)
    prev_rows = jnp.broadcast_to(prev_scr[...], (sb, tt, RW_PAD)).reshape(rows, RW_PAD)
    prev = jnp.where(_local_row(rw.shape, tt) == 0, prev_rows, pltpu.roll(rw, 1, axis=0))
    prev_scr[...] = rw3[:, tt - 1:tt, :]
    rwm = rw + (prev - rw) * mu_ref[...]
    r, k, v = rwm[:, 0:W], rwm[:, W:2 * W], rwm[:, 2 * W:3 * W]
    lr = rwm[:, 3 * W:3 * W + 256]
    lane = lax.broadcasted_iota(jnp.int32, lr.shape, 1)
    act = jnp.where(lane < 64, jnp.tanh(lr), jnp.where(lane < 128, lr, _sigmoid(lr)))
    up = jnp.dot(_bf(act), wup_ref[...], preferred_element_type=F32)
    log_w = -RW_DECAY * _sigmoid(w0 + up[:, 0:W])
    a = _sigmoid(a0 + up[:, W:2 * W])
    gate_s[...] = up[:, 2 * W:3 * W]
    kk = k * k_k
    kk = kk / jnp.maximum(jnp.sqrt(_headsum(kk * kk, ones_bd)), 1e-12)
    k2 = k * (1.0 + (a - 1.0) * k_a)
    kka = kk * a
    bon_s[...] = _headsum(r * k2 * r_k, ones_bd) * v
    v_s[...] = v
    te = te_ref[...]
    scans = [_cumsum_last(log_w[i * chunk:(i + 1) * chunk], te) for i in range(rows // chunk)]
    logp = jnp.concatenate([s[0] for s in scans], axis=0) if len(scans) > 1 else scans[0][0]
    lpl = jnp.concatenate([s[1] for s in scans], axis=0) if len(scans) > 1 else scans[0][1]
    lpl_s[...] = lpl
    at_s[...] = -kk * jnp.exp(logp - log_w)
    rt_s[...] = r * jnp.exp(logp)
    einv = jnp.exp(-logp)
    bi_s[...] = kka * einv
    ki_s[...] = k2 * einv
    elast = jnp.exp(lpl - logp)
    bd_s[...] = kka * elast
    kd_s[...] = k2 * elast

    def body_long(ch, carry):
        rss = [pl.ds(pl.multiple_of(q * tt + ch * chunk, chunk), chunk) for q in range(sb)]
        wide = lambda ref: jnp.concatenate([ref[rs, :] for rs in rss], axis=1)
        pc = jnp.exp(jnp.concatenate([lpl_s[pl.ds(rs.start, 1), :] for rs in rss], axis=1))
        ys, states = _rwkv_recur(wide(at_s), wide(rt_s), wide(bi_s), wide(ki_s), wide(bd_s), wide(kd_s),
                                 wide(v_s), [s_scr[q, h] for q in range(sb) for h in range(RW_H)], pc, seg)
        for q in range(sb):
            for h in range(RW_H):
                s_scr[q, h] = states[q * RW_H + h]
            yr_s[rss[q], :] = jnp.concatenate(ys[q * RW_H:(q + 1) * RW_H], axis=1)
        return carry

    def all_short():
        nch = rows // chunk
        rss = [slice(c * chunk, (c + 1) * chunk) for c in range(nch)]
        grp = [slice(c * nseg, (c + 1) * nseg) for c in range(nch)]
        wide = lambda ref: jnp.concatenate([ref[rs, :] for rs in rss], axis=1)
        pc = jnp.exp(jnp.concatenate([lpl_s[rs, :].reshape(nseg, seg, W)[:, 0:1, :] for rs in rss], axis=2))
        ys, states = _rwkv_recur(wide(at_s), wide(rt_s), wide(bi_s), wide(ki_s), wide(bd_s), wide(kd_s),
                                 wide(v_s), [s_scr[g, h] for g in grp for h in range(RW_H)], pc, seg)
        for c, g in enumerate(grp):
            for h in range(RW_H):
                s_scr[g, h] = states[c * RW_H + h]
            yr_s[rss[c], :] = jnp.concatenate(ys[c * RW_H:(c + 1) * RW_H], axis=1)

    if nseg == 1:
        lax.fori_loop(0, tt // chunk, body_long, 0)
    else:
        all_short()

    y = yr_s[...]
    mean = _headsum(y, ones_bd) * (1.0 / RW_D)
    yc = y - mean
    var = _headsum(yc * yc, ones_bd) * (1.0 / RW_D)
    yn = yc * lax.rsqrt(var + RW_GN_EPS) * ln_g + ln_b
    y_ref[...] = ((yn + bon_s[...]) * gate_s[...]).reshape(sb, tt, W)

    @pl.when(tstep == pl.num_programs(1) - 1)
    def _():
        _emit_state(l, sout_ref, sprev_ref, s_scr[...])
        _emit_state(l, shout_ref, shprev_ref, prev_scr[:, :, 0:RW_COLS])


def _rwkv_call(p, s_in, sh_in, prev, mu, vec, wup, ones_bd, te, l, rows, nlong, chunk, seg):
    bsz, t, _ = p.shape
    sb, tt = _rec_tile(bsz, t, rows, nlong)
    zero_init = s_in is None
    s_specs = _state_specs(l, sb, (RW_H, RW_D, RW_D))
    sh_specs = _state_specs(l, sb, (1, RW_COLS))
    args = ([p] + ([] if zero_init else [s_in, sh_in]) + [mu, vec, wup, ones_bd, te]
            + (list(prev) if l > 0 else []))
    specs = ([pl.BlockSpec((sb, tt, RW_PAD), lambda i, j: (i, j, 1))]
             + ([] if zero_init else [s_specs[0], sh_specs[0]])
             + [pl.BlockSpec((1, RW_PAD), lambda i, j: (0, 0)),
                pl.BlockSpec((8, W), lambda i, j: (0, 0)),
                pl.BlockSpec((256, 3 * W), lambda i, j: (0, 0)),
                pl.BlockSpec((W, W), lambda i, j: (0, 0)),
                pl.BlockSpec(te.shape, lambda i, j: (0, 0))]
             + ([s_specs[1], sh_specs[1]] if l > 0 else []))
    tile_scr = [pltpu.VMEM((sb * tt, W), F32) for _ in range(11)]
    return pl.pallas_call(
        functools.partial(_rwkv_kernel, (l, chunk, seg, zero_init)),
        out_shape=(jax.ShapeDtypeStruct((bsz, t, W), F32),
                   jax.ShapeDtypeStruct((l + 1, bsz, RW_H, RW_D, RW_D), F32),
                   jax.ShapeDtypeStruct((l + 1, bsz, 1, RW_COLS), F32)),
        grid=(bsz // sb, t // tt),
        in_specs=specs,
        out_specs=(pl.BlockSpec((sb, tt, W), lambda i, j: (i, j, 0)), s_specs[2], sh_specs[2]),
        scratch_shapes=[pltpu.VMEM((sb, RW_H, RW_D, RW_D), F32), pltpu.VMEM((sb, 1, RW_PAD), F32)] + tile_scr,
        compiler_params=_cparams(("arbitrary", "arbitrary")),
        name="rwkv7",
    )(*args)


def _merge_kernel(yhg_ref, yrw_ref, ycf_ref, ylr_ref, x_ref, sc1_ref, sh1_ref, g1_ref, sc2_ref, sh2_ref,
                  nmix_ref, nmlp_ref, wg_ref, bg_ref, wb_ref, wo_ref, x1_ref, h2_ref):
    sb, tt, _ = x_ref.shape
    tm = sb * tt
    x = x_ref[...]
    h = _bf(_adaln(x, nmix_ref[...], sc1_ref[...], sh1_ref[...]).reshape(tm, D))
    z = jnp.zeros((tm, D), F32)
    for b, y_ref in enumerate((yhg_ref, yrw_ref, ycf_ref, ylr_ref)):
        bo = jnp.dot(_bf(y_ref[...].reshape(tm, W)), wb_ref[b], preferred_element_type=F32)
        logit = jnp.dot(h, wg_ref[:, b * D:(b + 1) * D], preferred_element_type=F32) + bg_ref[:, b * D:(b + 1) * D]
        z = z + _sigmoid(logit) * bo
    out = jnp.dot(_bf(z), wo_ref[...], preferred_element_type=F32).reshape(sb, tt, D)
    x1 = x + g1_ref[...] * out
    x1_ref[...] = x1
    h2_ref[...] = _bf(_adaln(x1, nmlp_ref[...], sc2_ref[...], sh2_ref[...]))


def _merge_call(ys, x, mods, nmix, nmlp, wg, bg, wb, wo, rows=512):
    bsz, t, _ = x.shape
    sb, tt = _tile(bsz, t, rows)
    tok = lambda width: pl.BlockSpec((sb, tt, width), lambda i, j: (i, j, 0))
    seq = pl.BlockSpec((sb, 1, D), lambda i, j: (i, 0, 0))
    const = lambda shape: pl.BlockSpec(shape, lambda i, j: (0,) * len(shape), pipeline_mode=pl.Buffered(1))
    return pl.pallas_call(
        _merge_kernel,
        out_shape=(jax.ShapeDtypeStruct((bsz, t, D), F32),
                   jax.ShapeDtypeStruct((bsz, t, D), BF16)),
        grid=(bsz // sb, t // tt),
        in_specs=[tok(W), tok(W), tok(W), tok(W), tok(D), seq, seq, seq, seq, seq,
                  const((1, D)), const((1, D)), const((D, 4 * D)), const((1, 4 * D)),
                  const((4, W, D)), const((D, D))],
        out_specs=(tok(D), tok(D)),
        compiler_params=_cparams(("arbitrary", "arbitrary")),
        name="merge",
    )(*ys, x, *mods, nmix, nmlp, wg, bg, wb, wo)


def _mlp_kernel(final, h2_ref, x1_ref, g2_ref, fg_ref, w1_ref, w2_ref, o_ref):
    sb, tt, _ = x1_ref.shape
    hid = jnp.dot(h2_ref[...].reshape(sb * tt, D), w1_ref[...], preferred_element_type=F32)
    act = jnp.square(jnp.maximum(hid, 0.0))
    out = jnp.dot(_bf(act), w2_ref[...], preferred_element_type=F32)
    x2 = x1_ref[...] + g2_ref[...] * out.reshape(sb, tt, D)
    if final:
        ms = jnp.mean(x2 * x2, axis=-1, keepdims=True)
        x2 = x2 * lax.rsqrt(ms + EPS) * fg_ref[...]
    o_ref[...] = x2


def _mlp_call(h2, x1, g2, fg, w1, w2, final, rows=512):
    bsz, t, _ = x1.shape
    sb, tt = _tile(bsz, t, rows)
    tok = pl.BlockSpec((sb, tt, D), lambda i, j: (i, j, 0))
    const = lambda shape: pl.BlockSpec(shape, lambda i, j: (0,) * len(shape), pipeline_mode=pl.Buffered(1))
    return pl.pallas_call(
        functools.partial(_mlp_kernel, final),
        out_shape=jax.ShapeDtypeStruct((bsz, t, D), F32),
        grid=(bsz // sb, t // tt),
        in_specs=[tok, tok,
                  pl.BlockSpec((sb, 1, D), lambda i, j: (i, 0, 0)),
                  const((1, D)), const((D, HID)), const((HID, D))],
        out_specs=tok,
        compiler_params=_cparams(("arbitrary", "arbitrary")),
        name="mlp",
    )(h2, x1, g2, fg, w1, w2)


def _block_diag(w):
    n, c, d = w.shape
    eye = jnp.eye(n, dtype=w.dtype)
    return (eye[:, None, :, None] * w[:, :, None, :]).reshape(n * c, n * d)


def _prep_layer(wt, l):
    w_in = wt["w_in"][l]
    zpad = jnp.zeros((D, RW_PAD - RW_COLS), F32)
    wcat = jnp.concatenate([w_in[:, 0:2048], w_in[:, 2048:2048 + RW_COLS], zpad, w_in[:, 2048 + RW_COLS:]], axis=1)
    wup = jnp.zeros((256, 3 * W), F32)
    wup = wup.at[0:64, 0:W].set(wt["rw_w_up"][l])
    wup = wup.at[64:128, W:2 * W].set(wt["rw_a_up"][l])
    wup = wup.at[128:256, 2 * W:3 * W].set(wt["rw_g_up"][l])
    zrow = jnp.zeros((W,), F32)
    rw_vec = jnp.stack([wt["rw_w0"][l], wt["rw_a0"][l], wt["rw_k_k"][l], wt["rw_k_a"][l], wt["rw_r_k"][l],
                        wt["rw_ln_g"][l], wt["rw_ln_b"][l], zrow])
    mu = jnp.concatenate([wt["rw_mu"][l], jnp.zeros((RW_PAD - RW_COLS,), F32)])[None, :]
    cf_vec = jnp.stack([wt["cf_dw_b"][l], wt["cf_ln_g"][l], wt["cf_ln_b"][l]] + [zrow] * 5)
    cf_dw = jnp.concatenate([wt["cf_dw"][l], jnp.zeros((1, W), F32)], axis=0)
    lru_vec = jnp.stack([wt["lru_conv_b"][l], wt["lru_ba"][l], wt["lru_bx"][l], wt["lru_lambda"][l]] + [zrow] * 4)
    lru_cw = jnp.concatenate([wt["lru_conv_w"][l], jnp.zeros((8 - LRU_K, W), F32)], axis=0)
    wax = jnp.concatenate([_block_diag(wt["lru_wa"][l]), _block_diag(wt["lru_wx"][l])], axis=1)
    return dict(
        wcat=_bf(wcat), wup=_bf(wup), rw_vec=rw_vec, mu=mu, cf_vec=cf_vec, cf_dw=cf_dw,
        lru_vec=lru_vec, lru_cw=lru_cw, wax=_bf(wax), wg=_bf(wt["w_gate"][l]), bg=wt["b_gate"][l][None, :],
        wb=_bf(wt["w_branch"][l]), wo=_bf(wt["w_out"][l]), w1=_bf(wt["w_mlp1"][l]), w2=_bf(wt["w_mlp2"][l]),
        hg_gn=wt["hg_norm_g"][l][None, :], nmix=wt["norm_mix_g"][l][None, :], nmlp=wt["norm_mlp_g"][l][None, :],
    )


def _trunk(x, mod, states, wt, layers, ones_bd, mix_rows, rec_rows, nlong, chunk, seg):
    nl = len(layers)
    te = _scan_mats(chunk, seg)
    if states is None:
        s_hg = s_rw = s_shift = s_cf = s_lh = s_lc = None
    else:
        s_hg, s_rw, s_shift, s_cf, s_lh, s_lc = states
        s_shift = s_shift[:, :, None, :]
        s_lh = s_lh[:, :, None, :]
    fg = wt["norm_final_g"][None, :]
    n_hg = n_rw = n_sh = n_cf = n_lh = n_lc = None
    for l, lw in enumerate(layers):
        sh1, sc1, g1, sh2, sc2, g2 = [mod[l, i][:, None, :] for i in range(6)]
        conv_in = None if states is None else (s_cf, s_lh, s_lc)
        p, y_cf, y_lr, n_cf, n_lh, n_lc = _inproj_call(
            x, sc1, sh1, lw["nmix"], lw["wcat"], conv_in, (n_cf, n_lh, n_lc), lw["cf_dw"], lw["cf_vec"],
            lw["lru_cw"], lw["lru_vec"], lw["wax"], l, mix_rows)
        y_hg, n_hg = _hgrn_call(p, s_hg, n_hg, wt["hg_lower"], lw["hg_gn"], te, l, rec_rows, nlong, chunk, seg)
        y_rw, n_rw, n_sh = _rwkv_call(p, s_rw, s_shift, (n_rw, n_sh), lw["mu"], lw["rw_vec"], lw["wup"], ones_bd,
                                      te, l, rec_rows, nlong, chunk, seg)
        x1, h2 = _merge_call((y_hg, y_rw, y_cf, y_lr), x, (sc1, sh1, g1, sc2, sh2), lw["nmix"], lw["nmlp"],
                             lw["wg"], lw["bg"], lw["wb"], lw["wo"])
        x = _mlp_call(h2, x1, g2, fg, lw["w1"], lw["w2"], final=(l == nl - 1))
    return x, [n_hg, n_rw, n_sh[:, :, 0, :], n_cf, n_lh[:, :, 0, :], n_lc]


def _run(x_prompt, x_sample, sample_states, c_prompt, c_sample, wt):
    nl = wt["w_in"].shape[0]
    layers = [_prep_layer(wt, l) for l in range(nl)]
    head = jnp.arange(W, dtype=jnp.int32) // RW_D
    ones_bd = _bf((head[:, None] == head[None, :]).astype(F32))
    tp = x_prompt.shape[1]
    ts = x_sample.shape[1]
    rows_p = min(256, tp)
    chunk_p = min(64, tp)
    bp = x_prompt.shape[0]
    mod = _mod_call(jnp.concatenate([c_prompt, c_sample], axis=0), wt["ada_w"], wt["ada_b"])
    y_p, st_p = _trunk(x_prompt, mod[:, :, :bp], None, wt, layers, ones_bd, 2 * rows_p, rows_p // 4, 8, chunk_p,
                       chunk_p)
    y_s, st_s = _trunk(x_sample, mod[:, :, bp:], sample_states, wt, layers, ones_bd, rows_p, 16 * ts, 1, 8 * ts, ts)
    return (y_p, y_s, *st_p, *st_s)


def kernel(x_prompt, x_sample, state_hgrn, state_rwkv, state_rwkv_shift, state_conv, state_lru_h, state_lru_conv, c_prompt, c_sample, ada_w, ada_b, norm_mix_g, norm_mlp_g, norm_final_g, w_in, hg_lower, hg_norm_g, rw_mu, rw_w0, rw_w_up, rw_a0, rw_a_up, rw_g_up, rw_k_k, rw_k_a, rw_r_k, rw_ln_g, rw_ln_b, cf_dw, cf_dw_b, cf_ln_g, cf_ln_b, lru_conv_w, lru_conv_b, lru_wa, lru_ba, lru_wx, lru_bx, lru_lambda, w_branch, w_gate, b_gate, w_out, w_mlp1, w_mlp2):
    wt = dict(ada_w=ada_w, ada_b=ada_b, norm_mix_g=norm_mix_g, norm_mlp_g=norm_mlp_g,
              norm_final_g=norm_final_g, w_in=w_in, hg_lower=hg_lower, hg_norm_g=hg_norm_g, rw_mu=rw_mu,
              rw_w0=rw_w0, rw_w_up=rw_w_up, rw_a0=rw_a0, rw_a_up=rw_a_up, rw_g_up=rw_g_up, rw_k_k=rw_k_k,
              rw_k_a=rw_k_a, rw_r_k=rw_r_k, rw_ln_g=rw_ln_g, rw_ln_b=rw_ln_b, cf_dw=cf_dw, cf_dw_b=cf_dw_b,
              cf_ln_g=cf_ln_g, cf_ln_b=cf_ln_b, lru_conv_w=lru_conv_w, lru_conv_b=lru_conv_b, lru_wa=lru_wa,
              lru_ba=lru_ba, lru_wx=lru_wx, lru_bx=lru_bx, lru_lambda=lru_lambda, w_branch=w_branch,
              w_gate=w_gate, b_gate=b_gate, w_out=w_out, w_mlp1=w_mlp1, w_mlp2=w_mlp2)
    sample_states = (state_hgrn, state_rwkv, state_rwkv_shift, state_conv, state_lru_h, state_lru_conv)
    return _run(x_prompt, x_sample, sample_states, c_prompt, c_sample, wt)
```
